```python
import functools
import jax
import jax.numpy as jnp
from jax import lax
import numpy as np

D_MODEL = 1024
BATCH = 2
SEQ = 8192
DEPTH = 4
DEC_BATCH = 128
DEC_SEQ = 4
PAST_LEN = 8192
PAGE_SIZE = 128

HEAD_DIM = 64
N_MIXERS = 3
Q_BLOCK = 128
RMS_EPS = 1e-6
NEG = -1e30
A_HEADS = D_MODEL // HEAD_DIM
A_KV = A_HEADS // 8
A_G = A_HEADS // A_KV
A_WINDOW = 128
A_IN = 2 * A_HEADS * HEAD_DIM + 2 * A_KV * HEAD_DIM
B_HEADS = D_MODEL // HEAD_DIM
B_KV = B_HEADS // 4
B_G = B_HEADS // B_KV
B_CMP = 32
B_SEL = 64
B_TOPN = 16
B_WINDOW = 512
B_IN = 2 * B_HEADS * HEAD_DIM + 3 * 2 * B_KV * HEAD_DIM + 3 * B_HEADS
C_GROUPS = ((128, 1), (512, 4), (2048, 16))
C_NG = len(C_GROUPS)
C_HEADS = D_MODEL // HEAD_DIM
C_KV = C_HEADS // 8
C_G = C_HEADS // C_KV
C_IN = C_NG * (C_HEADS + 2 * C_KV) * HEAD_DIM + C_HEADS * HEAD_DIM

kernel_name = 'hybrid_swa_nsa_dilated_step'


def round_up(n, m):
    return -(-n // m) * m


def pad_seq(x, length):
    return jnp.pad(x, [(0, 0), (0, length - x.shape[1])] + [(0, 0)] * (x.ndim - 2))


def rms_norm(x, g):
    xf = x.astype(jnp.float32)
    y = xf * lax.rsqrt(jnp.mean(xf * xf, axis=-1, keepdims=True) + RMS_EPS)
    return (y * g.astype(jnp.float32)).astype(x.dtype)


def alibi_slopes(n_heads, n_kv):
    h = jnp.arange(1, n_heads + 1, dtype=jnp.float32)
    return jnp.exp2(-8.0 * h / n_heads).reshape(n_kv, n_heads // n_kv)


def masked_softmax(s, valid, sink=None):
    s = jnp.where(valid, s, NEG)
    m = jnp.max(s, axis=-1, keepdims=True)
    if sink is not None:
        m = jnp.maximum(m, sink)
    e = jnp.where(valid, jnp.exp(s - m), 0.0)
    den = jnp.sum(e, axis=-1, keepdims=True)
    if sink is not None:
        den = den + jnp.exp(sink - m)
    den = jnp.where(den > 0, den, 1.0)
    return e / den, (m + jnp.log(den))[..., 0]


def attend(q, k, v, dist, valid, slopes, sinks=None):
    s = jnp.einsum('...tkgd,...lkd->...kgtl', q, k, preferred_element_type=jnp.float32) * (HEAD_DIM ** -0.5)
    s = s - slopes[:, :, None, None] * dist.astype(jnp.float32)[..., None, None, :, :]
    sink = None if sinks is None else sinks.astype(jnp.float32)[:, :, None, None]
    p, lse = masked_softmax(s, valid[..., None, None, :, :], sink)
    o = jnp.einsum('...kgtl,...lkd->...tkgd', p.astype(v.dtype), v)
    return o, p, jnp.moveaxis(lse, -1, -3)


def banded(q, k, v, window, stride, slopes, sinks=None):
    n, s = q.shape[:2]
    span = round_up(window // stride, Q_BLOCK)
    sp = round_up(s, Q_BLOCK)
    nblk = sp // Q_BLOCK
    qb = pad_seq(q, sp).reshape(n, nblk, Q_BLOCK, *q.shape[2:])
    idx = jnp.arange(nblk)[:, None] * Q_BLOCK + jnp.arange(span + Q_BLOCK)[None, :]
    k_pos = idx - span
    q_pos = jnp.arange(sp).reshape(nblk, Q_BLOCK)

    def keys(x):
        return jnp.pad(x, [(0, 0), (span, sp - s)] + [(0, 0)] * (x.ndim - 2))[:, idx]

    dist = (q_pos[:, :, None] - k_pos[:, None, :]) * stride
    valid = (dist >= 0) & (dist <= window) & (k_pos[:, None, :] >= 0)
    o, _, lse = attend(qb, keys(k), keys(v), dist, valid, slopes, sinks)
    o = o.reshape(n, sp, *q.shape[2:])[:, :s]
    lse = lse.reshape(n, sp, *lse.shape[3:])[:, :s]
    return o, lse


def window_step(q, kv_new, buf, window, slopes, sinks=None):
    t = q.shape[1]
    lb = buf.shape[1]
    kv = jnp.concatenate([buf, kv_new.astype(buf.dtype)], axis=1)
    q_pos = PAST_LEN + jnp.arange(t)
    k_pos = PAST_LEN - lb + jnp.arange(lb + t)
    dist = q_pos[:, None] - k_pos[None, :]
    valid = (dist >= 0) & (dist <= window) & (k_pos[None, :] >= 0)
    o, _, _ = attend(q, kv[..., 0, :], kv[..., 1, :], dist, valid, slopes, sinks)
    return o, kv[:, t:]


def dilated_step(q, kv_new, buf, window, dilation, slopes):
    t = q.shape[1]
    lb = buf.shape[1]
    kv = jnp.concatenate([buf, kv_new.astype(buf.dtype)], axis=1)
    i = jnp.arange(t)[:, None]
    j = jnp.arange(window // dilation + 1)[None, :]
    idx = lb + i - j * dilation
    kg = kv[:, jnp.maximum(idx, 0)]
    k_pos = PAST_LEN - lb + idx
    dist = jnp.broadcast_to(j * dilation, idx.shape)[:, None, :]
    valid = (k_pos >= 0)[:, None, :]
    o, _, lse = attend(q[:, :, None], kg[..., 0, :], kg[..., 1, :], dist, valid, slopes)
    return o[:, :, 0], lse[:, :, 0], kv[:, t:]


def to_sub(x, d):
    n, s = x.shape[:2]
    return x.reshape(n, s // d, d, *x.shape[2:]).swapaxes(1, 2).reshape(n * d, s // d, *x.shape[2:])


def from_sub(x, d, n):
    s = x.shape[1] * d
    return x.reshape(n, d, s // d, *x.shape[2:]).swapaxes(1, 2).reshape(n, s, *x.shape[2:])


def out_proj(o, gate, w_out):
    n, t = o.shape[:2]
    return (o.reshape(n, t, -1) * jax.nn.silu(gate)) @ w_out


def nsa_pool(rows, pool_w):
    n, l = rows.shape[:2]
    blocks = rows.reshape(n, l // B_CMP, B_CMP, *rows.shape[2:])
    return jnp.einsum('nbckxd,xc->nbkxd', blocks, pool_w.astype(rows.dtype))


def gather_rows(kv, pos):
    n = jnp.arange(kv.shape[0]).reshape(-1, 1, 1, 1, 1)
    h = jnp.arange(kv.shape[2]).reshape(1, -1, 1, 1, 1)
    return kv[n, pos, h]


def gather_paged(pool, page_table, new_kv, pos):
    n = jnp.arange(page_table.shape[0]).reshape(-1, 1, 1, 1, 1)
    h = jnp.arange(pool.shape[2]).reshape(1, -1, 1, 1, 1)
    page = page_table[n, jnp.clip(pos // PAGE_SIZE, 0, page_table.shape[1] - 1)]
    past = pool[page, pos % PAGE_SIZE, h]
    new = new_kv[n, jnp.clip(pos - PAST_LEN, 0, new_kv.shape[1] - 1), h].astype(past.dtype)
    return jnp.where((pos < PAST_LEN)[..., None, None], past, new)


def nsa_core(q, q_pos, cmp_kv, gather_sel, slopes):
    n, t, kvh, g, _ = q.shape
    nbc = cmp_kv.shape[1]
    ratio = B_SEL // B_CMP
    nbs = nbc // ratio
    n_sel = min(B_TOPN, nbs)
    end = (jnp.arange(nbc) + 1) * B_CMP - 1
    dist_c = q_pos[:, None] - end[None, :]
    o_c, p_c, _ = attend(q, cmp_kv[..., 0, :], cmp_kv[..., 1, :], dist_c, dist_c >= 0, slopes)
    imp = jnp.sum(p_c, axis=2).reshape(n, kvh, t, nbs, ratio).sum(-1)
    blk = jnp.arange(nbs)[None, :]
    cur = (q_pos // B_SEL)[:, None]
    forced = ((blk == 0) | (blk == cur) | (blk == cur - 1)).astype(jnp.float32)
    score = jnp.where(blk * B_SEL <= q_pos[:, None], imp + forced * (g + 1.0), -1.0)
    _, idx = lax.top_k(score, n_sel)
    pos = idx[..., None] * B_SEL + jnp.arange(B_SEL)
    rows = gather_sel(pos).reshape(n, kvh, t, n_sel * B_SEL, 2, HEAD_DIM)
    dist_s = q_pos[:, None] - pos.reshape(n, kvh, t, n_sel * B_SEL)
    s = jnp.einsum('ntkgd,nktld->nkgtl', q, rows[..., 0, :], preferred_element_type=jnp.float32) * (HEAD_DIM ** -0.5)
    s = s - slopes[None, :, :, None, None] * dist_s.astype(jnp.float32)[:, :, None]
    p_s, _ = masked_softmax(s, (dist_s >= 0)[:, :, None])
    o_s = jnp.einsum('nkgtl,nktld->ntkgd', p_s.astype(rows.dtype), rows[..., 1, :])
    return o_c, o_s


def mixer_a(hp, hs, buf, w_in, w_out, sinks):
    slopes = alibi_slopes(A_HEADS, A_KV)
    sk = sinks.reshape(A_KV, A_G)
    nq = A_HEADS * HEAD_DIM
    nkv = 2 * A_KV * HEAD_DIM

    def proj(h):
        n, t = h.shape[:2]
        z = h @ w_in
        return (z[..., :nq].reshape(n, t, A_KV, A_G, HEAD_DIM),
                z[..., nq:nq + nkv].reshape(n, t, A_KV, 2, HEAD_DIM),
                z[..., nq + nkv:])

    qp, kvp, gp = proj(hp)
    op, _ = banded(qp, kvp[..., 0, :], kvp[..., 1, :], A_WINDOW, 1, slopes, sk)
    qs, kvs, gs = proj(hs)
    os_, new_buf = window_step(qs, kvs, buf, A_WINDOW, slopes, sk)
    s = kvp.shape[1]
    keep = min(A_WINDOW, s)
    return out_proj(op, gp, w_out), out_proj(os_, gs, w_out), [kvp[:, s - keep:]], [new_buf]


def mixer_b(hp, hs, cache_cmp, cache_sel, buf_win, page_table, w_in, w_out, cmp_pool):
    slopes = alibi_slopes(B_HEADS, B_KV)
    nq = B_HEADS * HEAD_DIM
    nkv = 2 * B_KV * HEAD_DIM

    def proj(h):
        n, t = h.shape[:2]
        z = h @ w_in
        q = z[..., :nq].reshape(n, t, B_KV, B_G, HEAD_DIM)
        c = z[..., nq:nq + nkv].reshape(n, t, B_KV, 2, HEAD_DIM)
        sl = z[..., nq + nkv:nq + 2 * nkv].reshape(n, t, B_KV, 2, HEAD_DIM)
        wn = z[..., nq + 2 * nkv:nq + 3 * nkv].reshape(n, t, B_KV, 2, HEAD_DIM)
        gate = z[..., nq + 3 * nkv:2 * nq + 3 * nkv]
        bg = jax.nn.sigmoid(z[..., 2 * nq + 3 * nkv:].astype(jnp.float32)).reshape(n, t, 3, B_KV, B_G, 1)
        return q, c, sl, wn, gate, bg

    def merge(bg, o_c, o_s, o_w):
        o = (bg[:, :, 0] * o_c.astype(jnp.float32) + bg[:, :, 1] * o_s.astype(jnp.float32)
             + bg[:, :, 2] * o_w.astype(jnp.float32))
        return o.astype(o_c.dtype)

    qp, c_p, s_p, w_p, gp, bgp = proj(hp)
    n, s = qp.shape[:2]
    lk = round_up(s, B_SEL)
    cmp_p = nsa_pool(pad_seq(c_p, lk), cmp_pool)
    gather_p = functools.partial(gather_rows, pad_seq(s_p, lk))
    sq = round_up(s, Q_BLOCK)
    nblk = sq // Q_BLOCK
    qb = pad_seq(qp, sq).reshape(n, nblk, Q_BLOCK, B_KV, B_G, HEAD_DIM).swapaxes(0, 1)
    posb = jnp.arange(sq).reshape(nblk, Q_BLOCK)
    oc_b, os_b = lax.map(lambda a: nsa_core(a[0], a[1], cmp_p, gather_p, slopes), (qb, posb))

    def unblock(o):
        return o.swapaxes(0, 1).reshape(n, sq, B_KV, B_G, HEAD_DIM)[:, :s]

    ow_p, _ = banded(qp, w_p[..., 0, :], w_p[..., 1, :], B_WINDOW, 1, slopes)
    op = merge(bgp, unblock(oc_b), unblock(os_b), ow_p)
    qs, c_s, s_s, w_s, gs, bgs = proj(hs)
    m, t = qs.shape[:2]
    past_cmp = cache_cmp[page_table].reshape(m, PAST_LEN, B_KV, 2, HEAD_DIM)
    new_cmp = pad_seq(c_s.astype(past_cmp.dtype), round_up(t, B_SEL))
    cmp_s = jnp.concatenate([nsa_pool(past_cmp, cmp_pool), nsa_pool(new_cmp, cmp_pool)], axis=1)
    gather_s = functools.partial(gather_paged, cache_sel, page_table, s_s)
    oc_s, os_s = nsa_core(qs, PAST_LEN + jnp.arange(t), cmp_s, gather_s, slopes)
    ow_s, new_win = window_step(qs, w_s, buf_win, B_WINDOW, slopes)
    osm = merge(bgs, oc_s, os_s, ow_s)
    keep = min(B_WINDOW, s)
    return (out_proj(op, gp, w_out), out_proj(osm, gs, w_out),
            [c_p, s_p, w_p[:, s - keep:]], [c_s, s_s, new_win])


def mixer_c(hp, hs, bufs, w_in, w_out):
    slopes = alibi_slopes(C_HEADS, C_KV)
    nq = C_NG * C_HEADS * HEAD_DIM
    nkv = C_NG * 2 * C_KV * HEAD_DIM

    def proj(h):
        n, t = h.shape[:2]
        z = h @ w_in
        return (z[..., :nq].reshape(n, t, C_NG, C_KV, C_G, HEAD_DIM),
                z[..., nq:nq + nkv].reshape(n, t, C_NG, C_KV, 2, HEAD_DIM),
                z[..., nq + nkv:])

    def merge(outs, lses):
        wts = jax.nn.softmax(jnp.stack(lses).astype(jnp.float32), axis=0)[..., None]
        return jnp.sum(wts * jnp.stack(outs).astype(jnp.float32), axis=0).astype(outs[0].dtype)

    qp, kvp, gp = proj(hp)
    n, s = qp.shape[:2]
    outs_p, lses_p, new_p = [], [], []
    for g, (w, d) in enumerate(C_GROUPS):
        kvg = kvp[:, :, g]
        o, lse = banded(to_sub(qp[:, :, g], d), to_sub(kvg[..., 0, :], d), to_sub(kvg[..., 1, :], d),
                        w, d, slopes)
        outs_p.append(from_sub(o, d, n))
        lses_p.append(from_sub(lse, d, n))
        new_p.append(kvg[:, s - min(w, s):])
    qs, kvs, gs = proj(hs)
    outs_s, lses_s, new_s = [], [], []
    for g, (w, d) in enumerate(C_GROUPS):
        o, lse, nb = dilated_step(qs[:, :, g], kvs[:, :, g], bufs[g], w, d, slopes)
        outs_s.append(o)
        lses_s.append(lse)
        new_s.append(nb)
    return (out_proj(merge(outs_p, lses_p), gp, w_out), out_proj(merge(outs_s, lses_s), gs, w_out),
            new_p, new_s)


def setup_inputs(seed: int = 0) -> dict:
    key = jax.random.key(seed)
    ks = jax.random.split(key, 24)
    nrm = jax.random.normal
    n_pages = PAST_LEN // PAGE_SIZE
    used = DEC_BATCH * n_pages
    pool_pages = used + max(1, used // 4)

    def w(k, fi, fo):
        return nrm(k, (fi, fo), jnp.float32) * fi ** -0.5

    page_table = jax.random.permutation(ks[0], pool_pages)[:used].reshape(DEC_BATCH, n_pages).astype(jnp.int32)
    return {
        'x_prompt': nrm(ks[1], (BATCH, SEQ, D_MODEL), jnp.float32),
        'x_sample': nrm(ks[2], (DEC_BATCH, DEC_SEQ, D_MODEL), jnp.float32),
        'state_a0_kv': nrm(ks[3], (DEC_BATCH, min(A_WINDOW, PAST_LEN), A_KV, 2, HEAD_DIM), jnp.float32),
        'cache_b1_cmp_kv': nrm(ks[4], (pool_pages, PAGE_SIZE, B_KV, 2, HEAD_DIM), jnp.float32),
        'cache_b1_sel_kv': nrm(ks[5], (pool_pages, PAGE_SIZE, B_KV, 2, HEAD_DIM), jnp.float32),
        'state_b1_win_kv': nrm(ks[6], (DEC_BATCH, min(B_WINDOW, PAST_LEN), B_KV, 2, HEAD_DIM), jnp.float32),
        'state_c2_g0_kv': nrm(ks[7], (DEC_BATCH, min(C_GROUPS[0][0], PAST_LEN), C_KV, 2, HEAD_DIM), jnp.float32),
        'state_c2_g1_kv': nrm(ks[8], (DEC_BATCH, min(C_GROUPS[1][0], PAST_LEN), C_KV, 2, HEAD_DIM), jnp.float32),
        'state_c2_g2_kv': nrm(ks[9], (DEC_BATCH, min(C_GROUPS[2][0], PAST_LEN), C_KV, 2, HEAD_DIM), jnp.float32),
        'state_a3_kv': nrm(ks[10], (DEC_BATCH, min(A_WINDOW, PAST_LEN), A_KV, 2, HEAD_DIM), jnp.float32),
        'page_table': page_table,
        'norm_g': 1.0 + 0.05 * nrm(ks[11], (DEPTH, 2, D_MODEL), jnp.float32),
        'w_in_0': w(ks[12], D_MODEL, A_IN),
        'w_out_0': w(ks[13], A_HEADS * HEAD_DIM, D_MODEL),
        'sinks_0': 0.5 * nrm(ks[14], (A_HEADS,), jnp.float32),
        'w_in_1': w(ks[15], D_MODEL, B_IN),
        'w_out_1': w(ks[16], B_HEADS * HEAD_DIM, D_MODEL),
        'cmp_pool_1': (1.0 + 0.1 * nrm(ks[17], (2, B_CMP), jnp.float32)) / B_CMP,
        'w_in_2': w(ks[18], D_MODEL, C_IN),
        'w_out_2': w(ks[19], C_HEADS * HEAD_DIM, D_MODEL),
        'w_in_3': w(ks[20], D_MODEL, A_IN),
        'w_out_3': w(ks[21], A_HEADS * HEAD_DIM, D_MODEL),
        'sinks_3': 0.5 * nrm(ks[22], (A_HEADS,), jnp.float32),
    }


def reference(x_prompt, x_sample, state_a0_kv, cache_b1_cmp_kv, cache_b1_sel_kv, state_b1_win_kv,
              state_c2_g0_kv, state_c2_g1_kv, state_c2_g2_kv, state_a3_kv, page_table, norm_g,
              w_in_0, w_out_0, sinks_0, w_in_1, w_out_1, cmp_pool_1, w_in_2, w_out_2,
              w_in_3, w_out_3, sinks_3):
    layer_states = ((state_a0_kv,), (cache_b1_cmp_kv, cache_b1_sel_kv, state_b1_win_kv),
                    (state_c2_g0_kv, state_c2_g1_kv, state_c2_g2_kv), (state_a3_kv,))
    layer_weights = ((w_in_0, w_out_0, sinks_0), (w_in_1, w_out_1, cmp_pool_1),
                     (w_in_2, w_out_2), (w_in_3, w_out_3, sinks_3))
    xp, xs = x_prompt, x_sample
    new_p, new_s = [], []
    for i in range(DEPTH):
        kind = i % N_MIXERS
        hp = rms_norm(xp, norm_g[i, 0])
        hs = rms_norm(xs, norm_g[i, 0])
        if kind == 0:
            yp, ys, sp, ss = mixer_a(hp, hs, *layer_states[i], *layer_weights[i])
        elif kind == 1:
            yp, ys, sp, ss = mixer_b(hp, hs, *layer_states[i], page_table, *layer_weights[i])
        else:
            yp, ys, sp, ss = mixer_c(hp, hs, layer_states[i], *layer_weights[i])
        xp = xp + rms_norm(yp, norm_g[i, 1])
        xs = xs + rms_norm(ys, norm_g[i, 1])
        new_p += sp
        new_s += ss
    a0_p, b1_cmp_p, b1_sel_p, b1_win_p, c2_g0_p, c2_g1_p, c2_g2_p, a3_p = new_p
    a0_s, b1_cmp_s, b1_sel_s, b1_win_s, c2_g0_s, c2_g1_s, c2_g2_s, a3_s = new_s
    return (xp, xs, a0_p, a0_s, b1_cmp_p, b1_cmp_s, b1_sel_p, b1_sel_s, b1_win_p, b1_win_s,
            c2_g0_p, c2_g0_s, c2_g1_p, c2_g1_s, c2_g2_p, c2_g2_s, a3_p, a3_s)
```

```python
import functools

import jax
import jax.numpy as jnp
from jax import lax
from jax.experimental import pallas as pl
from jax.experimental.pallas import tpu as pltpu

F32 = jnp.float32
BF16 = jnp.bfloat16

HEAD_DIM = 64
KV_LANES = 2 * HEAD_DIM
RMS_EPS = 1e-6
NEG = -1e30
SCALE = HEAD_DIM ** -0.5

A_WINDOW = 128
B_CMP = 32
B_SEL = 64
B_TOPN = 16
B_WINDOW = 512
C_GROUPS = ((128, 1), (512, 4), (2048, 16))

VMEM_LIMIT_BYTES = 48 * 1024 * 1024


def _params(*sem):
    return pltpu.CompilerParams(dimension_semantics=sem, vmem_limit_bytes=VMEM_LIMIT_BYTES)


def _smem():
    return pl.BlockSpec(memory_space=pltpu.SMEM)


def _dot_nt(a, b):
    return lax.dot_general(a, b, (((1,), (1,)), ((), ())), preferred_element_type=F32)


def _dot(a, b):
    return jnp.dot(a, b, preferred_element_type=F32)


def _iota(shape, dim):
    return lax.broadcasted_iota(jnp.int32, shape, dim)


def _log2(d):
    assert d > 0 and d & (d - 1) == 0
    return d.bit_length() - 1


def _fdiv(x, d):
    return lax.shift_right_arithmetic(x, _log2(d))


def _fmod(x, d):
    return x & (d - 1)


def _sum(xs):
    return functools.reduce(lambda a, b: a + b, xs)


def _alibi_slopes(n_heads):
    h = jnp.arange(1, n_heads + 1, dtype=F32)
    return jnp.exp2(-8.0 * h / n_heads)


def _softmax_parts(s, valid, sink=None):
    s = jnp.where(valid, s, NEG)
    m = jnp.max(s, axis=-1, keepdims=True)
    if sink is not None:
        m = jnp.maximum(m, sink)
    e = jnp.where(valid, jnp.exp(s - m), 0.0)
    den = jnp.sum(e, axis=-1, keepdims=True)
    if sink is not None:
        den = den + jnp.exp(sink - m)
    den = jnp.where(den > 0, den, 1.0)
    return e / den, m, den


def _proj_body(x_ref, g_ref, w_ref, *out_refs, splits, chunk):
    x = x_ref[...]
    ms = jnp.mean(x * x, axis=-1, keepdims=True)
    h = ((x * lax.rsqrt(ms + RMS_EPS)) * g_ref[...]).astype(BF16)
    for o_ref, (start, width) in zip(out_refs, splits):
        for c0 in range(0, width, chunk):
            cw = min(chunk, width - c0)
            z = _dot(h, w_ref[:, start + c0:start + c0 + cw])
            o_ref[:, c0:c0 + cw] = z.astype(o_ref.dtype)


def _proj(x2d, g, w_bf16, splits, dtypes, tm):
    m_rows, d = x2d.shape
    n = w_bf16.shape[1]
    tm = min(tm, m_rows)
    assert m_rows % tm == 0
    return pl.pallas_call(
        functools.partial(_proj_body, splits=tuple(splits), chunk=512),
        grid=(m_rows // tm,),
        in_specs=[pl.BlockSpec((tm, d), lambda i: (i, 0)),
                  pl.BlockSpec((1, d), lambda i: (0, 0)),
                  pl.BlockSpec((d, n), lambda i: (0, 0))],
        out_specs=[pl.BlockSpec((tm, wd), lambda i: (i, 0)) for _, wd in splits],
        out_shape=[jax.ShapeDtypeStruct((m_rows, wd), dt) for (_, wd), dt in zip(splits, dtypes)],
        compiler_params=_params("parallel"),
        name="proj",
    )(x2d, g.reshape(1, d), w_bf16)


def _outproj_body(*refs, n_branch):
    if n_branch == 1:
        o_ref, gate_ref, x_ref, g_ref, w_ref, y_ref = refs
        o = o_ref[...]
    else:
        o_refs = refs[:n_branch]
        l_refs = refs[n_branch:2 * n_branch]
        gate_ref, x_ref, g_ref, w_ref, y_ref = refs[2 * n_branch:]
        ls = [r[...] for r in l_refs]
        m = functools.reduce(jnp.maximum, ls)
        es = [jnp.exp(l - m) for l in ls]
        den = _sum(es)
        o = _sum([(e / den) * r[...] for e, r in zip(es, o_refs)])
    gate = gate_ref[...]
    a = (o * (gate * jax.nn.sigmoid(gate))).astype(BF16)
    y = _dot(a, w_ref[...])
    ms = jnp.mean(y * y, axis=-1, keepdims=True)
    y_ref[...] = x_ref[...] + (y * lax.rsqrt(ms + RMS_EPS)) * g_ref[...]


def _outproj(os_, lses, gate, x2d, g, w_bf16, tm):
    m_rows, d = x2d.shape
    k = w_bf16.shape[0]
    tm = min(tm, m_rows)
    assert m_rows % tm == 0
    row = lambda width: pl.BlockSpec((tm, width), lambda i: (i, 0))
    n_branch = len(os_)
    ins = list(os_) + list(lses) + [gate, x2d, g.reshape(1, d), w_bf16]
    in_specs = ([row(k)] * (n_branch + len(lses)) + [row(k), row(d),
                pl.BlockSpec((1, d), lambda i: (0, 0)), pl.BlockSpec((k, d), lambda i: (0, 0))])
    return pl.pallas_call(
        functools.partial(_outproj_body, n_branch=n_branch),
        grid=(m_rows // tm,),
        in_specs=in_specs,
        out_specs=row(d),
        out_shape=jax.ShapeDtypeStruct((m_rows, d), F32),
        compiler_params=_params("parallel"),
        name="outproj",
    )(*ins)


def _banded_body(*refs, tq, wsub, stride, window, g_heads, has_sink, with_lse, nsa_merge):
    it = iter(refs)
    slopes_ref = next(it)
    sinks_ref = next(it) if has_sink else None
    q_ref = next(it)
    kv_ref = next(it)
    if nsa_merge:
        oc_ref, os_ref, bgz_ref = next(it), next(it), next(it)
    o_ref = next(it)
    lse_ref = next(it) if with_lse else None

    k = pl.program_id(2)
    i = pl.program_id(3)
    j0 = i * tq
    span = wsub + tq
    kstart = pl.multiple_of(jnp.maximum(j0 - wsub, 0), 128)
    kv = kv_ref[0, pl.ds(kstart, span), :]
    kb = kv[:, :HEAD_DIM]
    vb = kv[:, HEAD_DIM:]
    q_pos = j0 + _iota((tq, 1), 0)
    k_pos = kstart + _iota((1, span), 1)
    dist = (q_pos - k_pos) * stride
    valid = (dist >= 0) & (dist <= window)
    distf = dist.astype(F32)
    if nsa_merge:
        bg = jax.nn.sigmoid(bgz_ref[0, 0])
    for g in range(g_heads):
        head = k * g_heads + g
        lo, hi = g * HEAD_DIM, (g + 1) * HEAD_DIM
        s = _dot_nt(q_ref[0, :, lo:hi], kb) * SCALE - slopes_ref[head] * distf
        sink = sinks_ref[head] if has_sink else None
        p, m, den = _softmax_parts(s, valid, sink)
        o = _dot(p.astype(BF16), vb)
        if nsa_merge:
            col = lambda j: bg[:, j * g_heads + g:j * g_heads + g + 1]
            o = col(0) * oc_ref[0, :, lo:hi] + col(1) * os_ref[0, :, lo:hi] + col(2) * o
        o_ref[0, :, lo:hi] = o
        if with_lse:
            lse_ref[0, :, lo:hi] = jnp.broadcast_to(m + jnp.log(den), (tq, HEAD_DIM))


def _banded(q, kv, slopes, *, n_heads, n_kv, window, stride, q_group=0, sinks=None,
            with_lse=False, merge=None, tq=256):
    n, s, qc = q.shape
    kc = kv.shape[2]
    g_heads = n_heads // n_kv
    gw = g_heads * HEAD_DIM
    sp = s // stride
    wsub = window // stride
    tq = min(tq, sp - wsub)
    assert sp % tq == 0 and wsub % 128 == 0 and tq % 128 == 0 and tq > 0
    qv = q.reshape(n, sp, stride * qc)
    kvv = kv.reshape(n, sp, stride * kc)
    q_blocks, kv_blocks, o_blocks = qc // gw, kc // KV_LANES, n_kv
    q_map = lambda b, r, k, i: (b, i, r * q_blocks + q_group * n_kv + k)
    kv_map = lambda b, r, k, i: (b, 0, r * kv_blocks + q_group * n_kv + k)
    o_map = lambda b, r, k, i: (b, i, r * o_blocks + k)
    ins = [slopes]
    in_specs = [_smem()]
    if sinks is not None:
        ins.append(sinks)
        in_specs.append(_smem())
    ins += [qv, kvv]
    in_specs += [pl.BlockSpec((1, tq, gw), q_map), pl.BlockSpec((1, sp, KV_LANES), kv_map)]
    if merge is not None:
        o_c, o_s, bgz = merge
        bgz = bgz.reshape(n, s, 3, n_kv, g_heads).transpose(0, 3, 1, 2, 4).reshape(n, n_kv, s, 3 * g_heads)
        ins += [o_c, o_s, bgz]
        in_specs += [pl.BlockSpec((1, tq, gw), o_map), pl.BlockSpec((1, tq, gw), o_map),
                     pl.BlockSpec((1, 1, tq, 3 * g_heads), lambda b, r, k, i: (b, k, i, 0))]
    o_sds = jax.ShapeDtypeStruct((n, sp, stride * n_heads * HEAD_DIM), F32)
    o_spec = pl.BlockSpec((1, tq, gw), o_map)
    out = pl.pallas_call(
        functools.partial(_banded_body, tq=tq, wsub=wsub, stride=stride, window=window, g_heads=g_heads,
                          has_sink=sinks is not None, with_lse=with_lse, nsa_merge=merge is not None),
        grid=(n, stride, n_kv, sp // tq),
        in_specs=in_specs,
        out_specs=[o_spec, o_spec] if with_lse else o_spec,
        out_shape=[o_sds, o_sds] if with_lse else o_sds,
        compiler_params=_params("parallel", "parallel", "parallel", "parallel"),
        name="banded_attn",
    )(*ins)
    if with_lse:
        return tuple(a.reshape(n, s, n_heads * HEAD_DIM) for a in out)
    return out.reshape(n, s, n_heads * HEAD_DIM)


POOL_ROWS = 1024


def _pool_body(*refs, n_in, rows_in, paged):
    if paged:
        refs = refs[1:]
    p_ref = refs[0]
    in_refs = refs[1:1 + n_in]
    ce_ref, co_ref = refs[1 + n_in:]
    is_k = _fmod(_iota((1, ce_ref.shape[2]), 1), KV_LANES) < HEAD_DIM
    for parity, out_ref in ((0, ce_ref), (1, co_ref)):
        acc_k = None
        acc_v = None
        for j, r in enumerate(in_refs):
            rows = r[0].astype(BF16)
            pk = p_ref[parity, :, j * rows_in:(j + 1) * rows_in]
            pv = p_ref[2 + parity, :, j * rows_in:(j + 1) * rows_in]
            dk, dv = _dot(pk, rows), _dot(pv, rows)
            acc_k = dk if acc_k is None else acc_k + dk
            acc_v = dv if acc_v is None else acc_v + dv
        out_ref[0] = jnp.where(is_k, acc_k, acc_v)


def _pool_mats(pool_w):
    nb = POOL_ROWS // B_CMP
    eye = jnp.eye(nb, dtype=F32)
    mats = []
    for x in range(2):
        full = jnp.kron(eye, pool_w[x].astype(F32)[None, :])
        mats += [full[0::2], full[1::2]]
    return jnp.stack(mats).astype(BF16)


def _pool_prompt(rows, pmats):
    n, l, c = rows.shape
    assert l % POOL_ROWS == 0
    half = POOL_ROWS // B_CMP // 2
    sds = jax.ShapeDtypeStruct((n, l // B_CMP // 2, c), F32)
    out_spec = pl.BlockSpec((1, half, c), lambda b, j: (b, j, 0))
    return pl.pallas_call(
        functools.partial(_pool_body, n_in=1, rows_in=POOL_ROWS, paged=False),
        grid=(n, l // POOL_ROWS),
        in_specs=[pl.BlockSpec(pmats.shape, lambda b, j: (0, 0, 0)),
                  pl.BlockSpec((1, POOL_ROWS, c), lambda b, j: (b, j, 0))],
        out_specs=[out_spec, out_spec],
        out_shape=[sds, sds],
        compiler_params=_params("parallel", "parallel"),
        name="nsa_pool",
    )(pmats, rows)


def _pool_paged(cache, page_table, pmats):
    _, page, c = cache.shape
    n, n_pg = page_table.shape
    per_step = POOL_ROWS // page
    assert n_pg % per_step == 0
    half = POOL_ROWS // B_CMP // 2
    sds = jax.ShapeDtypeStruct((n, n_pg * page // B_CMP // 2, c), F32)
    out_spec = pl.BlockSpec((1, half, c), lambda b, j, pt: (b, j, 0))
    page_spec = lambda u: pl.BlockSpec((1, page, c), lambda b, j, pt: (pt[b, j * per_step + u], 0, 0))
    return pl.pallas_call(
        functools.partial(_pool_body, n_in=per_step, rows_in=page, paged=True),
        grid_spec=pltpu.PrefetchScalarGridSpec(
            num_scalar_prefetch=1,
            grid=(n, n_pg // per_step),
            in_specs=[pl.BlockSpec(pmats.shape, lambda b, j, pt: (0, 0, 0))]
                     + [page_spec(u) for u in range(per_step)],
            out_specs=[out_spec, out_spec]),
        out_shape=[sds, sds],
        compiler_params=_params("parallel", "parallel"),
        name="nsa_pool_paged",
    )(page_table, pmats, *([cache] * per_step))


def _topk_mask(score, n_sel):
    rows, width = score.shape
    lane = _iota((rows, width), 1).astype(F32)
    sel = jnp.zeros((rows, width), F32)
    sc = score
    for _ in range(n_sel):
        mx = jnp.max(sc, axis=-1, keepdims=True)
        idx = jnp.min(jnp.where(sc == mx, lane, float(width)), axis=-1, keepdims=True)
        hit = lane == idx
        sel = jnp.where(hit, 1.0, sel)
        sc = jnp.where(hit, -4.0, sc)
    return sel


def _sel_score(imp, blk, q_pos, g_heads):
    cur = _fdiv(q_pos, B_SEL)
    forced = (blk == 0) | (blk == cur) | (blk == cur - 1)
    return jnp.where(blk * B_SEL <= q_pos, imp + jnp.where(forced, g_heads + 1.0, 0.0), -1.0)


def _nsa_cmp_body(slopes_ref, q_ref, ce_ref, co_ref, oc_ref, sel_ref, *, tq, g_heads, n_sel):
    k = pl.program_id(1)
    i = pl.program_id(2)
    nbs = ce_ref.shape[1]
    q_pos = i * tq + _iota((tq, 1), 0)
    blk = _iota((1, nbs), 1)
    dist_e = q_pos - ((2 * blk + 1) * B_CMP - 1)
    dist_o = q_pos - ((2 * blk + 2) * B_CMP - 1)
    ok_e, ok_o = dist_e >= 0, dist_o >= 0
    de, do = dist_e.astype(F32), dist_o.astype(F32)
    ce = ce_ref[0].astype(BF16)
    co = co_ref[0].astype(BF16)
    ke, ve, ko, vo = ce[:, :HEAD_DIM], ce[:, HEAD_DIM:], co[:, :HEAD_DIM], co[:, HEAD_DIM:]
    imp_e = jnp.zeros((tq, nbs), F32)
    imp_o = jnp.zeros((tq, nbs), F32)
    for g in range(g_heads):
        slope = slopes_ref[k * g_heads + g]
        lo, hi = g * HEAD_DIM, (g + 1) * HEAD_DIM
        qg = q_ref[0, :, lo:hi]
        se = jnp.where(ok_e, _dot_nt(qg, ke) * SCALE - slope * de, NEG)
        so = jnp.where(ok_o, _dot_nt(qg, ko) * SCALE - slope * do, NEG)
        m = jnp.maximum(jnp.max(se, axis=-1, keepdims=True), jnp.max(so, axis=-1, keepdims=True))
        ee = jnp.where(ok_e, jnp.exp(se - m), 0.0)
        eo = jnp.where(ok_o, jnp.exp(so - m), 0.0)
        den = jnp.sum(ee, axis=-1, keepdims=True) + jnp.sum(eo, axis=-1, keepdims=True)
        den = jnp.where(den > 0, den, 1.0)
        pe, po = ee / den, eo / den
        oc_ref[0, :, lo:hi] = _dot(pe.astype(BF16), ve) + _dot(po.astype(BF16), vo)
        imp_e = imp_e + pe
        imp_o = imp_o + po
    score = _sel_score(imp_e + imp_o, blk, q_pos, g_heads)
    sel_ref[0, 0] = _topk_mask(score, n_sel).astype(BF16)


def _nsa_cmp(q, cmp_e, cmp_o, slopes, *, n_heads, n_kv, tq=128):
    n, s, _ = q.shape
    g_heads = n_heads // n_kv
    gw = g_heads * HEAD_DIM
    nbs = cmp_e.shape[1]
    n_sel = min(B_TOPN, nbs)
    tq = min(tq, s)
    return pl.pallas_call(
        functools.partial(_nsa_cmp_body, tq=tq, g_heads=g_heads, n_sel=n_sel),
        grid=(n, n_kv, s // tq),
        in_specs=[_smem(),
                  pl.BlockSpec((1, tq, gw), lambda b, k, i: (b, i, k)),
                  pl.BlockSpec((1, nbs, KV_LANES), lambda b, k, i: (b, 0, k)),
                  pl.BlockSpec((1, nbs, KV_LANES), lambda b, k, i: (b, 0, k))],
        out_specs=[pl.BlockSpec((1, tq, gw), lambda b, k, i: (b, i, k)),
                   pl.BlockSpec((1, 1, tq, nbs), lambda b, k, i: (b, k, i, 0))],
        out_shape=[jax.ShapeDtypeStruct((n, s, n_heads * HEAD_DIM), F32),
                   jax.ShapeDtypeStruct((n, n_kv, s, nbs), BF16)],
        compiler_params=_params("parallel", "parallel", "parallel"),
        name="nsa_cmp_select",
    )(slopes, q, cmp_e, cmp_o)


def _flash_update(s, valid, vb, m_ref, l_ref, acc_ref, idx):
    s = jnp.where(valid, s, NEG)
    m_prev = m_ref[idx]
    m_new = jnp.maximum(m_prev, jnp.max(s, axis=-1, keepdims=True))
    alpha = jnp.exp(m_prev - m_new)
    e = jnp.where(valid, jnp.exp(s - m_new), 0.0)
    l_ref[idx] = alpha * l_ref[idx] + jnp.sum(e, axis=-1, keepdims=True)
    acc_ref[idx] = alpha * acc_ref[idx] + _dot(e.astype(BF16), vb)
    m_ref[idx] = m_new


def _flash_init(m_ref, l_ref, acc_ref):
    m_ref[...] = jnp.full(m_ref.shape, NEG, F32)
    l_ref[...] = jnp.zeros(l_ref.shape, F32)
    acc_ref[...] = jnp.zeros(acc_ref.shape, F32)


def _flash_result(l_ref, acc_ref, idx):
    l = l_ref[idx]
    return acc_ref[idx] / jnp.where(l > 0, l, 1.0)


def _nsa_sel_body(slopes_ref, q_ref, kv_ref, sel_ref, o_ref, m_ref, l_ref, acc_ref, *, tq, tk, g_heads):
    k = pl.program_id(1)
    i = pl.program_id(2)
    nbs = sel_ref.shape[3]
    j0 = i * tq
    q_pos = j0 + _iota((tq, 1), 0)
    sel = sel_ref[0, 0]
    _flash_init(m_ref, l_ref, acc_ref)

    def chunk(c, carry):
        k0 = pl.multiple_of(c * tk, tk)
        kv = kv_ref[0, pl.ds(k0, tk), :]
        kb, vb = kv[:, :HEAD_DIM], kv[:, HEAD_DIM:]
        k_pos = k0 + _iota((1, tk), 1)
        expand = (_iota((nbs, 1), 0) == _fdiv(k_pos, B_SEL)).astype(BF16)
        picked = _dot(sel, expand) > 0.5
        valid = picked & (k_pos <= q_pos)
        distf = (q_pos - k_pos).astype(F32)
        for g in range(g_heads):
            lo, hi = g * HEAD_DIM, (g + 1) * HEAD_DIM
            s = _dot_nt(q_ref[0, :, lo:hi], kb) * SCALE - slopes_ref[k * g_heads + g] * distf
            _flash_update(s, valid, vb, m_ref, l_ref, acc_ref, g)
        return carry

    lax.fori_loop(0, _fdiv(j0 + tq + tk - 1, tk), chunk, 0)
    for g in range(g_heads):
        o_ref[0, :, g * HEAD_DIM:(g + 1) * HEAD_DIM] = _flash_result(l_ref, acc_ref, g)


def _nsa_sel(q, kv, sel, slopes, *, n_heads, n_kv, tq=256, tk=256):
    n, s, _ = q.shape
    g_heads = n_heads // n_kv
    gw = g_heads * HEAD_DIM
    nbs = sel.shape[3]
    tq, tk = min(tq, s), min(tk, s)
    assert s % tq == 0 and s % tk == 0
    return pl.pallas_call(
        functools.partial(_nsa_sel_body, tq=tq, tk=tk, g_heads=g_heads),
        grid=(n, n_kv, s // tq),
        in_specs=[_smem(),
                  pl.BlockSpec((1, tq, gw), lambda b, k, i: (b, i, k)),
                  pl.BlockSpec((1, s, KV_LANES), lambda b, k, i: (b, 0, k)),
                  pl.BlockSpec((1, 1, tq, nbs), lambda b, k, i: (b, k, i, 0))],
        out_specs=pl.BlockSpec((1, tq, gw), lambda b, k, i: (b, i, k)),
        out_shape=jax.ShapeDtypeStruct((n, s, n_heads * HEAD_DIM), F32),
        scratch_shapes=[pltpu.VMEM((g_heads, tq, 1), F32), pltpu.VMEM((g_heads, tq, 1), F32),
                        pltpu.VMEM((g_heads, tq, HEAD_DIM), F32)],
        compiler_params=_params("parallel", "parallel", "parallel"),
        name="nsa_selected",
    )(slopes, q, kv, sel)


T_PAD = 8
NEW_PAD = 16
KEY_PAD = 128


def _row_select(ref, base, g_heads, rows):
    gi = _fdiv(_iota((rows, 1), 0), T_PAD)
    out = jnp.zeros((rows, 1), F32)
    for g in range(g_heads):
        out = jnp.where(gi == g, ref[base + g], out)
    return out


def _pad_rows(x, lo, total):
    return jnp.pad(x, ((0, 0), (lo, total - lo - x.shape[1]), (0, 0)))


def _dec_window_body(*refs, lb, t_new, window, dilation, past_len, g_heads, n_kv, has_sink, with_lse):
    it = iter(refs)
    slopes_ref = next(it)
    sinks_ref = next(it) if has_sink else None
    q_ref, new_lo_ref, new_hi_ref, buf_ref = next(it), next(it), next(it), next(it)
    o_ref = next(it)
    lse_ref = next(it) if with_lse else None
    nbuf_ref = next(it)
    keys_ref = next(it)

    rows = g_heads * T_PAD
    span = lb + KEY_PAD
    buf = buf_ref[0]
    keys_ref[0:lb, :] = buf.astype(BF16)
    keys_ref[lb:lb + NEW_PAD, :] = new_lo_ref[0].astype(BF16)
    keys_ref[lb + NEW_PAD:span, :] = jnp.zeros((KEY_PAD - NEW_PAD, keys_ref.shape[1]), BF16)
    shifted = pltpu.roll(buf, lb - t_new, axis=0)
    nbuf_ref[0] = shifted
    is_new = _iota((T_PAD, 1), 0) >= T_PAD - t_new
    nbuf_ref[0, lb - T_PAD:lb, :] = jnp.where(is_new, new_hi_ref[0], shifted[lb - T_PAD:lb])

    t = _fmod(_iota((rows, 1), 0), T_PAD)
    r = _iota((1, span), 1)
    dist = lb + t - r
    valid = (dist >= 0) & (dist <= window) & (past_len - lb + r >= 0)
    if dilation > 1:
        valid = valid & (_fmod(dist, dilation) == 0)
    distf = dist.astype(F32)
    for k in range(n_kv):
        kb = keys_ref[:, k * KV_LANES:k * KV_LANES + HEAD_DIM]
        vb = keys_ref[:, k * KV_LANES + HEAD_DIM:(k + 1) * KV_LANES]
        slope = _row_select(slopes_ref, k * g_heads, g_heads, rows)
        sink = _row_select(sinks_ref, k * g_heads, g_heads, rows) if has_sink else None
        s = _dot_nt(q_ref[0, k], kb) * SCALE - slope * distf
        p, m, den = _softmax_parts(s, valid, sink)
        o_ref[0, k] = _dot(p.astype(BF16), vb)
        if with_lse:
            lse_ref[0, k] = jnp.broadcast_to(m + jnp.log(den), (rows, HEAD_DIM))


def _dec_window(qd, kv_new, buf, slopes, *, window, dilation, past_len, sinks=None, with_lse=False):
    n, n_kv, rows, _ = qd.shape
    g_heads = rows // T_PAD
    _, t_new, c = kv_new.shape
    lb = buf.shape[1]
    assert lb % NEW_PAD == 0 and t_new <= T_PAD
    new_lo = _pad_rows(kv_new, 0, NEW_PAD)
    new_hi = _pad_rows(kv_new, T_PAD - t_new, T_PAD)
    ins = [slopes]
    in_specs = [_smem()]
    if sinks is not None:
        ins.append(sinks)
        in_specs.append(_smem())
    ins += [qd, new_lo, new_hi, buf]
    o_spec = pl.BlockSpec((1, n_kv, rows, HEAD_DIM), lambda b: (b, 0, 0, 0))
    buf_spec = pl.BlockSpec((1, lb, c), lambda b: (b, 0, 0))
    in_specs += [o_spec, pl.BlockSpec((1, NEW_PAD, c), lambda b: (b, 0, 0)),
                 pl.BlockSpec((1, T_PAD, c), lambda b: (b, 0, 0)), buf_spec]
    o_sds = jax.ShapeDtypeStruct((n, n_kv, rows, HEAD_DIM), F32)
    out_specs = [o_spec] + ([o_spec] if with_lse else []) + [buf_spec]
    out_shape = [o_sds] + ([o_sds] if with_lse else []) + [jax.ShapeDtypeStruct(buf.shape, F32)]
    return pl.pallas_call(
        functools.partial(_dec_window_body, lb=lb, t_new=t_new, window=window, dilation=dilation,
                          past_len=past_len, g_heads=g_heads, n_kv=n_kv, has_sink=sinks is not None,
                          with_lse=with_lse),
        grid=(n,),
        in_specs=in_specs,
        out_specs=out_specs,
        out_shape=out_shape,
        scratch_shapes=[pltpu.VMEM((lb + KEY_PAD, c), BF16)],
        compiler_params=_params("parallel"),
        name="decode_window",
    )(*ins)


def _nsa_dec_cmp_body(slopes_ref, q_ref, ce_ref, co_ref, cnew_ref, pnew_ref, oc_ref, sel_ref, *,
                      past_len, g_heads, n_kv, n_sel):
    rows = g_heads * T_PAD
    nbs = ce_ref.shape[1]
    t = _fmod(_iota((rows, 1), 0), T_PAD)
    q_pos = past_len + t
    blk = _iota((1, nbs), 1)
    dist_e = q_pos - ((2 * blk + 1) * B_CMP - 1)
    dist_o = q_pos - ((2 * blk + 2) * B_CMP - 1)
    lane_n = _iota((1, NEW_PAD), 1)
    dist_ne = q_pos - ((2 * nbs + 1) * B_CMP - 1)
    dist_no = q_pos - ((2 * nbs + 2) * B_CMP - 1)
    cnew = cnew_ref[0].astype(BF16)
    is_k = _fmod(_iota((1, cnew.shape[1]), 1), KV_LANES) < HEAD_DIM
    new_e = jnp.where(is_k, _dot(pnew_ref[0], cnew), _dot(pnew_ref[1], cnew)).astype(BF16)
    new_o = jnp.zeros_like(new_e)
    ce = ce_ref[0].astype(BF16)
    co = co_ref[0].astype(BF16)
    key_sets = ((ce, dist_e, dist_e >= 0), (co, dist_o, dist_o >= 0),
                (new_e, dist_ne, (dist_ne >= 0) & (lane_n == 0)),
                (new_o, dist_no, (dist_no >= 0) & (lane_n == 0)))
    q_pos8 = past_len + _iota((T_PAD, 1), 0)
    lane_p = _iota((1, KEY_PAD), 1)
    for k in range(n_kv):
        c0 = k * KV_LANES
        slope = _row_select(slopes_ref, k * g_heads, g_heads, rows)
        q = q_ref[0, k]
        ss = [jnp.where(ok, _dot_nt(q, keys[:, c0:c0 + HEAD_DIM]) * SCALE - slope * dist.astype(F32), NEG)
              for keys, dist, ok in key_sets]
        m = functools.reduce(jnp.maximum, [jnp.max(s, axis=-1, keepdims=True) for s in ss])
        es = [jnp.where(ok, jnp.exp(s - m), 0.0) for s, (_, _, ok) in zip(ss, key_sets)]
        den = _sum([jnp.sum(e, axis=-1, keepdims=True) for e in es])
        den = jnp.where(den > 0, den, 1.0)
        ps = [e / den for e in es]
        oc_ref[0, k] = _sum([_dot(p.astype(BF16), keys[:, c0 + HEAD_DIM:c0 + KV_LANES])
                             for p, (keys, _, _) in zip(ps, key_sets)])
        gsum = lambda p: _sum([p[g * T_PAD:(g + 1) * T_PAD] for g in range(g_heads)])
        imp = gsum(ps[0]) + gsum(ps[1])
        imp_new = gsum(ps[2]) + gsum(ps[3])
        score = _sel_score(imp, blk, q_pos8, g_heads)
        score_new = _sel_score(imp_new[:, 0:1], nbs + lane_p, q_pos8, g_heads)
        score_new = jnp.where(lane_p == 0, score_new, -3.0)
        sel_ref[0, k] = _topk_mask(jnp.concatenate([score, score_new], axis=1), n_sel)


def _nsa_dec_cmp(qd, cmp_e, cmp_o, c_new, pnew, slopes, *, past_len):
    n, n_kv, rows, _ = qd.shape
    g_heads = rows // T_PAD
    nbs = cmp_e.shape[1]
    c = cmp_e.shape[2]
    n_sel = min(B_TOPN, nbs + 1)
    cnew = _pad_rows(c_new, 0, NEW_PAD)
    q_spec = pl.BlockSpec((1, n_kv, rows, HEAD_DIM), lambda b: (b, 0, 0, 0))
    cmp_spec = pl.BlockSpec((1, nbs, c), lambda b: (b, 0, 0))
    return pl.pallas_call(
        functools.partial(_nsa_dec_cmp_body, past_len=past_len, g_heads=g_heads, n_kv=n_kv, n_sel=n_sel),
        grid=(n,),
        in_specs=[_smem(), q_spec, cmp_spec, cmp_spec,
                  pl.BlockSpec((1, NEW_PAD, c), lambda b: (b, 0, 0)),
                  pl.BlockSpec(pnew.shape, lambda b: (0, 0, 0))],
        out_specs=[q_spec, pl.BlockSpec((1, n_kv, T_PAD, nbs + KEY_PAD), lambda b: (b, 0, 0, 0))],
        out_shape=[jax.ShapeDtypeStruct((n, n_kv, rows, HEAD_DIM), F32),
                   jax.ShapeDtypeStruct((n, n_kv, T_PAD, nbs + KEY_PAD), F32)],
        compiler_params=_params("parallel"),
        name="nsa_decode_cmp_select",
    )(slopes, qd, cmp_e, cmp_o, cnew, pnew)


def _nsa_dec_sel_body(*refs, past_len, g_heads, n_kv, per_step, page):
    slopes_ref, q_ref, sel_ref, snew_ref = refs[1:5]
    page_refs = refs[5:5 + per_step]
    o_ref, m_ref, l_ref, acc_ref = refs[5 + per_step:]
    rows = g_heads * T_PAD
    j = pl.program_id(1)
    nsel = sel_ref.shape[3]
    tk = per_step * page
    t = _fmod(_iota((rows, 1), 0), T_PAD)
    q_pos = past_len + t

    @pl.when(j == 0)
    def _():
        _flash_init(m_ref, l_ref, acc_ref)

    k_pos = j * tk + _iota((1, tk), 1)
    expand = (_iota((nsel, 1), 0) == _fdiv(k_pos, B_SEL)).astype(BF16)
    distf = (q_pos - k_pos).astype(F32)
    pages = [r[0].astype(BF16) for r in page_refs]
    for k in range(n_kv):
        c0 = k * KV_LANES
        kb = jnp.concatenate([p[:, c0:c0 + HEAD_DIM] for p in pages], axis=0)
        vb = jnp.concatenate([p[:, c0 + HEAD_DIM:c0 + KV_LANES] for p in pages], axis=0)
        sel = jnp.concatenate([sel_ref[0, k]] * g_heads, axis=0).astype(BF16)
        valid = (_dot(sel, expand) > 0.5) & (k_pos <= q_pos)
        slope = _row_select(slopes_ref, k * g_heads, g_heads, rows)
        s = _dot_nt(q_ref[0, k], kb) * SCALE - slope * distf
        _flash_update(s, valid, vb, m_ref, l_ref, acc_ref, k)

    @pl.when(j == pl.num_programs(1) - 1)
    def _():
        n_past = past_len // B_SEL
        snew = snew_ref[0].astype(BF16)
        dist = t - _iota((1, NEW_PAD), 1)
        for k in range(n_kv):
            c0 = k * KV_LANES
            sel = jnp.concatenate([sel_ref[0, k]] * g_heads, axis=0)
            valid = (sel[:, n_past:n_past + 1] > 0.5) & (dist >= 0)
            slope = _row_select(slopes_ref, k * g_heads, g_heads, rows)
            s = _dot_nt(q_ref[0, k], snew[:, c0:c0 + HEAD_DIM]) * SCALE - slope * dist.astype(F32)
            _flash_update(s, valid, snew[:, c0 + HEAD_DIM:c0 + KV_LANES], m_ref, l_ref, acc_ref, k)
            o_ref[0, k] = _flash_result(l_ref, acc_ref, k)


def _nsa_dec_sel(qd, sel, s_new, cache, page_table, slopes, *, past_len, per_step=4):
    n, n_kv, rows, _ = qd.shape
    g_heads = rows // T_PAD
    _, page, c = cache.shape
    n_pg = page_table.shape[1]
    assert n_pg % per_step == 0 and past_len == n_pg * page
    snew = _pad_rows(s_new, 0, NEW_PAD)
    q_spec = pl.BlockSpec((1, n_kv, rows, HEAD_DIM), lambda b, j, pt: (b, 0, 0, 0))
    page_spec = lambda u: pl.BlockSpec((1, page, c), lambda b, j, pt: (pt[b, j * per_step + u], 0, 0))
    return pl.pallas_call(
        functools.partial(_nsa_dec_sel_body, past_len=past_len, g_heads=g_heads, n_kv=n_kv,
                          per_step=per_step, page=page),
        grid_spec=pltpu.PrefetchScalarGridSpec(
            num_scalar_prefetch=1,
            grid=(n, n_pg // per_step),
            in_specs=[_smem(), q_spec,
                      pl.BlockSpec((1, n_kv, T_PAD, sel.shape[3]), lambda b, j, pt: (b, 0, 0, 0)),
                      pl.BlockSpec((1, NEW_PAD, c), lambda b, j, pt: (b, 0, 0))]
                     + [page_spec(u) for u in range(per_step)],
            out_specs=q_spec,
            scratch_shapes=[pltpu.VMEM((n_kv, rows, 1), F32), pltpu.VMEM((n_kv, rows, 1), F32),
                            pltpu.VMEM((n_kv, rows, HEAD_DIM), F32)]),
        out_shape=jax.ShapeDtypeStruct((n, n_kv, rows, HEAD_DIM), F32),
        compiler_params=_params("parallel", "arbitrary"),
        name="nsa_decode_selected",
    )(page_table, slopes, qd, sel, snew, *([cache] * per_step))


def _nsa_merge_body(oc_ref, os_ref, ow_ref, bgz_ref, o_ref, *, n_heads):
    bg = jax.nn.sigmoid(bgz_ref[...])
    for h in range(n_heads):
        lo, hi = h * HEAD_DIM, (h + 1) * HEAD_DIM
        o_ref[:, lo:hi] = (bg[:, h:h + 1] * oc_ref[:, lo:hi]
                           + bg[:, n_heads + h:n_heads + h + 1] * os_ref[:, lo:hi]
                           + bg[:, 2 * n_heads + h:2 * n_heads + h + 1] * ow_ref[:, lo:hi])


def _nsa_merge(o_c, o_s, o_w, bgz, n_heads):
    m_rows, d = o_c.shape
    full = lambda a: pl.BlockSpec(a.shape, lambda i: (0, 0))
    return pl.pallas_call(
        functools.partial(_nsa_merge_body, n_heads=n_heads),
        grid=(1,),
        in_specs=[full(o_c), full(o_s), full(o_w), full(bgz)],
        out_specs=full(o_c),
        out_shape=jax.ShapeDtypeStruct((m_rows, d), F32),
        compiler_params=_params("arbitrary"),
        name="nsa_decode_merge",
    )(o_c, o_s, o_w, bgz)


def _to_decode_rows(q, n_kv):
    n, t, hd = q.shape
    g_heads = hd // HEAD_DIM // n_kv
    q = q.reshape(n, t, n_kv, g_heads, HEAD_DIM).transpose(0, 2, 3, 1, 4)
    q = jnp.pad(q, ((0, 0), (0, 0), (0, 0), (0, T_PAD - t), (0, 0)))
    return q.reshape(n, n_kv, g_heads * T_PAD, HEAD_DIM)


def _from_decode_rows(o, t):
    n, n_kv, rows, _ = o.shape
    g_heads = rows // T_PAD
    o = o.reshape(n, n_kv, g_heads, T_PAD, HEAD_DIM)[:, :, :, :t]
    return o.transpose(0, 3, 1, 2, 4).reshape(n * t, n_kv * g_heads * HEAD_DIM)


def _kv_state(rows, n_kv):
    return rows.reshape(*rows.shape[:-1], n_kv, 2, HEAD_DIM)


def _mixer_a(xp, xs, buf, g_pre, g_post, w_in, w_out, sinks, past_len, tm):
    n, s, d = xp.shape
    m, t, _ = xs.shape
    n_heads = d // HEAD_DIM
    n_kv = n_heads // 8
    nq, nkv = n_heads * HEAD_DIM, n_kv * KV_LANES
    slopes = _alibi_slopes(n_heads)
    w_in_b, w_out_b = w_in.astype(BF16), w_out.astype(BF16)
    splits = [(0, nq), (nq, nkv), (nq, nkv), (nq + nkv, nq)]
    dtypes = [BF16, F32, BF16, F32]
    q, kv, kvb, gate = _proj(xp.reshape(n * s, d), g_pre, w_in_b, splits, dtypes, tm)
    o = _banded(q.reshape(n, s, nq), kvb.reshape(n, s, nkv), slopes, n_heads=n_heads, n_kv=n_kv,
                window=A_WINDOW, stride=1, sinks=sinks)
    yp = _outproj([o.reshape(n * s, nq)], [], gate, xp.reshape(n * s, d), g_post, w_out_b, tm)
    keep = min(A_WINDOW, s)
    state_p = _kv_state(kv.reshape(n, s, nkv)[:, s - keep:], n_kv)
    qs, kvs, _, gs = _proj(xs.reshape(m * t, d), g_pre, w_in_b, splits, dtypes, tm)
    od, nbuf = _dec_window(_to_decode_rows(qs.reshape(m, t, nq), n_kv), kvs.reshape(m, t, nkv),
                           buf.reshape(m, buf.shape[1], nkv), slopes, window=A_WINDOW, dilation=1,
                           past_len=past_len, sinks=sinks)
    ys = _outproj([_from_decode_rows(od, t)], [], gs, xs.reshape(m * t, d), g_post, w_out_b, tm)
    return yp.reshape(n, s, d), ys.reshape(m, t, d), [state_p], [_kv_state(nbuf, n_kv)]


def _mixer_b(xp, xs, cache_cmp, cache_sel, buf_win, page_table, g_pre, g_post, w_in, w_out, cmp_pool,
             past_len, tm):
    n, s, d = xp.shape
    m, t, _ = xs.shape
    n_heads = d // HEAD_DIM
    n_kv = n_heads // 4
    nq, nkv = n_heads * HEAD_DIM, n_kv * KV_LANES
    n_bg = 3 * n_heads
    slopes = _alibi_slopes(n_heads)
    w_in_b, w_out_b = w_in.astype(BF16), w_out.astype(BF16)
    pmats = _pool_mats(cmp_pool)
    c0, s0, w0, g0, b0 = nq, nq + nkv, nq + 2 * nkv, nq + 3 * nkv, 2 * nq + 3 * nkv
    splits = [(0, nq), (c0, nkv), (s0, nkv), (s0, nkv), (w0, nkv), (w0, nkv), (g0, nq), (b0, n_bg)]
    dtypes = [BF16, F32, F32, BF16, F32, BF16, F32, F32]
    q, c_p, s_p, s_pb, w_p, w_pb, gate, bgz = _proj(xp.reshape(n * s, d), g_pre, w_in_b, splits, dtypes, tm)
    q = q.reshape(n, s, nq)
    cmp_e, cmp_o = _pool_prompt(c_p.reshape(n, s, nkv), pmats)
    o_c, sel = _nsa_cmp(q, cmp_e, cmp_o, slopes, n_heads=n_heads, n_kv=n_kv)
    o_s = _nsa_sel(q, s_pb.reshape(n, s, nkv), sel, slopes, n_heads=n_heads, n_kv=n_kv)
    o = _banded(q, w_pb.reshape(n, s, nkv), slopes, n_heads=n_heads, n_kv=n_kv, window=B_WINDOW, stride=1,
                merge=(o_c, o_s, bgz.reshape(n, s, n_bg)))
    yp = _outproj([o.reshape(n * s, nq)], [], gate, xp.reshape(n * s, d), g_post, w_out_b, tm)
    keep = min(B_WINDOW, s)
    states_p = [_kv_state(c_p.reshape(n, s, nkv), n_kv), _kv_state(s_p.reshape(n, s, nkv), n_kv),
                _kv_state(w_p.reshape(n, s, nkv)[:, s - keep:], n_kv)]
    qs, c_s, s_s, _, w_s, _, gs, bgzs = _proj(xs.reshape(m * t, d), g_pre, w_in_b, splits, dtypes, tm)
    qd = _to_decode_rows(qs.reshape(m, t, nq), n_kv)
    c_s, s_s, w_s = (a.reshape(m, t, nkv) for a in (c_s, s_s, w_s))
    pages = lambda cache: cache.reshape(cache.shape[0], cache.shape[1], nkv)
    pe, po = _pool_paged(pages(cache_cmp), page_table, pmats)
    n_w = min(NEW_PAD, B_CMP)
    pnew = jnp.zeros((2, NEW_PAD, NEW_PAD), F32).at[:, 0, :n_w].set(cmp_pool.astype(F32)[:, :n_w]).astype(BF16)
    oc_d, sel_d = _nsa_dec_cmp(qd, pe, po, c_s, pnew, slopes, past_len=past_len)
    os_d = _nsa_dec_sel(qd, sel_d, s_s, pages(cache_sel), page_table, slopes, past_len=past_len)
    ow_d, nwin = _dec_window(qd, w_s, buf_win.reshape(m, buf_win.shape[1], nkv), slopes, window=B_WINDOW,
                             dilation=1, past_len=past_len)
    osm = _nsa_merge(_from_decode_rows(oc_d, t), _from_decode_rows(os_d, t), _from_decode_rows(ow_d, t),
                     bgzs, n_heads)
    ys = _outproj([osm], [], gs, xs.reshape(m * t, d), g_post, w_out_b, tm)
    states_s = [_kv_state(c_s, n_kv), _kv_state(s_s, n_kv), _kv_state(nwin, n_kv)]
    return yp.reshape(n, s, d), ys.reshape(m, t, d), states_p, states_s


def _mixer_c(xp, xs, bufs, g_pre, g_post, w_in, w_out, past_len, tm):
    n, s, d = xp.shape
    m, t, _ = xs.shape
    n_heads = d // HEAD_DIM
    n_kv = n_heads // 8
    ng = len(C_GROUPS)
    nq1, nkv1 = n_heads * HEAD_DIM, n_kv * KV_LANES
    nq, nkv = ng * nq1, ng * nkv1
    slopes = _alibi_slopes(n_heads)
    w_in_b, w_out_b = w_in.astype(BF16), w_out.astype(BF16)
    splits = [(0, nq), (nq, nkv), (nq, nkv), (nq + nkv, nq1)]
    dtypes = [BF16, F32, BF16, F32]
    q, kv, kvb, gate = _proj(xp.reshape(n * s, d), g_pre, w_in_b, splits, dtypes, tm)
    q, kv, kvb = q.reshape(n, s, nq), kv.reshape(n, s, nkv), kvb.reshape(n, s, nkv)
    outs, lses, states_p = [], [], []
    for gi, (w, dil) in enumerate(C_GROUPS):
        o, lse = _banded(q, kvb, slopes, n_heads=n_heads, n_kv=n_kv, window=w, stride=dil, q_group=gi,
                         with_lse=True)
        outs.append(o.reshape(n * s, nq1))
        lses.append(lse.reshape(n * s, nq1))
        keep = min(w, s)
        states_p.append(_kv_state(kv[:, s - keep:, gi * nkv1:(gi + 1) * nkv1], n_kv))
    yp = _outproj(outs, lses, gate, xp.reshape(n * s, d), g_post, w_out_b, tm)
    qs, kvs, _, gs = _proj(xs.reshape(m * t, d), g_pre, w_in_b, splits, dtypes, tm)
    qs, kvs = qs.reshape(m, t, nq), kvs.reshape(m, t, nkv)
    outs, lses, states_s = [], [], []
    for gi, (w, dil) in enumerate(C_GROUPS):
        buf = bufs[gi]
        assert buf.shape[1] == w
        od, ld, nbuf = _dec_window(_to_decode_rows(qs[:, :, gi * nq1:(gi + 1) * nq1], n_kv),
                                   kvs[:, :, gi * nkv1:(gi + 1) * nkv1], buf.reshape(m, w, nkv1), slopes,
                                   window=w, dilation=dil, past_len=past_len, with_lse=True)
        outs.append(_from_decode_rows(od, t))
        lses.append(_from_decode_rows(ld, t))
        states_s.append(_kv_state(nbuf, n_kv))
    ys = _outproj(outs, lses, gs, xs.reshape(m * t, d), g_post, w_out_b, tm)
    return yp.reshape(n, s, d), ys.reshape(m, t, d), states_p, states_s


def kernel(x_prompt, x_sample, state_a0_kv, cache_b1_cmp_kv, cache_b1_sel_kv, state_b1_win_kv,
           state_c2_g0_kv, state_c2_g1_kv, state_c2_g2_kv, state_a3_kv, page_table, norm_g,
           w_in_0, w_out_0, sinks_0, w_in_1, w_out_1, cmp_pool_1, w_in_2, w_out_2,
           w_in_3, w_out_3, sinks_3):
    past_len = page_table.shape[1] * cache_b1_cmp_kv.shape[1]
    tm = 256
    xp, xs = x_prompt, x_sample
    xp, xs, (a0_p,), (a0_s,) = _mixer_a(xp, xs, state_a0_kv, norm_g[0, 0], norm_g[0, 1], w_in_0, w_out_0,
                                        sinks_0, past_len, tm)
    xp, xs, (b1_cmp_p, b1_sel_p, b1_win_p), (b1_cmp_s, b1_sel_s, b1_win_s) = _mixer_b(
        xp, xs, cache_b1_cmp_kv, cache_b1_sel_kv, state_b1_win_kv, page_table, norm_g[1, 0], norm_g[1, 1],
        w_in_1, w_out_1, cmp_pool_1, past_len, tm)
    xp, xs, (c2_g0_p, c2_g1_p, c2_g2_p), (c2_g0_s, c2_g1_s, c2_g2_s) = _mixer_c(
        xp, xs, (state_c2_g0_kv, state_c2_g1_kv, state_c2_g2_kv), norm_g[2, 0], norm_g[2, 1], w_in_2, w_out_2,
        past_len, tm)
    xp, xs, (a3_p,), (a3_s,) = _mixer_a(xp, xs, state_a3_kv, norm_g[3, 0], norm_g[3, 1], w_in_3, w_out_3,
                                        sinks_3, past_len, tm)
    return (xp, xs, a0_p, a0_s, b1_cmp_p, b1_cmp_s, b1_sel_p, b1_sel_s, b1_win_p, b1_win_s,
            c2_g0_p, c2_g0_s, c2_g1_p, c2_g1_s, c2_g2_p, c2_g2_s, a3_p, a3_s)
```

```python
import functools

import jax
import jax.numpy as jnp
from jax import lax
from jax.experimental import pallas as pl
from jax.experimental.pallas import tpu as pltpu

F32 = jnp.float32
BF16 = jnp.bfloat16

HEAD_DIM = 64
KV_LANES = 2 * HEAD_DIM
RMS_EPS = 1e-6
NEG = -1e30
SCALE = HEAD_DIM ** -0.5

A_WINDOW = 128
B_CMP = 32
B_SEL = 64
B_TOPN = 16
B_WINDOW = 512
C_GROUPS = ((128, 1), (512, 4), (2048, 16))

VMEM_LIMIT_BYTES = 48 * 1024 * 1024


def _params(*sem):
    return pltpu.CompilerParams(dimension_semantics=sem, vmem_limit_bytes=VMEM_LIMIT_BYTES)


def _smem():
    return pl.BlockSpec(memory_space=pltpu.SMEM)


def _dot_nt(a, b):
    return lax.dot_general(a, b, (((1,), (1,)), ((), ())), preferred_element_type=F32)


def _dot(a, b):
    return jnp.dot(a, b, preferred_element_type=F32)


def _iota(shape, dim):
    return lax.broadcasted_iota(jnp.int32, shape, dim)


def _log2(d):
    assert d > 0 and d & (d - 1) == 0
    return d.bit_length() - 1


def _fdiv(x, d):
    return lax.shift_right_arithmetic(x, _log2(d))


def _fmod(x, d):
    return x & (d - 1)


def _sum(xs):
    return functools.reduce(lambda a, b: a + b, xs)


def _alibi_slopes(n_heads):
    h = jnp.arange(1, n_heads + 1, dtype=F32)
    return jnp.exp2(-8.0 * h / n_heads)


def _softmax_parts(s, valid, sink=None):
    s = jnp.where(valid, s, NEG)
    m = jnp.max(s, axis=-1, keepdims=True)
    if sink is not None:
        m = jnp.maximum(m, sink)
    e = jnp.where(valid, jnp.exp(s - m), 0.0)
    den = jnp.sum(e, axis=-1, keepdims=True)
    if sink is not None:
        den = den + jnp.exp(sink - m)
    den = jnp.where(den > 0, den, 1.0)
    return e / den, m, den


def _proj_body(x_ref, g_ref, w_ref, *out_refs, splits, chunk, tm, seq_len):
    x = x_ref[...]
    ms = jnp.mean(x * x, axis=-1, keepdims=True)
    h = ((x * lax.rsqrt(ms + RMS_EPS)) * g_ref[...]).astype(BF16)
    for o_ref, (start, width, mode) in zip(out_refs, splits):
        if mode == "kaug":
            nbs = seq_len // B_SEL
            aug = KV_LANES + nbs
            pos = _fmod(pl.program_id(0) * tm + _iota((tm, 1), 0), seq_len)
            onehot = (_iota((1, nbs), 1) == _fdiv(pos, B_SEL)).astype(o_ref.dtype)
            is_k = _iota((1, KV_LANES), 1) < HEAD_DIM
            for j in range(width // KV_LANES):
                z = _dot(h, w_ref[:, start + j * KV_LANES:start + (j + 1) * KV_LANES])
                o_ref[:, j * aug:j * aug + KV_LANES] = jnp.where(is_k, z, 0.0).astype(o_ref.dtype)
                o_ref[:, j * aug + KV_LANES:(j + 1) * aug] = onehot
            continue
        for c0 in range(0, width, chunk):
            cw = min(chunk, width - c0)
            z = _dot(h, w_ref[:, start + c0:start + c0 + cw])
            if mode == "q":
                z = z * SCALE
            o_ref[:, c0:c0 + cw] = z.astype(o_ref.dtype)


_PROJ_DTYPES = {"f32": F32, "bf16": BF16, "q": BF16, "kaug": BF16}


def _proj(x2d, g, w_bf16, splits, tm, seq_len=None):
    m_rows, d = x2d.shape
    n = w_bf16.shape[1]
    tm = min(tm, m_rows)
    assert m_rows % tm == 0

    def out_width(width, mode):
        return width // KV_LANES * (KV_LANES + seq_len // B_SEL) if mode == "kaug" else width

    widths = [out_width(wd, mode) for _, wd, mode in splits]
    return pl.pallas_call(
        functools.partial(_proj_body, splits=tuple(splits), chunk=512, tm=tm, seq_len=seq_len),
        grid=(m_rows // tm,),
        in_specs=[pl.BlockSpec((tm, d), lambda i: (i, 0)),
                  pl.BlockSpec((1, d), lambda i: (0, 0)),
                  pl.BlockSpec((d, n), lambda i: (0, 0))],
        out_specs=[pl.BlockSpec((tm, wd), lambda i: (i, 0)) for wd in widths],
        out_shape=[jax.ShapeDtypeStruct((m_rows, wd), _PROJ_DTYPES[mode])
                   for wd, (_, _, mode) in zip(widths, splits)],
        compiler_params=_params("parallel"),
        name="proj",
    )(x2d, g.reshape(1, d), w_bf16)


def _outproj_body(*refs, n_branch):
    if n_branch == 1:
        o_ref, gate_ref, x_ref, g_ref, w_ref, y_ref = refs
        o = o_ref[...]
    else:
        o_refs = refs[:n_branch]
        l_refs = refs[n_branch:2 * n_branch]
        gate_ref, x_ref, g_ref, w_ref, y_ref = refs[2 * n_branch:]
        ls = [r[...] for r in l_refs]
        m = functools.reduce(jnp.maximum, ls)
        es = [jnp.exp(l - m) for l in ls]
        den = _sum(es)
        o = _sum([(e / den) * r[...] for e, r in zip(es, o_refs)])
    gate = gate_ref[...]
    a = (o * (gate * jax.nn.sigmoid(gate))).astype(BF16)
    y = _dot(a, w_ref[...])
    ms = jnp.mean(y * y, axis=-1, keepdims=True)
    y_ref[...] = x_ref[...] + (y * lax.rsqrt(ms + RMS_EPS)) * g_ref[...]


def _outproj(os_, lses, gate, x2d, g, w_bf16, tm):
    m_rows, d = x2d.shape
    k = w_bf16.shape[0]
    tm = min(tm, m_rows)
    assert m_rows % tm == 0
    row = lambda width: pl.BlockSpec((tm, width), lambda i: (i, 0))
    n_branch = len(os_)
    ins = list(os_) + list(lses) + [gate, x2d, g.reshape(1, d), w_bf16]
    in_specs = ([row(k)] * (n_branch + len(lses)) + [row(k), row(d),
                pl.BlockSpec((1, d), lambda i: (0, 0)), pl.BlockSpec((k, d), lambda i: (0, 0))])
    return pl.pallas_call(
        functools.partial(_outproj_body, n_branch=n_branch),
        grid=(m_rows // tm,),
        in_specs=in_specs,
        out_specs=row(d),
        out_shape=jax.ShapeDtypeStruct((m_rows, d), F32),
        compiler_params=_params("parallel"),
        name="outproj",
    )(*ins)


def _banded_body(*refs, tq, wsub, stride, window, g_heads, has_sink, with_lse, nsa_merge):
    it = iter(refs)
    slopes_ref = next(it)
    sinks_ref = next(it) if has_sink else None
    q_ref = next(it)
    kv_ref = next(it)
    if nsa_merge:
        oc_ref, os_ref, bgz_ref = next(it), next(it), next(it)
    o_ref = next(it)
    lse_ref = next(it) if with_lse else None

    k = pl.program_id(2)
    i = pl.program_id(3)
    j0 = i * tq
    span = wsub + tq
    kstart = pl.multiple_of(jnp.maximum(j0 - wsub, 0), 128)
    kv = kv_ref[0, pl.ds(kstart, span), :]
    kb = kv[:, :HEAD_DIM]
    vb = kv[:, HEAD_DIM:]
    q_pos = j0 + _iota((tq, 1), 0)
    k_pos = kstart + _iota((1, span), 1)
    dist = (q_pos - k_pos) * stride
    valid = (dist >= 0) & (dist <= window)
    distf = dist.astype(F32)
    if nsa_merge:
        bg = jax.nn.sigmoid(bgz_ref[0, 0])
    for g in range(g_heads):
        head = k * g_heads + g
        lo, hi = g * HEAD_DIM, (g + 1) * HEAD_DIM
        s = _dot_nt(q_ref[0, :, lo:hi], kb) - slopes_ref[head] * distf
        sink = sinks_ref[head] if has_sink else None
        p, m, den = _softmax_parts(s, valid, sink)
        o = _dot(p.astype(BF16), vb)
        if nsa_merge:
            col = lambda j: bg[:, j * g_heads + g:j * g_heads + g + 1]
            o = col(0) * oc_ref[0, :, lo:hi] + col(1) * os_ref[0, :, lo:hi] + col(2) * o
        o_ref[0, :, lo:hi] = o
        if with_lse:
            lse_ref[0, :, lo:hi] = jnp.broadcast_to(m + jnp.log(den), (tq, HEAD_DIM))


def _banded(q, kv, slopes, *, n_heads, n_kv, window, stride, q_group=0, sinks=None,
            with_lse=False, merge=None, tq=256):
    n, s, qc = q.shape
    kc = kv.shape[2]
    g_heads = n_heads // n_kv
    gw = g_heads * HEAD_DIM
    sp = s // stride
    wsub = window // stride
    tq = min(tq, sp - wsub)
    assert sp % tq == 0 and wsub % 128 == 0 and tq % 128 == 0 and tq > 0
    qv = q.reshape(n, sp, stride * qc)
    kvv = kv.reshape(n, sp, stride * kc)
    q_blocks, kv_blocks, o_blocks = qc // gw, kc // KV_LANES, n_kv
    q_map = lambda b, r, k, i: (b, i, r * q_blocks + q_group * n_kv + k)
    kv_map = lambda b, r, k, i: (b, 0, r * kv_blocks + q_group * n_kv + k)
    o_map = lambda b, r, k, i: (b, i, r * o_blocks + k)
    ins = [slopes]
    in_specs = [_smem()]
    if sinks is not None:
        ins.append(sinks)
        in_specs.append(_smem())
    ins += [qv, kvv]
    in_specs += [pl.BlockSpec((1, tq, gw), q_map), pl.BlockSpec((1, sp, KV_LANES), kv_map)]
    if merge is not None:
        o_c, o_s, bgz = merge
        bgz = bgz.reshape(n, s, 3, n_kv, g_heads).transpose(0, 3, 1, 2, 4).reshape(n, n_kv, s, 3 * g_heads)
        ins += [o_c, o_s, bgz]
        in_specs += [pl.BlockSpec((1, tq, gw), o_map), pl.BlockSpec((1, tq, gw), o_map),
                     pl.BlockSpec((1, 1, tq, 3 * g_heads), lambda b, r, k, i: (b, k, i, 0))]
    o_sds = jax.ShapeDtypeStruct((n, sp, stride * n_heads * HEAD_DIM), F32)
    o_spec = pl.BlockSpec((1, tq, gw), o_map)
    out = pl.pallas_call(
        functools.partial(_banded_body, tq=tq, wsub=wsub, stride=stride, window=window, g_heads=g_heads,
                          has_sink=sinks is not None, with_lse=with_lse, nsa_merge=merge is not None),
        grid=(n, stride, n_kv, sp // tq),
        in_specs=in_specs,
        out_specs=[o_spec, o_spec] if with_lse else o_spec,
        out_shape=[o_sds, o_sds] if with_lse else o_sds,
        compiler_params=_params("parallel", "parallel", "parallel", "parallel"),
        name="banded_attn",
    )(*ins)
    if with_lse:
        return tuple(a.reshape(n, s, n_heads * HEAD_DIM) for a in out)
    return out.reshape(n, s, n_heads * HEAD_DIM)


POOL_ROWS = 1024


def _pool_body(*refs, n_in, rows_in, paged):
    if paged:
        refs = refs[1:]
    p_ref = refs[0]
    in_refs = refs[1:1 + n_in]
    ce_ref, co_ref = refs[1 + n_in:]
    is_k = _fmod(_iota((1, ce_ref.shape[2]), 1), KV_LANES) < HEAD_DIM
    for parity, out_ref in ((0, ce_ref), (1, co_ref)):
        acc_k = None
        acc_v = None
        for j, r in enumerate(in_refs):
            rows = r[0].astype(BF16)
            pk = p_ref[parity, :, j * rows_in:(j + 1) * rows_in]
            pv = p_ref[2 + parity, :, j * rows_in:(j + 1) * rows_in]
            dk, dv = _dot(pk, rows), _dot(pv, rows)
            acc_k = dk if acc_k is None else acc_k + dk
            acc_v = dv if acc_v is None else acc_v + dv
        out_ref[0] = jnp.where(is_k, acc_k, acc_v)


def _pool_mats(pool_w):
    nb = POOL_ROWS // B_CMP
    eye = jnp.eye(nb, dtype=F32)
    mats = []
    for x in range(2):
        full = jnp.kron(eye, pool_w[x].astype(F32)[None, :])
        mats += [full[0::2], full[1::2]]
    return jnp.stack(mats).astype(BF16)


def _pool_prompt(rows, pmats):
    n, l, c = rows.shape
    assert l % POOL_ROWS == 0
    half = POOL_ROWS // B_CMP // 2
    sds = jax.ShapeDtypeStruct((n, l // B_CMP // 2, c), F32)
    out_spec = pl.BlockSpec((1, half, c), lambda b, j: (b, j, 0))
    return pl.pallas_call(
        functools.partial(_pool_body, n_in=1, rows_in=POOL_ROWS, paged=False),
        grid=(n, l // POOL_ROWS),
        in_specs=[pl.BlockSpec(pmats.shape, lambda b, j: (0, 0, 0)),
                  pl.BlockSpec((1, POOL_ROWS, c), lambda b, j: (b, j, 0))],
        out_specs=[out_spec, out_spec],
        out_shape=[sds, sds],
        compiler_params=_params("parallel", "parallel"),
        name="nsa_pool",
    )(pmats, rows)


def _pages_view(cache):
    return jnp.transpose(cache, (0, 2, 3, 4, 1))


def _pool_paged_body(pt_ref, wt_ref, *refs, per_step, n_kv, page):
    page_refs = refs[:per_step]
    out_ref = refs[per_step]
    j = pl.program_id(1)
    width = out_ref.shape[4]
    half = width // 2

    @pl.when(j == 0)
    def _():
        out_ref[...] = jnp.zeros(out_ref.shape, F32)

    r = _iota((page, 1), 0)
    col = _iota((1, width), 1)
    blk_in_page = _fdiv(r, B_CMP)
    per_page = page // B_CMP
    for u in range(per_step):
        pg = j * per_step + u
        target = jnp.where(_fmod(blk_in_page, 2) == 1, half, 0) + _fdiv(per_page * pg + blk_in_page, 2)
        hit = col == target
        for x in range(2):
            pw = jnp.where(hit, wt_ref[x], 0.0).astype(BF16)
            rows_t = page_refs[u][0, :, x].reshape(n_kv * HEAD_DIM, page).astype(BF16)
            out_ref[0, :, x] += _dot(rows_t, pw).reshape(n_kv, HEAD_DIM, width)


def _pool_paged(cache_t, page_table, pool_w, per_step=16):
    _, n_kv, _, _, page = cache_t.shape
    n, n_pg = page_table.shape
    per_step = min(per_step, n_pg)
    assert n_pg % per_step == 0 and page % (2 * B_CMP) == 0
    width = n_pg * page // B_CMP
    wt = jnp.broadcast_to(jnp.tile(pool_w.astype(F32), (1, page // B_CMP))[:, :, None], (2, page, width))
    page_spec = lambda u: pl.BlockSpec((1, n_kv, 2, HEAD_DIM, page),
                                       lambda b, j, pt: (pt[b, j * per_step + u], 0, 0, 0, 0))
    return pl.pallas_call(
        functools.partial(_pool_paged_body, per_step=per_step, n_kv=n_kv, page=page),
        grid_spec=pltpu.PrefetchScalarGridSpec(
            num_scalar_prefetch=1,
            grid=(n, n_pg // per_step),
            in_specs=[pl.BlockSpec(wt.shape, lambda b, j, pt: (0, 0, 0))]
                     + [page_spec(u) for u in range(per_step)],
            out_specs=pl.BlockSpec((1, n_kv, 2, HEAD_DIM, width), lambda b, j, pt: (b, 0, 0, 0, 0))),
        out_shape=jax.ShapeDtypeStruct((n, n_kv, 2, HEAD_DIM, width), F32),
        compiler_params=_params("parallel", "arbitrary"),
        name="nsa_pool_paged",
    )(page_table, wt, *([cache_t] * per_step))


def _topk_mask(score, n_sel):
    rows, width = score.shape
    lane = _iota((rows, width), 1).astype(F32)
    sel = jnp.zeros((rows, width), F32)
    sc = score
    for _ in range(n_sel):
        mx = jnp.max(sc, axis=-1, keepdims=True)
        idx = jnp.min(jnp.where(sc == mx, lane, float(width)), axis=-1, keepdims=True)
        hit = lane == idx
        sel = jnp.where(hit, 1.0, sel)
        sc = jnp.where(hit, -4.0, sc)
    return sel


def _sel_score(imp, blk, q_pos, g_heads):
    cur = _fdiv(q_pos, B_SEL)
    forced = (blk == 0) | (blk == cur) | (blk == cur - 1)
    return jnp.where(blk * B_SEL <= q_pos, imp + jnp.where(forced, g_heads + 1.0, 0.0), -1.0)


def _nsa_cmp_body(slopes_ref, q_ref, ce_ref, co_ref, oc_ref, selt_ref, any_ref, *, tq, g_heads, n_sel):
    k = pl.program_id(1)
    i = pl.program_id(2)
    nbs = ce_ref.shape[1]
    q_pos = i * tq + _iota((tq, 1), 0)
    blk = _iota((1, nbs), 1)
    dist_e = q_pos - ((2 * blk + 1) * B_CMP - 1)
    dist_o = q_pos - ((2 * blk + 2) * B_CMP - 1)
    ok_e, ok_o = dist_e >= 0, dist_o >= 0
    de, do = dist_e.astype(F32), dist_o.astype(F32)
    ce = ce_ref[0].astype(BF16)
    co = co_ref[0].astype(BF16)
    ke, ve, ko, vo = ce[:, :HEAD_DIM], ce[:, HEAD_DIM:], co[:, :HEAD_DIM], co[:, HEAD_DIM:]
    imp_e = jnp.zeros((tq, nbs), F32)
    imp_o = jnp.zeros((tq, nbs), F32)
    for g in range(g_heads):
        slope = slopes_ref[k * g_heads + g]
        lo, hi = g * HEAD_DIM, (g + 1) * HEAD_DIM
        qg = q_ref[0, :, lo:hi]
        se = jnp.where(ok_e, _dot_nt(qg, ke) - slope * de, NEG)
        so = jnp.where(ok_o, _dot_nt(qg, ko) - slope * do, NEG)
        m = jnp.maximum(jnp.max(se, axis=-1, keepdims=True), jnp.max(so, axis=-1, keepdims=True))
        ee = jnp.where(ok_e, jnp.exp(se - m), 0.0)
        eo = jnp.where(ok_o, jnp.exp(so - m), 0.0)
        den = jnp.sum(ee, axis=-1, keepdims=True) + jnp.sum(eo, axis=-1, keepdims=True)
        den = jnp.where(den > 0, den, 1.0)
        pe, po = ee / den, eo / den
        oc_ref[0, :, lo:hi] = _dot(pe.astype(BF16), ve) + _dot(po.astype(BF16), vo)
        imp_e = imp_e + pe
        imp_o = imp_o + po
    score = _sel_score(imp_e + imp_o, blk, q_pos, g_heads)
    sel = _topk_mask(score, n_sel)
    selt_ref[0, 0] = sel.T.astype(BF16)
    any_ref[0, 0, 0] = jnp.broadcast_to(jnp.max(sel, axis=0, keepdims=True), (T_PAD, nbs))


def _nsa_cmp(q, cmp_e, cmp_o, slopes, *, n_heads, n_kv, tq):
    n, s, _ = q.shape
    g_heads = n_heads // n_kv
    gw = g_heads * HEAD_DIM
    nbs = cmp_e.shape[1]
    n_sel = min(B_TOPN, nbs)
    assert s % tq == 0
    return pl.pallas_call(
        functools.partial(_nsa_cmp_body, tq=tq, g_heads=g_heads, n_sel=n_sel),
        grid=(n, n_kv, s // tq),
        in_specs=[_smem(),
                  pl.BlockSpec((1, tq, gw), lambda b, k, i: (b, i, k)),
                  pl.BlockSpec((1, nbs, KV_LANES), lambda b, k, i: (b, 0, k)),
                  pl.BlockSpec((1, nbs, KV_LANES), lambda b, k, i: (b, 0, k))],
        out_specs=[pl.BlockSpec((1, tq, gw), lambda b, k, i: (b, i, k)),
                   pl.BlockSpec((1, 1, nbs, tq), lambda b, k, i: (b, k, 0, i)),
                   pl.BlockSpec((1, 1, 1, T_PAD, nbs), lambda b, k, i: (b, k, i, 0, 0))],
        out_shape=[jax.ShapeDtypeStruct((n, s, n_heads * HEAD_DIM), F32),
                   jax.ShapeDtypeStruct((n, n_kv, nbs, s), BF16),
                   jax.ShapeDtypeStruct((n, n_kv, s // tq, T_PAD, nbs), F32)],
        compiler_params=_params("parallel", "parallel", "parallel"),
        name="nsa_cmp_select",
    )(slopes, q, cmp_e, cmp_o)


def _flash_update(s, valid, vb, m_ref, l_ref, acc_ref, idx, v_transposed=False):
    s = jnp.where(valid, s, NEG)
    m_prev = m_ref[idx]
    m_new = jnp.maximum(m_prev, jnp.max(s, axis=-1, keepdims=True))
    alpha = jnp.exp(m_prev - m_new)
    e = jnp.where(valid, jnp.exp(s - m_new), 0.0)
    l_ref[idx] = alpha * l_ref[idx] + jnp.sum(e, axis=-1, keepdims=True)
    pv = _dot_nt(e.astype(BF16), vb) if v_transposed else _dot(e.astype(BF16), vb)
    acc_ref[idx] = alpha * acc_ref[idx] + pv
    m_ref[idx] = m_new


def _flash_init(m_ref, l_ref, acc_ref):
    m_ref[...] = jnp.full(m_ref.shape, NEG, F32)
    l_ref[...] = jnp.zeros(l_ref.shape, F32)
    acc_ref[...] = jnp.zeros(acc_ref.shape, F32)


def _flash_result(l_ref, acc_ref, idx):
    l = l_ref[idx]
    return acc_ref[idx] / jnp.where(l > 0, l, 1.0)


MASK_BIG = 1e30
M_INIT = -3e38


def _nsa_sel_body(flags_ref, slopes_ref, qt_ref, selt_ref, kaug_ref, kvt_ref, o_ref,
                  qa_ref, bias_ref, m_ref, l_ref, acc_ref, *, t, g_heads, n_kv, n_q):
    b = pl.program_id(0)
    k = pl.program_id(1)
    i = pl.program_id(2)
    selbias = ((selt_ref[0, 0].astype(F32) - 1.0) * MASK_BIG).astype(BF16)
    r_minus_c = (_iota((1, t), 1) - _iota((t, 1), 0)).astype(F32)
    for g in range(g_heads):
        qa_ref[g, 0:HEAD_DIM, :] = qt_ref[0, g]
        qa_ref[g, HEAD_DIM:KV_LANES, :] = jnp.zeros((HEAD_DIM, t), BF16)
        qa_ref[g, KV_LANES:, :] = selbias
        bias_ref[g] = slopes_ref[k * g_heads + g] * r_minus_c
    m_ref[...] = jnp.full(m_ref.shape, M_INIT, F32)
    l_ref[...] = jnp.zeros(l_ref.shape, F32)
    acc_ref[...] = jnp.zeros(acc_ref.shape, F32)
    causal = jnp.where(_iota((t, 1), 0) > _iota((1, t), 1), NEG, 0.0)

    def chunk(c, diagonal):
        k0 = pl.multiple_of(c * t, t)
        ka = kaug_ref[0, pl.ds(k0, t), :]
        kvt = kvt_ref[0, 0, :, pl.ds(k0, t)]
        tile_dist = jnp.full((1, t), (i - c) * t, jnp.int32).astype(F32)
        for g in range(g_heads):
            s = _dot(ka, qa_ref[g]) - bias_ref[g]
            if diagonal:
                s = s + causal
            delta = slopes_ref[k * g_heads + g] * tile_dist
            m_prev = m_ref[g]
            m_new = jnp.maximum(m_prev, jnp.max(s, axis=0, keepdims=True) - delta)
            alpha = jnp.exp(m_prev - m_new)
            e = jnp.exp(s - (m_new + delta))
            l_ref[g] = alpha * l_ref[g] + jnp.sum(e, axis=0, keepdims=True)
            acc_ref[g] = alpha * acc_ref[g] + _dot(kvt, e.astype(BF16))
            m_ref[g] = m_new

    def maybe_chunk(c, carry):
        @pl.when(flags_ref[((b * n_kv + k) * n_q + i) * n_q + c] != 0)
        def _():
            chunk(c, False)
        return carry

    lax.fori_loop(0, i, maybe_chunk, 0)
    chunk(i, True)
    for g in range(g_heads):
        l = l_ref[g]
        o_t = acc_ref[g, HEAD_DIM:KV_LANES, :] / jnp.where(l > 0, l, 1.0)
        o_ref[0, :, g * HEAD_DIM:(g + 1) * HEAD_DIM] = o_t.T


def _nsa_sel(qt, selt, kaug, kvt, flags, slopes, *, n_heads, n_kv, t):
    n, _, _, s = qt.shape
    g_heads = n_heads // n_kv
    gw = g_heads * HEAD_DIM
    nbs = selt.shape[2]
    aug = KV_LANES + nbs
    n_q = s // t
    assert s % t == 0 and t % B_SEL == 0
    return pl.pallas_call(
        functools.partial(_nsa_sel_body, t=t, g_heads=g_heads, n_kv=n_kv, n_q=n_q),
        grid_spec=pltpu.PrefetchScalarGridSpec(
            num_scalar_prefetch=1,
            grid=(n, n_kv, n_q),
            in_specs=[_smem(),
                      pl.BlockSpec((1, g_heads, HEAD_DIM, t), lambda b, k, i, fl: (b, k, 0, i)),
                      pl.BlockSpec((1, 1, nbs, t), lambda b, k, i, fl: (b, k, 0, i)),
                      pl.BlockSpec((1, s, aug), lambda b, k, i, fl: (b, 0, k)),
                      pl.BlockSpec((1, 1, KV_LANES, s), lambda b, k, i, fl: (b, k, 0, 0))],
            out_specs=pl.BlockSpec((1, t, gw), lambda b, k, i, fl: (b, i, k)),
            scratch_shapes=[pltpu.VMEM((g_heads, aug, t), BF16), pltpu.VMEM((g_heads, t, t), F32),
                            pltpu.VMEM((g_heads, 1, t), F32), pltpu.VMEM((g_heads, 1, t), F32),
                            pltpu.VMEM((g_heads, KV_LANES, t), F32)]),
        out_shape=jax.ShapeDtypeStruct((n, s, n_heads * HEAD_DIM), F32),
        compiler_params=_params("parallel", "parallel", "parallel"),
        name="nsa_selected",
    )(flags, slopes, qt, selt, kaug, kvt)


T_PAD = 8
NEW_PAD = 16
KEY_PAD = 128


def _row_select(ref, base, g_heads, rows):
    gi = _fdiv(_iota((rows, 1), 0), T_PAD)
    out = jnp.zeros((rows, 1), F32)
    for g in range(g_heads):
        out = jnp.where(gi == g, ref[base + g], out)
    return out


def _pad_rows(x, lo, total):
    return jnp.pad(x, ((0, 0), (lo, total - lo - x.shape[1]), (0, 0)))


def _dec_window_body(*refs, lb, t_new, window, dilation, past_len, g_heads, n_kv, has_sink, with_lse):
    it = iter(refs)
    slopes_ref = next(it)
    sinks_ref = next(it) if has_sink else None
    q_ref, new_lo_ref, new_hi_ref, buf_ref = next(it), next(it), next(it), next(it)
    o_ref = next(it)
    lse_ref = next(it) if with_lse else None
    nbuf_ref = next(it)
    keys_ref = next(it)

    rows = g_heads * T_PAD
    span = lb + KEY_PAD
    buf = buf_ref[0]
    keys_ref[0:lb, :] = buf.astype(BF16)
    keys_ref[lb:lb + NEW_PAD, :] = new_lo_ref[0].astype(BF16)
    keys_ref[lb + NEW_PAD:span, :] = jnp.zeros((KEY_PAD - NEW_PAD, keys_ref.shape[1]), BF16)
    shifted = pltpu.roll(buf, lb - t_new, axis=0)
    nbuf_ref[0] = shifted
    is_new = _iota((T_PAD, 1), 0) >= T_PAD - t_new
    nbuf_ref[0, lb - T_PAD:lb, :] = jnp.where(is_new, new_hi_ref[0], shifted[lb - T_PAD:lb])

    t = _fmod(_iota((rows, 1), 0), T_PAD)
    r = _iota((1, span), 1)
    dist = lb + t - r
    valid = (dist >= 0) & (dist <= window) & (past_len - lb + r >= 0)
    if dilation > 1:
        valid = valid & (_fmod(dist, dilation) == 0)
    distf = dist.astype(F32)
    for k in range(n_kv):
        kb = keys_ref[:, k * KV_LANES:k * KV_LANES + HEAD_DIM]
        vb = keys_ref[:, k * KV_LANES + HEAD_DIM:(k + 1) * KV_LANES]
        slope = _row_select(slopes_ref, k * g_heads, g_heads, rows)
        sink = _row_select(sinks_ref, k * g_heads, g_heads, rows) if has_sink else None
        s = _dot_nt(q_ref[0, k], kb) - slope * distf
        p, m, den = _softmax_parts(s, valid, sink)
        o_ref[0, k] = _dot(p.astype(BF16), vb)
        if with_lse:
            lse_ref[0, k] = jnp.broadcast_to(m + jnp.log(den), (rows, HEAD_DIM))


def _dec_window(qd, kv_new, buf, slopes, *, window, dilation, past_len, sinks=None, with_lse=False):
    n, n_kv, rows, _ = qd.shape
    g_heads = rows // T_PAD
    _, t_new, c = kv_new.shape
    lb = buf.shape[1]
    assert lb % NEW_PAD == 0 and t_new <= T_PAD
    new_lo = _pad_rows(kv_new, 0, NEW_PAD)
    new_hi = _pad_rows(kv_new, T_PAD - t_new, T_PAD)
    ins = [slopes]
    in_specs = [_smem()]
    if sinks is not None:
        ins.append(sinks)
        in_specs.append(_smem())
    ins += [qd, new_lo, new_hi, buf]
    o_spec = pl.BlockSpec((1, n_kv, rows, HEAD_DIM), lambda b: (b, 0, 0, 0))
    buf_spec = pl.BlockSpec((1, lb, c), lambda b: (b, 0, 0))
    in_specs += [o_spec, pl.BlockSpec((1, NEW_PAD, c), lambda b: (b, 0, 0)),
                 pl.BlockSpec((1, T_PAD, c), lambda b: (b, 0, 0)), buf_spec]
    o_sds = jax.ShapeDtypeStruct((n, n_kv, rows, HEAD_DIM), F32)
    out_specs = [o_spec] + ([o_spec] if with_lse else []) + [buf_spec]
    out_shape = [o_sds] + ([o_sds] if with_lse else []) + [jax.ShapeDtypeStruct(buf.shape, F32)]
    return pl.pallas_call(
        functools.partial(_dec_window_body, lb=lb, t_new=t_new, window=window, dilation=dilation,
                          past_len=past_len, g_heads=g_heads, n_kv=n_kv, has_sink=sinks is not None,
                          with_lse=with_lse),
        grid=(n,),
        in_specs=in_specs,
        out_specs=out_specs,
        out_shape=out_shape,
        scratch_shapes=[pltpu.VMEM((lb + KEY_PAD, c), BF16)],
        compiler_params=_params("parallel"),
        name="decode_window",
    )(*ins)


def _nsa_dec_cmp_body(slopes_ref, q_ref, cmpt_ref, cnew_ref, pnew_ref, oc_ref, sel_ref, *,
                      past_len, g_heads, n_kv, n_sel):
    rows = g_heads * T_PAD
    width = cmpt_ref.shape[4]
    nbs = width // 2
    t = _fmod(_iota((rows, 1), 0), T_PAD)
    q_pos = past_len + t
    lane = _iota((1, width), 1)
    cmp_blk = jnp.where(lane < nbs, 2 * lane, 2 * (lane - nbs) + 1)
    dist_c = q_pos - ((cmp_blk + 1) * B_CMP - 1)
    lane_n = _iota((1, NEW_PAD), 1)
    dist_ne = q_pos - ((2 * nbs + 1) * B_CMP - 1)
    dist_no = q_pos - ((2 * nbs + 2) * B_CMP - 1)
    cnew = cnew_ref[0].astype(BF16)
    is_k = _fmod(_iota((1, cnew.shape[1]), 1), KV_LANES) < HEAD_DIM
    new_e = jnp.where(is_k, _dot(pnew_ref[0], cnew), _dot(pnew_ref[1], cnew)).astype(BF16)
    new_o = jnp.zeros_like(new_e)
    new_sets = ((new_e, dist_ne, (dist_ne >= 0) & (lane_n == 0)),
                (new_o, dist_no, (dist_no >= 0) & (lane_n == 0)))
    ok_c = dist_c >= 0
    q_pos8 = past_len + _iota((T_PAD, 1), 0)
    blk = _iota((1, nbs), 1)
    lane_p = _iota((1, KEY_PAD), 1)
    for k in range(n_kv):
        c0 = k * KV_LANES
        slope = _row_select(slopes_ref, k * g_heads, g_heads, rows)
        q = q_ref[0, k]
        kt = cmpt_ref[0, k, 0].astype(BF16)
        vt = cmpt_ref[0, k, 1].astype(BF16)
        ss = [jnp.where(ok_c, _dot(q, kt) - slope * dist_c.astype(F32), NEG)]
        oks = [ok_c]
        for keys, dist, ok in new_sets:
            ss.append(jnp.where(ok, _dot_nt(q, keys[:, c0:c0 + HEAD_DIM]) - slope * dist.astype(F32), NEG))
            oks.append(ok)
        m = functools.reduce(jnp.maximum, [jnp.max(s, axis=-1, keepdims=True) for s in ss])
        es = [jnp.where(ok, jnp.exp(s - m), 0.0) for s, ok in zip(ss, oks)]
        den = _sum([jnp.sum(e, axis=-1, keepdims=True) for e in es])
        den = jnp.where(den > 0, den, 1.0)
        ps = [e / den for e in es]
        oc_ref[0, k] = (_dot_nt(ps[0].astype(BF16), vt)
                        + _sum([_dot(p.astype(BF16), keys[:, c0 + HEAD_DIM:c0 + KV_LANES])
                                for p, (keys, _, _) in zip(ps[1:], new_sets)]))
        gsum = lambda p: _sum([p[g * T_PAD:(g + 1) * T_PAD] for g in range(g_heads)])
        imp_c = gsum(ps[0])
        imp = imp_c[:, :nbs] + imp_c[:, nbs:]
        imp_new = gsum(ps[1]) + gsum(ps[2])
        score = _sel_score(imp, blk, q_pos8, g_heads)
        score_new = _sel_score(imp_new[:, 0:1], nbs + lane_p, q_pos8, g_heads)
        score_new = jnp.where(lane_p == 0, score_new, -3.0)
        sel_ref[0, k] = _topk_mask(jnp.concatenate([score, score_new], axis=1), n_sel)


def _nsa_dec_cmp(qd, cmpt, c_new, pnew, slopes, *, past_len):
    n, n_kv, rows, _ = qd.shape
    g_heads = rows // T_PAD
    width = cmpt.shape[4]
    nbs = width // 2
    c = c_new.shape[2]
    n_sel = min(B_TOPN, nbs + 1)
    cnew = _pad_rows(c_new, 0, NEW_PAD)
    q_spec = pl.BlockSpec((1, n_kv, rows, HEAD_DIM), lambda b: (b, 0, 0, 0))
    return pl.pallas_call(
        functools.partial(_nsa_dec_cmp_body, past_len=past_len, g_heads=g_heads, n_kv=n_kv, n_sel=n_sel),
        grid=(n,),
        in_specs=[_smem(), q_spec,
                  pl.BlockSpec((1, n_kv, 2, HEAD_DIM, width), lambda b: (b, 0, 0, 0, 0)),
                  pl.BlockSpec((1, NEW_PAD, c), lambda b: (b, 0, 0)),
                  pl.BlockSpec(pnew.shape, lambda b: (0, 0, 0))],
        out_specs=[q_spec, pl.BlockSpec((1, n_kv, T_PAD, nbs + KEY_PAD), lambda b: (b, 0, 0, 0))],
        out_shape=[jax.ShapeDtypeStruct((n, n_kv, rows, HEAD_DIM), F32),
                   jax.ShapeDtypeStruct((n, n_kv, T_PAD, nbs + KEY_PAD), F32)],
        compiler_params=_params("parallel"),
        name="nsa_decode_cmp_select",
    )(slopes, qd, cmpt, cnew, pnew)


def _nsa_dec_sel_body(*refs, past_len, g_heads, n_kv, per_step, page):
    slopes_ref, q_ref, sel_ref, snew_ref = refs[1:5]
    page_refs = refs[5:5 + per_step]
    o_ref, m_ref, l_ref, acc_ref = refs[5 + per_step:]
    rows = g_heads * T_PAD
    j = pl.program_id(1)
    nsel = sel_ref.shape[3]
    tk = per_step * page
    t = _fmod(_iota((rows, 1), 0), T_PAD)
    q_pos = past_len + t

    @pl.when(j == 0)
    def _():
        _flash_init(m_ref, l_ref, acc_ref)

    k_pos = j * tk + _iota((1, tk), 1)
    expand = (_iota((nsel, 1), 0) == _fdiv(k_pos, B_SEL)).astype(BF16)
    distf = (q_pos - k_pos).astype(F32)
    for k in range(n_kv):
        kt = jnp.concatenate([r[0, k, 0].astype(BF16) for r in page_refs], axis=1)
        vt = jnp.concatenate([r[0, k, 1].astype(BF16) for r in page_refs], axis=1)
        sel = jnp.concatenate([sel_ref[0, k]] * g_heads, axis=0).astype(BF16)
        valid = (_dot(sel, expand) > 0.5) & (k_pos <= q_pos)
        slope = _row_select(slopes_ref, k * g_heads, g_heads, rows)
        s = _dot(q_ref[0, k], kt) - slope * distf
        _flash_update(s, valid, vt, m_ref, l_ref, acc_ref, k, v_transposed=True)

    @pl.when(j == pl.num_programs(1) - 1)
    def _():
        n_past = past_len // B_SEL
        snew = snew_ref[0].astype(BF16)
        dist = t - _iota((1, NEW_PAD), 1)
        for k in range(n_kv):
            c0 = k * KV_LANES
            sel = jnp.concatenate([sel_ref[0, k]] * g_heads, axis=0)
            valid = (sel[:, n_past:n_past + 1] > 0.5) & (dist >= 0)
            slope = _row_select(slopes_ref, k * g_heads, g_heads, rows)
            s = _dot_nt(q_ref[0, k], snew[:, c0:c0 + HEAD_DIM]) - slope * dist.astype(F32)
            _flash_update(s, valid, snew[:, c0 + HEAD_DIM:c0 + KV_LANES], m_ref, l_ref, acc_ref, k)
            o_ref[0, k] = _flash_result(l_ref, acc_ref, k)


def _nsa_dec_sel(qd, sel, s_new, cache_t, page_table, slopes, *, past_len, per_step=8):
    n, n_kv, rows, _ = qd.shape
    g_heads = rows // T_PAD
    page = cache_t.shape[4]
    c = s_new.shape[2]
    n_pg = page_table.shape[1]
    per_step = min(per_step, n_pg)
    assert n_pg % per_step == 0 and past_len == n_pg * page
    snew = _pad_rows(s_new, 0, NEW_PAD)
    q_spec = pl.BlockSpec((1, n_kv, rows, HEAD_DIM), lambda b, j, pt: (b, 0, 0, 0))
    page_spec = lambda u: pl.BlockSpec((1, n_kv, 2, HEAD_DIM, page),
                                       lambda b, j, pt: (pt[b, j * per_step + u], 0, 0, 0, 0))
    return pl.pallas_call(
        functools.partial(_nsa_dec_sel_body, past_len=past_len, g_heads=g_heads, n_kv=n_kv,
                          per_step=per_step, page=page),
        grid_spec=pltpu.PrefetchScalarGridSpec(
            num_scalar_prefetch=1,
            grid=(n, n_pg // per_step),
            in_specs=[_smem(), q_spec,
                      pl.BlockSpec((1, n_kv, T_PAD, sel.shape[3]), lambda b, j, pt: (b, 0, 0, 0)),
                      pl.BlockSpec((1, NEW_PAD, c), lambda b, j, pt: (b, 0, 0))]
                     + [page_spec(u) for u in range(per_step)],
            out_specs=q_spec,
            scratch_shapes=[pltpu.VMEM((n_kv, rows, 1), F32), pltpu.VMEM((n_kv, rows, 1), F32),
                            pltpu.VMEM((n_kv, rows, HEAD_DIM), F32)]),
        out_shape=jax.ShapeDtypeStruct((n, n_kv, rows, HEAD_DIM), F32),
        compiler_params=_params("parallel", "arbitrary"),
        name="nsa_decode_selected",
    )(page_table, slopes, qd, sel, snew, *([cache_t] * per_step))


def _nsa_merge_body(oc_ref, os_ref, ow_ref, bgz_ref, o_ref, *, n_heads):
    bg = jax.nn.sigmoid(bgz_ref[...])
    for h in range(n_heads):
        lo, hi = h * HEAD_DIM, (h + 1) * HEAD_DIM
        o_ref[:, lo:hi] = (bg[:, h:h + 1] * oc_ref[:, lo:hi]
                           + bg[:, n_heads + h:n_heads + h + 1] * os_ref[:, lo:hi]
                           + bg[:, 2 * n_heads + h:2 * n_heads + h + 1] * ow_ref[:, lo:hi])


def _nsa_merge(o_c, o_s, o_w, bgz, n_heads):
    m_rows, d = o_c.shape
    full = lambda a: pl.BlockSpec(a.shape, lambda i: (0, 0))
    return pl.pallas_call(
        functools.partial(_nsa_merge_body, n_heads=n_heads),
        grid=(1,),
        in_specs=[full(o_c), full(o_s), full(o_w), full(bgz)],
        out_specs=full(o_c),
        out_shape=jax.ShapeDtypeStruct((m_rows, d), F32),
        compiler_params=_params("arbitrary"),
        name="nsa_decode_merge",
    )(o_c, o_s, o_w, bgz)


def _to_decode_rows(q, n_kv):
    n, t, hd = q.shape
    g_heads = hd // HEAD_DIM // n_kv
    q = q.reshape(n, t, n_kv, g_heads, HEAD_DIM).transpose(0, 2, 3, 1, 4)
    q = jnp.pad(q, ((0, 0), (0, 0), (0, 0), (0, T_PAD - t), (0, 0)))
    return q.reshape(n, n_kv, g_heads * T_PAD, HEAD_DIM)


def _from_decode_rows(o, t):
    n, n_kv, rows, _ = o.shape
    g_heads = rows // T_PAD
    o = o.reshape(n, n_kv, g_heads, T_PAD, HEAD_DIM)[:, :, :, :t]
    return o.transpose(0, 3, 1, 2, 4).reshape(n * t, n_kv * g_heads * HEAD_DIM)


def _kv_state(rows, n_kv):
    return rows.reshape(*rows.shape[:-1], n_kv, 2, HEAD_DIM)


def _mixer_a(xp, xs, buf, g_pre, g_post, w_in, w_out, sinks, past_len, tm):
    n, s, d = xp.shape
    m, t, _ = xs.shape
    n_heads = d // HEAD_DIM
    n_kv = n_heads // 8
    nq, nkv = n_heads * HEAD_DIM, n_kv * KV_LANES
    slopes = _alibi_slopes(n_heads)
    w_in_b, w_out_b = w_in.astype(BF16), w_out.astype(BF16)
    splits = [(0, nq, "q"), (nq, nkv, "f32"), (nq, nkv, "bf16"), (nq + nkv, nq, "f32")]
    q, kv, kvb, gate = _proj(xp.reshape(n * s, d), g_pre, w_in_b, splits, tm)
    o = _banded(q.reshape(n, s, nq), kvb.reshape(n, s, nkv), slopes, n_heads=n_heads, n_kv=n_kv,
                window=A_WINDOW, stride=1, sinks=sinks)
    yp = _outproj([o.reshape(n * s, nq)], [], gate, xp.reshape(n * s, d), g_post, w_out_b, tm)
    keep = min(A_WINDOW, s)
    state_p = _kv_state(kv.reshape(n, s, nkv)[:, s - keep:], n_kv)
    qs, kvs, _, gs = _proj(xs.reshape(m * t, d), g_pre, w_in_b, splits, tm)
    od, nbuf = _dec_window(_to_decode_rows(qs.reshape(m, t, nq), n_kv), kvs.reshape(m, t, nkv),
                           buf.reshape(m, buf.shape[1], nkv), slopes, window=A_WINDOW, dilation=1,
                           past_len=past_len, sinks=sinks)
    ys = _outproj([_from_decode_rows(od, t)], [], gs, xs.reshape(m * t, d), g_post, w_out_b, tm)
    return yp.reshape(n, s, d), ys.reshape(m, t, d), [state_p], [_kv_state(nbuf, n_kv)]


def _mixer_b(xp, xs, cache_cmp, cache_sel, buf_win, page_table, g_pre, g_post, w_in, w_out, cmp_pool,
             past_len, tm):
    n, s, d = xp.shape
    m, t, _ = xs.shape
    n_heads = d // HEAD_DIM
    n_kv = n_heads // 4
    nq, nkv = n_heads * HEAD_DIM, n_kv * KV_LANES
    n_bg = 3 * n_heads
    slopes = _alibi_slopes(n_heads)
    w_in_b, w_out_b = w_in.astype(BF16), w_out.astype(BF16)
    pmats = _pool_mats(cmp_pool)
    c0, s0, w0, g0, b0 = nq, nq + nkv, nq + 2 * nkv, nq + 3 * nkv, 2 * nq + 3 * nkv
    base = [(0, nq, "q"), (c0, nkv, "f32"), (s0, nkv, "f32"), (w0, nkv, "f32"), (g0, nq, "f32"),
            (b0, n_bg, "f32")]
    splits = base + [(s0, nkv, "bf16"), (s0, nkv, "kaug"), (w0, nkv, "bf16")]
    q, c_p, s_p, w_p, gate, bgz, s_pb, kaug, w_pb = _proj(xp.reshape(n * s, d), g_pre, w_in_b, splits, tm,
                                                          seq_len=s)
    q = q.reshape(n, s, nq)
    cmp_e, cmp_o = _pool_prompt(c_p.reshape(n, s, nkv), pmats)
    t_sel = min(256, s)
    o_c, selt, sel_any = _nsa_cmp(q, cmp_e, cmp_o, slopes, n_heads=n_heads, n_kv=n_kv, tq=t_sel)
    per_tile = t_sel // B_SEL
    flags = (sel_any[:, :, :, 0, :].reshape(n, n_kv, s // t_sel, s // t_sel, per_tile).max(axis=-1) > 0)
    flags = flags.astype(jnp.int32).reshape(-1)
    qt = q.reshape(n, s, n_heads, HEAD_DIM).transpose(0, 2, 3, 1)
    kvt = s_pb.reshape(n, s, n_kv, KV_LANES).transpose(0, 2, 3, 1)
    o_s = _nsa_sel(qt, selt, kaug.reshape(n, s, -1), kvt, flags, slopes, n_heads=n_heads, n_kv=n_kv, t=t_sel)
    o = _banded(q, w_pb.reshape(n, s, nkv), slopes, n_heads=n_heads, n_kv=n_kv, window=B_WINDOW, stride=1,
                merge=(o_c, o_s, bgz.reshape(n, s, n_bg)))
    yp = _outproj([o.reshape(n * s, nq)], [], gate, xp.reshape(n * s, d), g_post, w_out_b, tm)
    keep = min(B_WINDOW, s)
    states_p = [_kv_state(c_p.reshape(n, s, nkv), n_kv), _kv_state(s_p.reshape(n, s, nkv), n_kv),
                _kv_state(w_p.reshape(n, s, nkv)[:, s - keep:], n_kv)]
    qs, c_s, s_s, w_s, gs, bgzs = _proj(xs.reshape(m * t, d), g_pre, w_in_b, base, tm)
    qd = _to_decode_rows(qs.reshape(m, t, nq), n_kv)
    c_s, s_s, w_s = (a.reshape(m, t, nkv) for a in (c_s, s_s, w_s))
    cmpt = _pool_paged(_pages_view(cache_cmp), page_table, cmp_pool)
    n_w = min(NEW_PAD, B_CMP)
    pnew = jnp.zeros((2, NEW_PAD, NEW_PAD), F32).at[:, 0, :n_w].set(cmp_pool.astype(F32)[:, :n_w]).astype(BF16)
    oc_d, sel_d = _nsa_dec_cmp(qd, cmpt, c_s, pnew, slopes, past_len=past_len)
    os_d = _nsa_dec_sel(qd, sel_d, s_s, _pages_view(cache_sel), page_table, slopes, past_len=past_len)
    ow_d, nwin = _dec_window(qd, w_s, buf_win.reshape(m, buf_win.shape[1], nkv), slopes, window=B_WINDOW,
                             dilation=1, past_len=past_len)
    osm = _nsa_merge(_from_decode_rows(oc_d, t), _from_decode_rows(os_d, t), _from_decode_rows(ow_d, t),
                     bgzs, n_heads)
    ys = _outproj([osm], [], gs, xs.reshape(m * t, d), g_post, w_out_b, tm)
    states_s = [_kv_state(c_s, n_kv), _kv_state(s_s, n_kv), _kv_state(nwin, n_kv)]
    return yp.reshape(n, s, d), ys.reshape(m, t, d), states_p, states_s


def _mixer_c(xp, xs, bufs, g_pre, g_post, w_in, w_out, past_len, tm):
    n, s, d = xp.shape
    m, t, _ = xs.shape
    n_heads = d // HEAD_DIM
    n_kv = n_heads // 8
    ng = len(C_GROUPS)
    nq1, nkv1 = n_heads * HEAD_DIM, n_kv * KV_LANES
    nq, nkv = ng * nq1, ng * nkv1
    slopes = _alibi_slopes(n_heads)
    w_in_b, w_out_b = w_in.astype(BF16), w_out.astype(BF16)
    splits = [(0, nq, "q"), (nq, nkv, "f32"), (nq, nkv, "bf16"), (nq + nkv, nq1, "f32")]
    q, kv, kvb, gate = _proj(xp.reshape(n * s, d), g_pre, w_in_b, splits, tm)
    q, kv, kvb = q.reshape(n, s, nq), kv.reshape(n, s, nkv), kvb.reshape(n, s, nkv)
    outs, lses, states_p = [], [], []
    for gi, (w, dil) in enumerate(C_GROUPS):
        o, lse = _banded(q, kvb, slopes, n_heads=n_heads, n_kv=n_kv, window=w, stride=dil, q_group=gi,
                         with_lse=True)
        outs.append(o.reshape(n * s, nq1))
        lses.append(lse.reshape(n * s, nq1))
        keep = min(w, s)
        states_p.append(_kv_state(kv[:, s - keep:, gi * nkv1:(gi + 1) * nkv1], n_kv))
    yp = _outproj(outs, lses, gate, xp.reshape(n * s, d), g_post, w_out_b, tm)
    qs, kvs, _, gs = _proj(xs.reshape(m * t, d), g_pre, w_in_b, splits, tm)
    qs, kvs = qs.reshape(m, t, nq), kvs.reshape(m, t, nkv)
    outs, lses, states_s = [], [], []
    for gi, (w, dil) in enumerate(C_GROUPS):
        buf = bufs[gi]
        assert buf.shape[1] == w
        od, ld, nbuf = _dec_window(_to_decode_rows(qs[:, :, gi * nq1:(gi + 1) * nq1], n_kv),
                                   kvs[:, :, gi * nkv1:(gi + 1) * nkv1], buf.reshape(m, w, nkv1), slopes,
                                   window=w, dilation=dil, past_len=past_len, with_lse=True)
        outs.append(_from_decode_rows(od, t))
        lses.append(_from_decode_rows(ld, t))
        states_s.append(_kv_state(nbuf, n_kv))
    ys = _outproj(outs, lses, gs, xs.reshape(m * t, d), g_post, w_out_b, tm)
    return yp.reshape(n, s, d), ys.reshape(m, t, d), states_p, states_s


def kernel(x_prompt, x_sample, state_a0_kv, cache_b1_cmp_kv, cache_b1_sel_kv, state_b1_win_kv,
           state_c2_g0_kv, state_c2_g1_kv, state_c2_g2_kv, state_a3_kv, page_table, norm_g,
           w_in_0, w_out_0, sinks_0, w_in_1, w_out_1, cmp_pool_1, w_in_2, w_out_2,
           w_in_3, w_out_3, sinks_3):
    past_len = page_table.shape[1] * cache_b1_cmp_kv.shape[1]
    tm = 256
    xp, xs = x_prompt, x_sample
    xp, xs, (a0_p,), (a0_s,) = _mixer_a(xp, xs, state_a0_kv, norm_g[0, 0], norm_g[0, 1], w_in_0, w_out_0,
                                        sinks_0, past_len, tm)
    xp, xs, (b1_cmp_p, b1_sel_p, b1_win_p), (b1_cmp_s, b1_sel_s, b1_win_s) = _mixer_b(
        xp, xs, cache_b1_cmp_kv, cache_b1_sel_kv, state_b1_win_kv, page_table, norm_g[1, 0], norm_g[1, 1],
        w_in_1, w_out_1, cmp_pool_1, past_len, tm)
    xp, xs, (c2_g0_p, c2_g1_p, c2_g2_p), (c2_g0_s, c2_g1_s, c2_g2_s) = _mixer_c(
        xp, xs, (state_c2_g0_kv, state_c2_g1_kv, state_c2_g2_kv), norm_g[2, 0], norm_g[2, 1], w_in_2, w_out_2,
        past_len, tm)
    xp, xs, (a3_p,), (a3_s,) = _mixer_a(xp, xs, state_a3_kv, norm_g[3, 0], norm_g[3, 1], w_in_3, w_out_3,
                                        sinks_3, past_len, tm)
    return (xp, xs, a0_p, a0_s, b1_cmp_p, b1_cmp_s, b1_sel_p, b1_sel_s, b1_win_p, b1_win_s,
            c2_g0_p, c2_g0_s, c2_g1_p, c2_g1_s, c2_g2_p, c2_g2_s, a3_p, a3_s)
```

```python
import functools

import jax
import jax.numpy as jnp
from jax import lax
from jax.experimental import pallas as pl
from jax.experimental.pallas import tpu as pltpu

F32 = jnp.float32
BF16 = jnp.bfloat16

HEAD_DIM = 64
KV_LANES = 2 * HEAD_DIM
RMS_EPS = 1e-6
NEG = -1e30
SCALE = HEAD_DIM ** -0.5

A_WINDOW = 128
B_CMP = 32
B_SEL = 64
B_TOPN = 16
B_WINDOW = 512
C_GROUPS = ((128, 1), (512, 4), (2048, 16))

VMEM_LIMIT_BYTES = 48 * 1024 * 1024


def _params(*sem):
    return pltpu.CompilerParams(dimension_semantics=sem, vmem_limit_bytes=VMEM_LIMIT_BYTES)


def _smem():
    return pl.BlockSpec(memory_space=pltpu.SMEM)


def _dot_nt(a, b):
    return lax.dot_general(a, b, (((1,), (1,)), ((), ())), preferred_element_type=F32)


def _dot(a, b):
    return jnp.dot(a, b, preferred_element_type=F32)


def _iota(shape, dim):
    return lax.broadcasted_iota(jnp.int32, shape, dim)


def _log2(d):
    assert d > 0 and d & (d - 1) == 0
    return d.bit_length() - 1


def _fdiv(x, d):
    return lax.shift_right_arithmetic(x, _log2(d))


def _fmod(x, d):
    return x & (d - 1)


def _sum(xs):
    return functools.reduce(lambda a, b: a + b, xs)


def _alibi_slopes(n_heads):
    h = jnp.arange(1, n_heads + 1, dtype=F32)
    return jnp.exp2(-8.0 * h / n_heads)


def _softmax_parts(s, valid, sink=None):
    s = jnp.where(valid, s, NEG)
    m = jnp.max(s, axis=-1, keepdims=True)
    if sink is not None:
        m = jnp.maximum(m, sink)
    e = jnp.where(valid, jnp.exp(s - m), 0.0)
    den = jnp.sum(e, axis=-1, keepdims=True)
    if sink is not None:
        den = den + jnp.exp(sink - m)
    den = jnp.where(den > 0, den, 1.0)
    return e / den, m, den


def _proj_body(x_ref, g_ref, w_ref, *out_refs, splits, chunk, tm, seq_len):
    x = x_ref[...]
    ms = jnp.mean(x * x, axis=-1, keepdims=True)
    h = ((x * lax.rsqrt(ms + RMS_EPS)) * g_ref[...]).astype(BF16)
    for o_ref, (start, width, mode) in zip(out_refs, splits):
        if mode == "kaug":
            nbs = seq_len // B_SEL
            aug = KV_LANES + nbs
            pos = _fmod(pl.program_id(0) * tm + _iota((tm, 1), 0), seq_len)
            onehot = (_iota((1, nbs), 1) == _fdiv(pos, B_SEL)).astype(o_ref.dtype)
            is_k = _iota((1, KV_LANES), 1) < HEAD_DIM
            for j in range(width // KV_LANES):
                z = _dot(h, w_ref[:, start + j * KV_LANES:start + (j + 1) * KV_LANES])
                o_ref[:, j * aug:j * aug + KV_LANES] = jnp.where(is_k, z, 0.0).astype(o_ref.dtype)
                o_ref[:, j * aug + KV_LANES:(j + 1) * aug] = onehot
            continue
        for c0 in range(0, width, chunk):
            cw = min(chunk, width - c0)
            z = _dot(h, w_ref[:, start + c0:start + c0 + cw])
            if mode == "q":
                z = z * SCALE
            o_ref[:, c0:c0 + cw] = z.astype(o_ref.dtype)


_PROJ_DTYPES = {"f32": F32, "bf16": BF16, "q": BF16, "kaug": BF16}


def _proj(x2d, g, w_bf16, splits, tm, seq_len=None):
    m_rows, d = x2d.shape
    n = w_bf16.shape[1]
    tm = min(tm, m_rows)
    assert m_rows % tm == 0

    def out_width(width, mode):
        return width // KV_LANES * (KV_LANES + seq_len // B_SEL) if mode == "kaug" else width

    widths = [out_width(wd, mode) for _, wd, mode in splits]
    return pl.pallas_call(
        functools.partial(_proj_body, splits=tuple(splits), chunk=512, tm=tm, seq_len=seq_len),
        grid=(m_rows // tm,),
        in_specs=[pl.BlockSpec((tm, d), lambda i: (i, 0)),
                  pl.BlockSpec((1, d), lambda i: (0, 0)),
                  pl.BlockSpec((d, n), lambda i: (0, 0))],
        out_specs=[pl.BlockSpec((tm, wd), lambda i: (i, 0)) for wd in widths],
        out_shape=[jax.ShapeDtypeStruct((m_rows, wd), _PROJ_DTYPES[mode])
                   for wd, (_, _, mode) in zip(widths, splits)],
        compiler_params=_params("parallel"),
        name="proj",
    )(x2d, g.reshape(1, d), w_bf16)


def _outproj_body(*refs, n_branch):
    if n_branch == 1:
        o_ref, gate_ref, x_ref, g_ref, w_ref, y_ref = refs
        o = o_ref[...]
    else:
        o_refs = refs[:n_branch]
        l_refs = refs[n_branch:2 * n_branch]
        gate_ref, x_ref, g_ref, w_ref, y_ref = refs[2 * n_branch:]
        ls = [r[...] for r in l_refs]
        m = functools.reduce(jnp.maximum, ls)
        es = [jnp.exp(l - m) for l in ls]
        den = _sum(es)
        o = _sum([(e / den) * r[...] for e, r in zip(es, o_refs)])
    gate = gate_ref[...]
    a = (o * (gate * jax.nn.sigmoid(gate))).astype(BF16)
    y = _dot(a, w_ref[...])
    ms = jnp.mean(y * y, axis=-1, keepdims=True)
    y_ref[...] = x_ref[...] + (y * lax.rsqrt(ms + RMS_EPS)) * g_ref[...]


def _outproj(os_, lses, gate, x2d, g, w_bf16, tm):
    m_rows, d = x2d.shape
    k = w_bf16.shape[0]
    tm = min(tm, m_rows)
    assert m_rows % tm == 0
    row = lambda width: pl.BlockSpec((tm, width), lambda i: (i, 0))
    n_branch = len(os_)
    ins = list(os_) + list(lses) + [gate, x2d, g.reshape(1, d), w_bf16]
    in_specs = ([row(k)] * (n_branch + len(lses)) + [row(k), row(d),
                pl.BlockSpec((1, d), lambda i: (0, 0)), pl.BlockSpec((k, d), lambda i: (0, 0))])
    return pl.pallas_call(
        functools.partial(_outproj_body, n_branch=n_branch),
        grid=(m_rows // tm,),
        in_specs=in_specs,
        out_specs=row(d),
        out_shape=jax.ShapeDtypeStruct((m_rows, d), F32),
        compiler_params=_params("parallel"),
        name="outproj",
    )(*ins)


SUB_Q = 128


def _head_rows(ref, base, g_heads, rows_per_head):
    rows = g_heads * rows_per_head
    gi = _fdiv(_iota((rows, 1), 0), rows_per_head)
    out = jnp.zeros((rows, 1), F32)
    for g in range(g_heads):
        out = jnp.where(gi == g, ref[base + g], out)
    return out


def _banded_body(*refs, tq, wsub, stride, window, g_heads, has_sink, with_lse, nsa_merge):
    it = iter(refs)
    slopes_ref = next(it)
    sinks_ref = next(it) if has_sink else None
    q_ref = next(it)
    kv_ref = next(it)
    if nsa_merge:
        oc_ref, os_ref, bgz_ref = next(it), next(it), next(it)
    o_ref = next(it)
    lse_ref = next(it) if with_lse else None
    bias_ref = next(it)

    k = pl.program_id(2)
    i = pl.program_id(3)
    span = wsub + SUB_Q
    rows = g_heads * SUB_Q
    n_var = wsub // SUB_Q + 1

    @pl.when(i == 0)
    def _():
        r = _fmod(_iota((rows, 1), 0), SUB_Q)
        c = _iota((1, span), 1)
        slope = _head_rows(slopes_ref, k * g_heads, g_heads, SUB_Q)
        for v in range(n_var):
            dist = (wsub - SUB_Q * v + r - c) * stride
            ok = (dist >= 0) & (dist <= window)
            bias_ref[v] = jnp.where(ok, -(slope * dist.astype(F32)), NEG)

    sink = _head_rows(sinks_ref, k * g_heads, g_heads, SUB_Q) if has_sink else None
    for sb in range(tq // SUB_Q):
        r0 = sb * SUB_Q
        q0 = i * tq + r0
        kst = pl.multiple_of(jnp.maximum(q0 - wsub, 0), SUB_Q)
        var = _fdiv(jnp.maximum(wsub - q0, 0), SUB_Q)
        kv = kv_ref[0, pl.ds(kst, span), :]
        qs = jnp.concatenate([q_ref[0, r0:r0 + SUB_Q, g * HEAD_DIM:(g + 1) * HEAD_DIM]
                              for g in range(g_heads)], axis=0)
        s = _dot_nt(qs, kv[:, :HEAD_DIM]) + bias_ref[var]
        m = jnp.max(s, axis=-1, keepdims=True)
        if has_sink:
            m = jnp.maximum(m, sink)
        e = jnp.exp(s - m)
        den = jnp.sum(e, axis=-1, keepdims=True)
        if has_sink:
            den = den + jnp.exp(sink - m)
        o = _dot(e.astype(BF16), kv[:, HEAD_DIM:]) / den
        if with_lse:
            lse = m + jnp.log(den)
        if nsa_merge:
            bg = jax.nn.sigmoid(bgz_ref[0, 0, r0:r0 + SUB_Q, :])
        for g in range(g_heads):
            lo, hi = g * HEAD_DIM, (g + 1) * HEAD_DIM
            og = o[g * SUB_Q:(g + 1) * SUB_Q]
            if nsa_merge:
                col = lambda j: bg[:, j * g_heads + g:j * g_heads + g + 1]
                og = (col(0) * oc_ref[0, r0:r0 + SUB_Q, lo:hi] + col(1) * os_ref[0, r0:r0 + SUB_Q, lo:hi]
                      + col(2) * og)
            o_ref[0, r0:r0 + SUB_Q, lo:hi] = og
            if with_lse:
                lse_ref[0, r0:r0 + SUB_Q, lo:hi] = jnp.broadcast_to(lse[g * SUB_Q:(g + 1) * SUB_Q],
                                                                    (SUB_Q, HEAD_DIM))


def _banded(q, kv, slopes, *, n_heads, n_kv, window, stride, q_group=0, sinks=None,
            with_lse=False, merge=None, tq=512):
    n, s, qc = q.shape
    kc = kv.shape[2]
    g_heads = n_heads // n_kv
    gw = g_heads * HEAD_DIM
    sp = s // stride
    wsub = window // stride
    tq = min(tq, sp)
    assert sp % tq == 0 and wsub % SUB_Q == 0 and tq % SUB_Q == 0 and wsub + SUB_Q <= sp
    qv = q.reshape(n, sp, stride * qc)
    kvv = kv.reshape(n, sp, stride * kc)
    q_blocks, kv_blocks, o_blocks = qc // gw, kc // KV_LANES, n_kv
    q_map = lambda b, r, k, i: (b, i, r * q_blocks + q_group * n_kv + k)
    kv_map = lambda b, r, k, i: (b, 0, r * kv_blocks + q_group * n_kv + k)
    o_map = lambda b, r, k, i: (b, i, r * o_blocks + k)
    ins = [slopes]
    in_specs = [_smem()]
    if sinks is not None:
        ins.append(sinks)
        in_specs.append(_smem())
    ins += [qv, kvv]
    in_specs += [pl.BlockSpec((1, tq, gw), q_map), pl.BlockSpec((1, sp, KV_LANES), kv_map)]
    if merge is not None:
        o_c, o_s, bgz = merge
        bgz = bgz.reshape(n, s, 3, n_kv, g_heads).transpose(0, 3, 1, 2, 4).reshape(n, n_kv, s, 3 * g_heads)
        ins += [o_c, o_s, bgz]
        in_specs += [pl.BlockSpec((1, tq, gw), o_map), pl.BlockSpec((1, tq, gw), o_map),
                     pl.BlockSpec((1, 1, tq, 3 * g_heads), lambda b, r, k, i: (b, k, i, 0))]
    o_sds = jax.ShapeDtypeStruct((n, sp, stride * n_heads * HEAD_DIM), F32)
    o_spec = pl.BlockSpec((1, tq, gw), o_map)
    out = pl.pallas_call(
        functools.partial(_banded_body, tq=tq, wsub=wsub, stride=stride, window=window, g_heads=g_heads,
                          has_sink=sinks is not None, with_lse=with_lse, nsa_merge=merge is not None),
        grid=(n, stride, n_kv, sp // tq),
        in_specs=in_specs,
        out_specs=[o_spec, o_spec] if with_lse else o_spec,
        out_shape=[o_sds, o_sds] if with_lse else o_sds,
        scratch_shapes=[pltpu.VMEM((wsub // SUB_Q + 1, g_heads * SUB_Q, wsub + SUB_Q), F32)],
        compiler_params=_params("parallel", "parallel", "parallel", "arbitrary"),
        name="banded_attn",
    )(*ins)
    if with_lse:
        return tuple(a.reshape(n, s, n_heads * HEAD_DIM) for a in out)
    return out.reshape(n, s, n_heads * HEAD_DIM)


POOL_ROWS = 1024


def _pool_body(*refs, n_in, rows_in, paged):
    if paged:
        refs = refs[1:]
    p_ref = refs[0]
    in_refs = refs[1:1 + n_in]
    ce_ref, co_ref = refs[1 + n_in:]
    is_k = _fmod(_iota((1, ce_ref.shape[2]), 1), KV_LANES) < HEAD_DIM
    for parity, out_ref in ((0, ce_ref), (1, co_ref)):
        acc_k = None
        acc_v = None
        for j, r in enumerate(in_refs):
            rows = r[0].astype(BF16)
            pk = p_ref[parity, :, j * rows_in:(j + 1) * rows_in]
            pv = p_ref[2 + parity, :, j * rows_in:(j + 1) * rows_in]
            dk, dv = _dot(pk, rows), _dot(pv, rows)
            acc_k = dk if acc_k is None else acc_k + dk
            acc_v = dv if acc_v is None else acc_v + dv
        out_ref[0] = jnp.where(is_k, acc_k, acc_v)


def _pool_mats(pool_w):
    nb = POOL_ROWS // B_CMP
    eye = jnp.eye(nb, dtype=F32)
    mats = []
    for x in range(2):
        full = jnp.kron(eye, pool_w[x].astype(F32)[None, :])
        mats += [full[0::2], full[1::2]]
    return jnp.stack(mats).astype(BF16)


def _pool_prompt(rows, pmats):
    n, l, c = rows.shape
    assert l % POOL_ROWS == 0
    half = POOL_ROWS // B_CMP // 2
    sds = jax.ShapeDtypeStruct((n, l // B_CMP // 2, c), F32)
    out_spec = pl.BlockSpec((1, half, c), lambda b, j: (b, j, 0))
    return pl.pallas_call(
        functools.partial(_pool_body, n_in=1, rows_in=POOL_ROWS, paged=False),
        grid=(n, l // POOL_ROWS),
        in_specs=[pl.BlockSpec(pmats.shape, lambda b, j: (0, 0, 0)),
                  pl.BlockSpec((1, POOL_ROWS, c), lambda b, j: (b, j, 0))],
        out_specs=[out_spec, out_spec],
        out_shape=[sds, sds],
        compiler_params=_params("parallel", "parallel"),
        name="nsa_pool",
    )(pmats, rows)


def _pages_view(cache):
    return jnp.transpose(cache, (0, 2, 3, 4, 1))


def _pool_paged_body(pt_ref, wt_ref, *refs, per_step, n_kv, page):
    page_refs = refs[:per_step]
    out_ref = refs[per_step]
    j = pl.program_id(1)
    width = out_ref.shape[4]
    half = width // 2

    @pl.when(j == 0)
    def _():
        out_ref[...] = jnp.zeros(out_ref.shape, F32)

    r = _iota((page, 1), 0)
    col = _iota((1, width), 1)
    blk_in_page = _fdiv(r, B_CMP)
    per_page = page // B_CMP
    for u in range(per_step):
        pg = j * per_step + u
        target = jnp.where(_fmod(blk_in_page, 2) == 1, half, 0) + _fdiv(per_page * pg + blk_in_page, 2)
        hit = col == target
        for x in range(2):
            pw = jnp.where(hit, wt_ref[x], 0.0).astype(BF16)
            rows_t = page_refs[u][0, :, x].reshape(n_kv * HEAD_DIM, page).astype(BF16)
            out_ref[0, :, x] += _dot(rows_t, pw).reshape(n_kv, HEAD_DIM, width)


def _pool_paged(cache_t, page_table, pool_w, per_step=16):
    _, n_kv, _, _, page = cache_t.shape
    n, n_pg = page_table.shape
    per_step = min(per_step, n_pg)
    assert n_pg % per_step == 0 and page % (2 * B_CMP) == 0
    width = n_pg * page // B_CMP
    wt = jnp.broadcast_to(jnp.tile(pool_w.astype(F32), (1, page // B_CMP))[:, :, None], (2, page, width))
    page_spec = lambda u: pl.BlockSpec((1, n_kv, 2, HEAD_DIM, page),
                                       lambda b, j, pt: (pt[b, j * per_step + u], 0, 0, 0, 0))
    return pl.pallas_call(
        functools.partial(_pool_paged_body, per_step=per_step, n_kv=n_kv, page=page),
        grid_spec=pltpu.PrefetchScalarGridSpec(
            num_scalar_prefetch=1,
            grid=(n, n_pg // per_step),
            in_specs=[pl.BlockSpec(wt.shape, lambda b, j, pt: (0, 0, 0))]
                     + [page_spec(u) for u in range(per_step)],
            out_specs=pl.BlockSpec((1, n_kv, 2, HEAD_DIM, width), lambda b, j, pt: (b, 0, 0, 0, 0))),
        out_shape=jax.ShapeDtypeStruct((n, n_kv, 2, HEAD_DIM, width), F32),
        compiler_params=_params("parallel", "arbitrary"),
        name="nsa_pool_paged",
    )(page_table, wt, *([cache_t] * per_step))


def _topk_mask(score, n_sel):
    rows, width = score.shape
    lane = _iota((rows, width), 1).astype(F32)
    sel = jnp.zeros((rows, width), F32)
    sc = score
    for _ in range(n_sel):
        mx = jnp.max(sc, axis=-1, keepdims=True)
        idx = jnp.min(jnp.where(sc == mx, lane, float(width)), axis=-1, keepdims=True)
        hit = lane == idx
        sel = jnp.where(hit, 1.0, sel)
        sc = jnp.where(hit, -4.0, sc)
    return sel


def _sel_score(imp, blk, q_pos, g_heads):
    cur = _fdiv(q_pos, B_SEL)
    forced = (blk == 0) | (blk == cur) | (blk == cur - 1)
    return jnp.where(blk * B_SEL <= q_pos, imp + jnp.where(forced, g_heads + 1.0, 0.0), -1.0)


def _nsa_cmp_body(slopes_ref, q_ref, ce_ref, co_ref, oc_ref, selt_ref, any_ref, *, tq, g_heads, n_sel):
    k = pl.program_id(1)
    i = pl.program_id(2)
    nbs = ce_ref.shape[1]
    q_pos = i * tq + _iota((tq, 1), 0)
    blk = _iota((1, nbs), 1)
    dist_e = q_pos - ((2 * blk + 1) * B_CMP - 1)
    dist_o = q_pos - ((2 * blk + 2) * B_CMP - 1)
    ok_e, ok_o = dist_e >= 0, dist_o >= 0
    de, do = dist_e.astype(F32), dist_o.astype(F32)
    ce = ce_ref[0].astype(BF16)
    co = co_ref[0].astype(BF16)
    ke, ve, ko, vo = ce[:, :HEAD_DIM], ce[:, HEAD_DIM:], co[:, :HEAD_DIM], co[:, HEAD_DIM:]
    imp_e = jnp.zeros((tq, nbs), F32)
    imp_o = jnp.zeros((tq, nbs), F32)
    for g in range(g_heads):
        slope = slopes_ref[k * g_heads + g]
        lo, hi = g * HEAD_DIM, (g + 1) * HEAD_DIM
        qg = q_ref[0, :, lo:hi]
        se = jnp.where(ok_e, _dot_nt(qg, ke) - slope * de, NEG)
        so = jnp.where(ok_o, _dot_nt(qg, ko) - slope * do, NEG)
        m = jnp.maximum(jnp.max(se, axis=-1, keepdims=True), jnp.max(so, axis=-1, keepdims=True))
        ee = jnp.where(ok_e, jnp.exp(se - m), 0.0)
        eo = jnp.where(ok_o, jnp.exp(so - m), 0.0)
        den = jnp.sum(ee, axis=-1, keepdims=True) + jnp.sum(eo, axis=-1, keepdims=True)
        den = jnp.where(den > 0, den, 1.0)
        pe, po = ee / den, eo / den
        oc_ref[0, :, lo:hi] = _dot(pe.astype(BF16), ve) + _dot(po.astype(BF16), vo)
        imp_e = imp_e + pe
        imp_o = imp_o + po
    score = _sel_score(imp_e + imp_o, blk, q_pos, g_heads)
    sel = _topk_mask(score, n_sel)
    selt_ref[0, 0] = sel.T.astype(BF16)
    any_ref[0, 0, 0] = jnp.broadcast_to(jnp.max(sel, axis=0, keepdims=True), (T_PAD, nbs))


def _nsa_cmp(q, cmp_e, cmp_o, slopes, *, n_heads, n_kv, tq):
    n, s, _ = q.shape
    g_heads = n_heads // n_kv
    gw = g_heads * HEAD_DIM
    nbs = cmp_e.shape[1]
    n_sel = min(B_TOPN, nbs)
    assert s % tq == 0
    return pl.pallas_call(
        functools.partial(_nsa_cmp_body, tq=tq, g_heads=g_heads, n_sel=n_sel),
        grid=(n, n_kv, s // tq),
        in_specs=[_smem(),
                  pl.BlockSpec((1, tq, gw), lambda b, k, i: (b, i, k)),
                  pl.BlockSpec((1, nbs, KV_LANES), lambda b, k, i: (b, 0, k)),
                  pl.BlockSpec((1, nbs, KV_LANES), lambda b, k, i: (b, 0, k))],
        out_specs=[pl.BlockSpec((1, tq, gw), lambda b, k, i: (b, i, k)),
                   pl.BlockSpec((1, 1, nbs, tq), lambda b, k, i: (b, k, 0, i)),
                   pl.BlockSpec((1, 1, 1, T_PAD, nbs), lambda b, k, i: (b, k, i, 0, 0))],
        out_shape=[jax.ShapeDtypeStruct((n, s, n_heads * HEAD_DIM), F32),
                   jax.ShapeDtypeStruct((n, n_kv, nbs, s), BF16),
                   jax.ShapeDtypeStruct((n, n_kv, s // tq, T_PAD, nbs), F32)],
        compiler_params=_params("parallel", "parallel", "parallel"),
        name="nsa_cmp_select",
    )(slopes, q, cmp_e, cmp_o)


def _flash_update(s, valid, vb, m_ref, l_ref, acc_ref, idx, v_transposed=False):
    s = jnp.where(valid, s, NEG)
    m_prev = m_ref[idx]
    m_new = jnp.maximum(m_prev, jnp.max(s, axis=-1, keepdims=True))
    alpha = jnp.exp(m_prev - m_new)
    e = jnp.where(valid, jnp.exp(s - m_new), 0.0)
    l_ref[idx] = alpha * l_ref[idx] + jnp.sum(e, axis=-1, keepdims=True)
    pv = _dot_nt(e.astype(BF16), vb) if v_transposed else _dot(e.astype(BF16), vb)
    acc_ref[idx] = alpha * acc_ref[idx] + pv
    m_ref[idx] = m_new


def _flash_init(m_ref, l_ref, acc_ref):
    m_ref[...] = jnp.full(m_ref.shape, NEG, F32)
    l_ref[...] = jnp.zeros(l_ref.shape, F32)
    acc_ref[...] = jnp.zeros(acc_ref.shape, F32)


def _flash_result(l_ref, acc_ref, idx):
    l = l_ref[idx]
    return acc_ref[idx] / jnp.where(l > 0, l, 1.0)


MASK_BIG = 1e30
M_INIT = -3e38


def _nsa_sel_body(flags_ref, slopes_ref, qt_ref, selt_ref, kaug_ref, kvt_ref, o_ref,
                  qa_ref, bias_ref, m_ref, l_ref, acc_ref, *, t, g_heads, n_kv, n_q):
    b = pl.program_id(0)
    k = pl.program_id(1)
    i = pl.program_id(2)
    width = g_heads * t
    selbias = ((selt_ref[0, 0].astype(F32) - 1.0) * MASK_BIG).astype(BF16)
    r_minus_c = (_iota((1, t), 1) - _iota((t, 1), 0)).astype(F32)
    lane_head = _fdiv(_iota((1, width), 1), t)
    slope = jnp.zeros((1, width), F32)
    for g in range(g_heads):
        cols = slice(g * t, (g + 1) * t)
        qa_ref[0:HEAD_DIM, cols] = qt_ref[0, g]
        qa_ref[HEAD_DIM:KV_LANES, cols] = jnp.zeros((HEAD_DIM, t), BF16)
        qa_ref[KV_LANES:, cols] = selbias
        bias_ref[:, cols] = slopes_ref[k * g_heads + g] * r_minus_c
        slope = jnp.where(lane_head == g, slopes_ref[k * g_heads + g], slope)
    m_ref[...] = jnp.full(m_ref.shape, M_INIT, F32)
    l_ref[...] = jnp.zeros(l_ref.shape, F32)
    acc_ref[...] = jnp.zeros(acc_ref.shape, F32)

    def chunk(c, diagonal):
        k0 = pl.multiple_of(c * t, t)
        ka = kaug_ref[0, pl.ds(k0, t), :]
        kvt = kvt_ref[0, 0, :, pl.ds(k0, t)]
        s = _dot(ka, qa_ref[...]) - bias_ref[...]
        if diagonal:
            causal = jnp.where(_iota((t, 1), 0) > _iota((1, t), 1), NEG, 0.0)
            s = s + jnp.concatenate([causal] * g_heads, axis=1)
        delta = slope * jnp.full((1, width), (i - c) * t, jnp.int32).astype(F32)
        m_prev = m_ref[...]
        m_new = jnp.maximum(m_prev, jnp.max(s, axis=0, keepdims=True) - delta)
        alpha = jnp.exp(m_prev - m_new)
        e = jnp.exp(s - (m_new + delta))
        l_ref[...] = alpha * l_ref[...] + jnp.sum(e, axis=0, keepdims=True)
        acc_ref[...] = alpha * acc_ref[...] + _dot(kvt, e.astype(BF16))
        m_ref[...] = m_new

    def maybe_chunk(c, carry):
        @pl.when(flags_ref[((b * n_kv + k) * n_q + i) * n_q + c] != 0)
        def _():
            chunk(c, False)
        return carry

    lax.fori_loop(0, i, maybe_chunk, 0)
    chunk(i, True)
    l = l_ref[...]
    o_t = acc_ref[HEAD_DIM:KV_LANES, :] / jnp.where(l > 0, l, 1.0)
    for g in range(g_heads):
        o_ref[0, :, g * HEAD_DIM:(g + 1) * HEAD_DIM] = o_t[:, g * t:(g + 1) * t].T


def _nsa_sel(qt, selt, kaug, kvt, flags, slopes, *, n_heads, n_kv, t):
    n, _, _, s = qt.shape
    g_heads = n_heads // n_kv
    gw = g_heads * HEAD_DIM
    nbs = selt.shape[2]
    aug = KV_LANES + nbs
    n_q = s // t
    assert s % t == 0 and t % B_SEL == 0
    return pl.pallas_call(
        functools.partial(_nsa_sel_body, t=t, g_heads=g_heads, n_kv=n_kv, n_q=n_q),
        grid_spec=pltpu.PrefetchScalarGridSpec(
            num_scalar_prefetch=1,
            grid=(n, n_kv, n_q),
            in_specs=[_smem(),
                      pl.BlockSpec((1, g_heads, HEAD_DIM, t), lambda b, k, i, fl: (b, k, 0, i)),
                      pl.BlockSpec((1, 1, nbs, t), lambda b, k, i, fl: (b, k, 0, i)),
                      pl.BlockSpec((1, s, aug), lambda b, k, i, fl: (b, 0, k)),
                      pl.BlockSpec((1, 1, KV_LANES, s), lambda b, k, i, fl: (b, k, 0, 0))],
            out_specs=pl.BlockSpec((1, t, gw), lambda b, k, i, fl: (b, i, k)),
            scratch_shapes=[pltpu.VMEM((aug, g_heads * t), BF16), pltpu.VMEM((t, g_heads * t), F32),
                            pltpu.VMEM((1, g_heads * t), F32), pltpu.VMEM((1, g_heads * t), F32),
                            pltpu.VMEM((KV_LANES, g_heads * t), F32)]),
        out_shape=jax.ShapeDtypeStruct((n, s, n_heads * HEAD_DIM), F32),
        compiler_params=_params("parallel", "parallel", "parallel"),
        name="nsa_selected",
    )(flags, slopes, qt, selt, kaug, kvt)


T_PAD = 8
NEW_PAD = 16
KEY_PAD = 128


def _row_select(ref, base, g_heads, rows):
    gi = _fdiv(_iota((rows, 1), 0), T_PAD)
    out = jnp.zeros((rows, 1), F32)
    for g in range(g_heads):
        out = jnp.where(gi == g, ref[base + g], out)
    return out


def _pad_rows(x, lo, total):
    return jnp.pad(x, ((0, 0), (lo, total - lo - x.shape[1]), (0, 0)))


def _dec_window_body(*refs, lb, t_new, window, dilation, past_len, g_heads, n_kv, has_sink, with_lse):
    it = iter(refs)
    slopes_ref = next(it)
    sinks_ref = next(it) if has_sink else None
    q_ref, new_lo_ref, new_hi_ref, buf_ref = next(it), next(it), next(it), next(it)
    o_ref = next(it)
    lse_ref = next(it) if with_lse else None
    nbuf_ref = next(it)
    keys_ref = next(it)

    rows = g_heads * T_PAD
    span = lb + KEY_PAD
    buf = buf_ref[0]
    keys_ref[0:lb, :] = buf.astype(BF16)
    keys_ref[lb:lb + NEW_PAD, :] = new_lo_ref[0].astype(BF16)
    keys_ref[lb + NEW_PAD:span, :] = jnp.zeros((KEY_PAD - NEW_PAD, keys_ref.shape[1]), BF16)
    shifted = pltpu.roll(buf, lb - t_new, axis=0)
    nbuf_ref[0] = shifted
    is_new = _iota((T_PAD, 1), 0) >= T_PAD - t_new
    nbuf_ref[0, lb - T_PAD:lb, :] = jnp.where(is_new, new_hi_ref[0], shifted[lb - T_PAD:lb])

    t = _fmod(_iota((rows, 1), 0), T_PAD)
    r = _iota((1, span), 1)
    dist = lb + t - r
    valid = (dist >= 0) & (dist <= window) & (past_len - lb + r >= 0)
    if dilation > 1:
        valid = valid & (_fmod(dist, dilation) == 0)
    distf = dist.astype(F32)
    for k in range(n_kv):
        kb = keys_ref[:, k * KV_LANES:k * KV_LANES + HEAD_DIM]
        vb = keys_ref[:, k * KV_LANES + HEAD_DIM:(k + 1) * KV_LANES]
        slope = _row_select(slopes_ref, k * g_heads, g_heads, rows)
        sink = _row_select(sinks_ref, k * g_heads, g_heads, rows) if has_sink else None
        s = _dot_nt(q_ref[0, k], kb) - slope * distf
        p, m, den = _softmax_parts(s, valid, sink)
        o_ref[0, k] = _dot(p.astype(BF16), vb)
        if with_lse:
            lse_ref[0, k] = jnp.broadcast_to(m + jnp.log(den), (rows, HEAD_DIM))


def _dec_window(qd, kv_new, buf, slopes, *, window, dilation, past_len, sinks=None, with_lse=False):
    n, n_kv, rows, _ = qd.shape
    g_heads = rows // T_PAD
    _, t_new, c = kv_new.shape
    lb = buf.shape[1]
    assert lb % NEW_PAD == 0 and t_new <= T_PAD
    new_lo = _pad_rows(kv_new, 0, NEW_PAD)
    new_hi = _pad_rows(kv_new, T_PAD - t_new, T_PAD)
    ins = [slopes]
    in_specs = [_smem()]
    if sinks is not None:
        ins.append(sinks)
        in_specs.append(_smem())
    ins += [qd, new_lo, new_hi, buf]
    o_spec = pl.BlockSpec((1, n_kv, rows, HEAD_DIM), lambda b: (b, 0, 0, 0))
    buf_spec = pl.BlockSpec((1, lb, c), lambda b: (b, 0, 0))
    in_specs += [o_spec, pl.BlockSpec((1, NEW_PAD, c), lambda b: (b, 0, 0)),
                 pl.BlockSpec((1, T_PAD, c), lambda b: (b, 0, 0)), buf_spec]
    o_sds = jax.ShapeDtypeStruct((n, n_kv, rows, HEAD_DIM), F32)
    out_specs = [o_spec] + ([o_spec] if with_lse else []) + [buf_spec]
    out_shape = [o_sds] + ([o_sds] if with_lse else []) + [jax.ShapeDtypeStruct(buf.shape, F32)]
    return pl.pallas_call(
        functools.partial(_dec_window_body, lb=lb, t_new=t_new, window=window, dilation=dilation,
                          past_len=past_len, g_heads=g_heads, n_kv=n_kv, has_sink=sinks is not None,
                          with_lse=with_lse),
        grid=(n,),
        in_specs=in_specs,
        out_specs=out_specs,
        out_shape=out_shape,
        scratch_shapes=[pltpu.VMEM((lb + KEY_PAD, c), BF16)],
        compiler_params=_params("parallel"),
        name="decode_window",
    )(*ins)


def _nsa_dec_cmp_body(slopes_ref, q_ref, cmpt_ref, cnew_ref, pnew_ref, oc_ref, sel_ref, *,
                      past_len, g_heads, n_kv, n_sel):
    rows = g_heads * T_PAD
    width = cmpt_ref.shape[4]
    nbs = width // 2
    t = _fmod(_iota((rows, 1), 0), T_PAD)
    q_pos = past_len + t
    lane = _iota((1, width), 1)
    cmp_blk = jnp.where(lane < nbs, 2 * lane, 2 * (lane - nbs) + 1)
    dist_c = q_pos - ((cmp_blk + 1) * B_CMP - 1)
    lane_n = _iota((1, NEW_PAD), 1)
    dist_ne = q_pos - ((2 * nbs + 1) * B_CMP - 1)
    dist_no = q_pos - ((2 * nbs + 2) * B_CMP - 1)
    cnew = cnew_ref[0].astype(BF16)
    is_k = _fmod(_iota((1, cnew.shape[1]), 1), KV_LANES) < HEAD_DIM
    new_e = jnp.where(is_k, _dot(pnew_ref[0], cnew), _dot(pnew_ref[1], cnew)).astype(BF16)
    new_o = jnp.zeros_like(new_e)
    new_sets = ((new_e, dist_ne, (dist_ne >= 0) & (lane_n == 0)),
                (new_o, dist_no, (dist_no >= 0) & (lane_n == 0)))
    ok_c = dist_c >= 0
    q_pos8 = past_len + _iota((T_PAD, 1), 0)
    blk = _iota((1, nbs), 1)
    lane_p = _iota((1, KEY_PAD), 1)
    for k in range(n_kv):
        c0 = k * KV_LANES
        slope = _row_select(slopes_ref, k * g_heads, g_heads, rows)
        q = q_ref[0, k]
        kt = cmpt_ref[0, k, 0].astype(BF16)
        vt = cmpt_ref[0, k, 1].astype(BF16)
        ss = [jnp.where(ok_c, _dot(q, kt) - slope * dist_c.astype(F32), NEG)]
        oks = [ok_c]
        for keys, dist, ok in new_sets:
            ss.append(jnp.where(ok, _dot_nt(q, keys[:, c0:c0 + HEAD_DIM]) - slope * dist.astype(F32), NEG))
            oks.append(ok)
        m = functools.reduce(jnp.maximum, [jnp.max(s, axis=-1, keepdims=True) for s in ss])
        es = [jnp.where(ok, jnp.exp(s - m), 0.0) for s, ok in zip(ss, oks)]
        den = _sum([jnp.sum(e, axis=-1, keepdims=True) for e in es])
        den = jnp.where(den > 0, den, 1.0)
        ps = [e / den for e in es]
        oc_ref[0, k] = (_dot_nt(ps[0].astype(BF16), vt)
                        + _sum([_dot(p.astype(BF16), keys[:, c0 + HEAD_DIM:c0 + KV_LANES])
                                for p, (keys, _, _) in zip(ps[1:], new_sets)]))
        gsum = lambda p: _sum([p[g * T_PAD:(g + 1) * T_PAD] for g in range(g_heads)])
        imp_c = gsum(ps[0])
        imp = imp_c[:, :nbs] + imp_c[:, nbs:]
        imp_new = gsum(ps[1]) + gsum(ps[2])
        score = _sel_score(imp, blk, q_pos8, g_heads)
        score_new = _sel_score(imp_new[:, 0:1], nbs + lane_p, q_pos8, g_heads)
        score_new = jnp.where(lane_p == 0, score_new, -3.0)
        sel_ref[0, k] = _topk_mask(jnp.concatenate([score, score_new], axis=1), n_sel)


def _nsa_dec_cmp(qd, cmpt, c_new, pnew, slopes, *, past_len):
    n, n_kv, rows, _ = qd.shape
    g_heads = rows // T_PAD
    width = cmpt.shape[4]
    nbs = width // 2
    c = c_new.shape[2]
    n_sel = min(B_TOPN, nbs + 1)
    cnew = _pad_rows(c_new, 0, NEW_PAD)
    q_spec = pl.BlockSpec((1, n_kv, rows, HEAD_DIM), lambda b: (b, 0, 0, 0))
    return pl.pallas_call(
        functools.partial(_nsa_dec_cmp_body, past_len=past_len, g_heads=g_heads, n_kv=n_kv, n_sel=n_sel),
        grid=(n,),
        in_specs=[_smem(), q_spec,
                  pl.BlockSpec((1, n_kv, 2, HEAD_DIM, width), lambda b: (b, 0, 0, 0, 0)),
                  pl.BlockSpec((1, NEW_PAD, c), lambda b: (b, 0, 0)),
                  pl.BlockSpec(pnew.shape, lambda b: (0, 0, 0))],
        out_specs=[q_spec, pl.BlockSpec((1, n_kv, T_PAD, nbs + KEY_PAD), lambda b: (b, 0, 0, 0))],
        out_shape=[jax.ShapeDtypeStruct((n, n_kv, rows, HEAD_DIM), F32),
                   jax.ShapeDtypeStruct((n, n_kv, T_PAD, nbs + KEY_PAD), F32)],
        compiler_params=_params("parallel"),
        name="nsa_decode_cmp_select",
    )(slopes, qd, cmpt, cnew, pnew)


def _nsa_dec_sel_body(*refs, past_len, g_heads, n_kv, per_step, page):
    logical_ref, count_ref = refs[1], refs[2]
    slopes_ref, q_ref, sel_ref, snew_ref = refs[3:7]
    page_refs = refs[7:7 + per_step]
    o_ref, m_ref, l_ref, acc_ref = refs[7 + per_step:]
    rows = g_heads * T_PAD
    b = pl.program_id(0)
    j = pl.program_id(1)
    nsel = sel_ref.shape[3]
    t = _fmod(_iota((rows, 1), 0), T_PAD)
    q_pos = past_len + t

    @pl.when(j == 0)
    def _():
        _flash_init(m_ref, l_ref, acc_ref)

    @pl.when(j * per_step < count_ref[b])
    def _():
        lane = _iota((1, page), 1)
        k_pos = jnp.concatenate([logical_ref[b, j * per_step + u] * page + lane for u in range(per_step)], axis=1)
        live = jnp.concatenate([jnp.full((1, page), j * per_step + u, jnp.int32) for u in range(per_step)],
                               axis=1) < count_ref[b]
        expand = (_iota((nsel, 1), 0) == _fdiv(k_pos, B_SEL)).astype(BF16)
        distf = (q_pos - k_pos).astype(F32)
        for k in range(n_kv):
            kt = jnp.concatenate([r[0, k, 0].astype(BF16) for r in page_refs], axis=1)
            vt = jnp.concatenate([r[0, k, 1].astype(BF16) for r in page_refs], axis=1)
            sel = jnp.concatenate([sel_ref[0, k]] * g_heads, axis=0).astype(BF16)
            valid = (_dot(sel, expand) > 0.5) & live & (k_pos <= q_pos)
            slope = _row_select(slopes_ref, k * g_heads, g_heads, rows)
            s = _dot(q_ref[0, k], kt) - slope * distf
            _flash_update(s, valid, vt, m_ref, l_ref, acc_ref, k, v_transposed=True)

    @pl.when(j == pl.num_programs(1) - 1)
    def _():
        n_past = past_len // B_SEL
        snew = snew_ref[0].astype(BF16)
        dist = t - _iota((1, NEW_PAD), 1)
        for k in range(n_kv):
            c0 = k * KV_LANES
            sel = jnp.concatenate([sel_ref[0, k]] * g_heads, axis=0)
            valid = (sel[:, n_past:n_past + 1] > 0.5) & (dist >= 0)
            slope = _row_select(slopes_ref, k * g_heads, g_heads, rows)
            s = _dot_nt(q_ref[0, k], snew[:, c0:c0 + HEAD_DIM]) - slope * dist.astype(F32)
            _flash_update(s, valid, snew[:, c0 + HEAD_DIM:c0 + KV_LANES], m_ref, l_ref, acc_ref, k)
            o_ref[0, k] = _flash_result(l_ref, acc_ref, k)


def _needed_pages(sel, page_table, t_new, page):
    n, n_pg = page_table.shape
    per_page = page // B_SEL
    picked = sel[:, :, :t_new, :n_pg * per_page].max(axis=(1, 2)) > 0
    need = picked.reshape(n, n_pg, per_page).any(axis=-1)
    count = need.sum(axis=1).astype(jnp.int32)
    order = jnp.argsort(jnp.logical_not(need), axis=1, stable=True).astype(jnp.int32)
    slot = jnp.minimum(jnp.arange(n_pg, dtype=jnp.int32)[None, :], jnp.maximum(count[:, None] - 1, 0))
    logical = jnp.take_along_axis(order, slot, axis=1)
    physical = jnp.take_along_axis(page_table, logical, axis=1)
    return physical, logical, count


def _nsa_dec_sel(qd, sel, s_new, cache_t, page_table, slopes, *, past_len, per_step=4):
    n, n_kv, rows, _ = qd.shape
    g_heads = rows // T_PAD
    page = cache_t.shape[4]
    t_new, c = s_new.shape[1:]
    n_pg = page_table.shape[1]
    per_step = min(per_step, n_pg)
    assert n_pg % per_step == 0 and past_len == n_pg * page
    snew = _pad_rows(s_new, 0, NEW_PAD)
    physical, logical, count = _needed_pages(sel, page_table, t_new, page)
    q_spec = pl.BlockSpec((1, n_kv, rows, HEAD_DIM), lambda b, j, *_: (b, 0, 0, 0))
    page_spec = lambda u: pl.BlockSpec((1, n_kv, 2, HEAD_DIM, page),
                                       lambda b, j, phys, *_: (phys[b, j * per_step + u], 0, 0, 0, 0))
    return pl.pallas_call(
        functools.partial(_nsa_dec_sel_body, past_len=past_len, g_heads=g_heads, n_kv=n_kv,
                          per_step=per_step, page=page),
        grid_spec=pltpu.PrefetchScalarGridSpec(
            num_scalar_prefetch=3,
            grid=(n, n_pg // per_step),
            in_specs=[_smem(), q_spec,
                      pl.BlockSpec((1, n_kv, T_PAD, sel.shape[3]), lambda b, j, *_: (b, 0, 0, 0)),
                      pl.BlockSpec((1, NEW_PAD, c), lambda b, j, *_: (b, 0, 0))]
                     + [page_spec(u) for u in range(per_step)],
            out_specs=q_spec,
            scratch_shapes=[pltpu.VMEM((n_kv, rows, 1), F32), pltpu.VMEM((n_kv, rows, 1), F32),
                            pltpu.VMEM((n_kv, rows, HEAD_DIM), F32)]),
        out_shape=jax.ShapeDtypeStruct((n, n_kv, rows, HEAD_DIM), F32),
        compiler_params=_params("parallel", "arbitrary"),
        name="nsa_decode_selected",
    )(physical, logical, count, slopes, qd, sel, snew, *([cache_t] * per_step))


def _nsa_merge_body(oc_ref, os_ref, ow_ref, bgz_ref, o_ref, *, n_heads):
    bg = jax.nn.sigmoid(bgz_ref[...])
    for h in range(n_heads):
        lo, hi = h * HEAD_DIM, (h + 1) * HEAD_DIM
        o_ref[:, lo:hi] = (bg[:, h:h + 1] * oc_ref[:, lo:hi]
                           + bg[:, n_heads + h:n_heads + h + 1] * os_ref[:, lo:hi]
                           + bg[:, 2 * n_heads + h:2 * n_heads + h + 1] * ow_ref[:, lo:hi])


def _nsa_merge(o_c, o_s, o_w, bgz, n_heads):
    m_rows, d = o_c.shape
    full = lambda a: pl.BlockSpec(a.shape, lambda i: (0, 0))
    return pl.pallas_call(
        functools.partial(_nsa_merge_body, n_heads=n_heads),
        grid=(1,),
        in_specs=[full(o_c), full(o_s), full(o_w), full(bgz)],
        out_specs=full(o_c),
        out_shape=jax.ShapeDtypeStruct((m_rows, d), F32),
        compiler_params=_params("arbitrary"),
        name="nsa_decode_merge",
    )(o_c, o_s, o_w, bgz)


def _to_decode_rows(q, n_kv):
    n, t, hd = q.shape
    g_heads = hd // HEAD_DIM // n_kv
    q = q.reshape(n, t, n_kv, g_heads, HEAD_DIM).transpose(0, 2, 3, 1, 4)
    q = jnp.pad(q, ((0, 0), (0, 0), (0, 0), (0, T_PAD - t), (0, 0)))
    return q.reshape(n, n_kv, g_heads * T_PAD, HEAD_DIM)


def _from_decode_rows(o, t):
    n, n_kv, rows, _ = o.shape
    g_heads = rows // T_PAD
    o = o.reshape(n, n_kv, g_heads, T_PAD, HEAD_DIM)[:, :, :, :t]
    return o.transpose(0, 3, 1, 2, 4).reshape(n * t, n_kv * g_heads * HEAD_DIM)


def _kv_state(rows, n_kv):
    return rows.reshape(*rows.shape[:-1], n_kv, 2, HEAD_DIM)


def _mixer_a(xp, xs, buf, g_pre, g_post, w_in, w_out, sinks, past_len, tm):
    n, s, d = xp.shape
    m, t, _ = xs.shape
    n_heads = d // HEAD_DIM
    n_kv = n_heads // 8
    nq, nkv = n_heads * HEAD_DIM, n_kv * KV_LANES
    slopes = _alibi_slopes(n_heads)
    w_in_b, w_out_b = w_in.astype(BF16), w_out.astype(BF16)
    splits = [(0, nq, "q"), (nq, nkv, "f32"), (nq, nkv, "bf16"), (nq + nkv, nq, "f32")]
    q, kv, kvb, gate = _proj(xp.reshape(n * s, d), g_pre, w_in_b, splits, tm)
    o = _banded(q.reshape(n, s, nq), kvb.reshape(n, s, nkv), slopes, n_heads=n_heads, n_kv=n_kv,
                window=A_WINDOW, stride=1, sinks=sinks)
    yp = _outproj([o.reshape(n * s, nq)], [], gate, xp.reshape(n * s, d), g_post, w_out_b, tm)
    keep = min(A_WINDOW, s)
    state_p = _kv_state(kv.reshape(n, s, nkv)[:, s - keep:], n_kv)
    qs, kvs, _, gs = _proj(xs.reshape(m * t, d), g_pre, w_in_b, splits, tm)
    od, nbuf = _dec_window(_to_decode_rows(qs.reshape(m, t, nq), n_kv), kvs.reshape(m, t, nkv),
                           buf.reshape(m, buf.shape[1], nkv), slopes, window=A_WINDOW, dilation=1,
                           past_len=past_len, sinks=sinks)
    ys = _outproj([_from_decode_rows(od, t)], [], gs, xs.reshape(m * t, d), g_post, w_out_b, tm)
    return yp.reshape(n, s, d), ys.reshape(m, t, d), [state_p], [_kv_state(nbuf, n_kv)]


def _mixer_b(xp, xs, cache_cmp, cache_sel, buf_win, page_table, g_pre, g_post, w_in, w_out, cmp_pool,
             past_len, tm):
    n, s, d = xp.shape
    m, t, _ = xs.shape
    n_heads = d // HEAD_DIM
    n_kv = n_heads // 4
    nq, nkv = n_heads * HEAD_DIM, n_kv * KV_LANES
    n_bg = 3 * n_heads
    slopes = _alibi_slopes(n_heads)
    w_in_b, w_out_b = w_in.astype(BF16), w_out.astype(BF16)
    pmats = _pool_mats(cmp_pool)
    c0, s0, w0, g0, b0 = nq, nq + nkv, nq + 2 * nkv, nq + 3 * nkv, 2 * nq + 3 * nkv
    base = [(0, nq, "q"), (c0, nkv, "f32"), (s0, nkv, "f32"), (w0, nkv, "f32"), (g0, nq, "f32"),
            (b0, n_bg, "f32")]
    splits = base + [(s0, nkv, "bf16"), (s0, nkv, "kaug"), (w0, nkv, "bf16")]
    q, c_p, s_p, w_p, gate, bgz, s_pb, kaug, w_pb = _proj(xp.reshape(n * s, d), g_pre, w_in_b, splits, tm,
                                                          seq_len=s)
    q = q.reshape(n, s, nq)
    cmp_e, cmp_o = _pool_prompt(c_p.reshape(n, s, nkv), pmats)
    t_sel = min(256, s)
    o_c, selt, sel_any = _nsa_cmp(q, cmp_e, cmp_o, slopes, n_heads=n_heads, n_kv=n_kv, tq=t_sel)
    per_tile = t_sel // B_SEL
    flags = (sel_any[:, :, :, 0, :].reshape(n, n_kv, s // t_sel, s // t_sel, per_tile).max(axis=-1) > 0)
    flags = flags.astype(jnp.int32).reshape(-1)
    qt = q.reshape(n, s, n_heads, HEAD_DIM).transpose(0, 2, 3, 1)
    kvt = s_pb.reshape(n, s, n_kv, KV_LANES).transpose(0, 2, 3, 1)
    o_s = _nsa_sel(qt, selt, kaug.reshape(n, s, -1), kvt, flags, slopes, n_heads=n_heads, n_kv=n_kv, t=t_sel)
    o = _banded(q, w_pb.reshape(n, s, nkv), slopes, n_heads=n_heads, n_kv=n_kv, window=B_WINDOW, stride=1,
                merge=(o_c, o_s, bgz.reshape(n, s, n_bg)))
    yp = _outproj([o.reshape(n * s, nq)], [], gate, xp.reshape(n * s, d), g_post, w_out_b, tm)
    keep = min(B_WINDOW, s)
    states_p = [_kv_state(c_p.reshape(n, s, nkv), n_kv), _kv_state(s_p.reshape(n, s, nkv), n_kv),
                _kv_state(w_p.reshape(n, s, nkv)[:, s - keep:], n_kv)]
    qs, c_s, s_s, w_s, gs, bgzs = _proj(xs.reshape(m * t, d), g_pre, w_in_b, base, tm)
    qd = _to_decode_rows(qs.reshape(m, t, nq), n_kv)
    c_s, s_s, w_s = (a.reshape(m, t, nkv) for a in (c_s, s_s, w_s))
    cmpt = _pool_paged(_pages_view(cache_cmp), page_table, cmp_pool)
    n_w = min(NEW_PAD, B_CMP)
    pnew = jnp.zeros((2, NEW_PAD, NEW_PAD), F32).at[:, 0, :n_w].set(cmp_pool.astype(F32)[:, :n_w]).astype(BF16)
    oc_d, sel_d = _nsa_dec_cmp(qd, cmpt, c_s, pnew, slopes, past_len=past_len)
    os_d = _nsa_dec_sel(qd, sel_d, s_s, _pages_view(cache_sel), page_table, slopes, past_len=past_len)
    ow_d, nwin = _dec_window(qd, w_s, buf_win.reshape(m, buf_win.shape[1], nkv), slopes, window=B_WINDOW,
                             dilation=1, past_len=past_len)
    osm = _nsa_merge(_from_decode_rows(oc_d, t), _from_decode_rows(os_d, t), _from_decode_rows(ow_d, t),
                     bgzs, n_heads)
    ys = _outproj([osm], [], gs, xs.reshape(m * t, d), g_post, w_out_b, tm)
    states_s = [_kv_state(c_s, n_kv), _kv_state(s_s, n_kv), _kv_state(nwin, n_kv)]
    return yp.reshape(n, s, d), ys.reshape(m, t, d), states_p, states_s


def _mixer_c(xp, xs, bufs, g_pre, g_post, w_in, w_out, past_len, tm):
    n, s, d = xp.shape
    m, t, _ = xs.shape
    n_heads = d // HEAD_DIM
    n_kv = n_heads // 8
    ng = len(C_GROUPS)
    nq1, nkv1 = n_heads * HEAD_DIM, n_kv * KV_LANES
    nq, nkv = ng * nq1, ng * nkv1
    slopes = _alibi_slopes(n_heads)
    w_in_b, w_out_b = w_in.astype(BF16), w_out.astype(BF16)
    splits = [(0, nq, "q"), (nq, nkv, "f32"), (nq, nkv, "bf16"), (nq + nkv, nq1, "f32")]
    q, kv, kvb, gate = _proj(xp.reshape(n * s, d), g_pre, w_in_b, splits, tm)
    q, kv, kvb = q.reshape(n, s, nq), kv.reshape(n, s, nkv), kvb.reshape(n, s, nkv)
    outs, lses, states_p = [], [], []
    for gi, (w, dil) in enumerate(C_GROUPS):
        o, lse = _banded(q, kvb, slopes, n_heads=n_heads, n_kv=n_kv, window=w, stride=dil, q_group=gi,
                         with_lse=True)
        outs.append(o.reshape(n * s, nq1))
        lses.append(lse.reshape(n * s, nq1))
        keep = min(w, s)
        states_p.append(_kv_state(kv[:, s - keep:, gi * nkv1:(gi + 1) * nkv1], n_kv))
    yp = _outproj(outs, lses, gate, xp.reshape(n * s, d), g_post, w_out_b, tm)
    qs, kvs, _, gs = _proj(xs.reshape(m * t, d), g_pre, w_in_b, splits, tm)
    qs, kvs = qs.reshape(m, t, nq), kvs.reshape(m, t, nkv)
    outs, lses, states_s = [], [], []
    for gi, (w, dil) in enumerate(C_GROUPS):
        buf = bufs[gi]
        assert buf.shape[1] == w
        od, ld, nbuf = _dec_window(_to_decode_rows(qs[:, :, gi * nq1:(gi + 1) * nq1], n_kv),
                                   kvs[:, :, gi * nkv1:(gi + 1) * nkv1], buf.reshape(m, w, nkv1), slopes,
                                   window=w, dilation=dil, past_len=past_len, with_lse=True)
        outs.append(_from_decode_rows(od, t))
        lses.append(_from_decode_rows(ld, t))
        states_s.append(_kv_state(nbuf, n_kv))
    ys = _outproj(outs, lses, gs, xs.reshape(m * t, d), g_post, w_out_b, tm)
    return yp.reshape(n, s, d), ys.reshape(m, t, d), states_p, states_s


def kernel(x_prompt, x_sample, state_a0_kv, cache_b1_cmp_kv, cache_b1_sel_kv, state_b1_win_kv,
           state_c2_g0_kv, state_c2_g1_kv, state_c2_g2_kv, state_a3_kv, page_table, norm_g,
           w_in_0, w_out_0, sinks_0, w_in_1, w_out_1, cmp_pool_1, w_in_2, w_out_2,
           w_in_3, w_out_3, sinks_3):
    past_len = page_table.shape[1] * cache_b1_cmp_kv.shape[1]
    tm = 256
    xp, xs = x_prompt, x_sample
    xp, xs, (a0_p,), (a0_s,) = _mixer_a(xp, xs, state_a0_kv, norm_g[0, 0], norm_g[0, 1], w_in_0, w_out_0,
                                        sinks_0, past_len, tm)
    xp, xs, (b1_cmp_p, b1_sel_p, b1_win_p), (b1_cmp_s, b1_sel_s, b1_win_s) = _mixer_b(
        xp, xs, cache_b1_cmp_kv, cache_b1_sel_kv, state_b1_win_kv, page_table, norm_g[1, 0], norm_g[1, 1],
        w_in_1, w_out_1, cmp_pool_1, past_len, tm)
    xp, xs, (c2_g0_p, c2_g1_p, c2_g2_p), (c2_g0_s, c2_g1_s, c2_g2_s) = _mixer_c(
        xp, xs, (state_c2_g0_kv, state_c2_g1_kv, state_c2_g2_kv), norm_g[2, 0], norm_g[2, 1], w_in_2, w_out_2,
        past_len, tm)
    xp, xs, (a3_p,), (a3_s,) = _mixer_a(xp, xs, state_a3_kv, norm_g[3, 0], norm_g[3, 1], w_in_3, w_out_3,
                                        sinks_3, past_len, tm)
    return (xp, xs, a0_p, a0_s, b1_cmp_p, b1_cmp_s, b1_sel_p, b1_sel_s, b1_win_p, b1_win_s,
            c2_g0_p, c2_g0_s, c2_g1_p, c2_g1_s, c2_g2_p, c2_g2_s, a3_p, a3_s)
```

```python
import functools

import jax
import jax.numpy as jnp
from jax import lax
from jax.experimental import pallas as pl
from jax.experimental.pallas import tpu as pltpu

F32 = jnp.float32
BF16 = jnp.bfloat16

HEAD_DIM = 64
KV_LANES = 2 * HEAD_DIM
RMS_EPS = 1e-6
NEG = -1e30
SCALE = HEAD_DIM ** -0.5

A_WINDOW = 128
B_CMP = 32
B_SEL = 64
B_TOPN = 16
B_WINDOW = 512
C_GROUPS = ((128, 1), (512, 4), (2048, 16))

VMEM_LIMIT_BYTES = 48 * 1024 * 1024


def _params(*sem):
    return pltpu.CompilerParams(dimension_semantics=sem, vmem_limit_bytes=VMEM_LIMIT_BYTES)


def _smem():
    return pl.BlockSpec(memory_space=pltpu.SMEM)


def _dot_nt(a, b):
    return lax.dot_general(a, b, (((1,), (1,)), ((), ())), preferred_element_type=F32)


def _dot(a, b):
    return jnp.dot(a, b, preferred_element_type=F32)


def _iota(shape, dim):
    return lax.broadcasted_iota(jnp.int32, shape, dim)


def _log2(d):
    assert d > 0 and d & (d - 1) == 0
    return d.bit_length() - 1


def _fdiv(x, d):
    return lax.shift_right_arithmetic(x, _log2(d))


def _fmod(x, d):
    return x & (d - 1)


def _sum(xs):
    return functools.reduce(lambda a, b: a + b, xs)


def _alibi_slopes(n_heads):
    h = jnp.arange(1, n_heads + 1, dtype=F32)
    return jnp.exp2(-8.0 * h / n_heads)


def _softmax_parts(s, valid, sink=None):
    s = jnp.where(valid, s, NEG)
    m = jnp.max(s, axis=-1, keepdims=True)
    if sink is not None:
        m = jnp.maximum(m, sink)
    e = jnp.where(valid, jnp.exp(s - m), 0.0)
    den = jnp.sum(e, axis=-1, keepdims=True)
    if sink is not None:
        den = den + jnp.exp(sink - m)
    den = jnp.where(den > 0, den, 1.0)
    return e / den, m, den


def _proj_body(x_ref, g_ref, w_ref, *out_refs, splits, chunk, tm, seq_len):
    x = x_ref[...]
    ms = jnp.mean(x * x, axis=-1, keepdims=True)
    h = ((x * lax.rsqrt(ms + RMS_EPS)) * g_ref[...]).astype(BF16)
    for o_ref, (start, width, mode) in zip(out_refs, splits):
        if mode == "kaug":
            nbs = seq_len // B_SEL
            aug = KV_LANES + nbs
            pos = _fmod(pl.program_id(0) * tm + _iota((tm, 1), 0), seq_len)
            onehot = (_iota((1, nbs), 1) == _fdiv(pos, B_SEL)).astype(o_ref.dtype)
            is_k = _iota((1, KV_LANES), 1) < HEAD_DIM
            for j in range(width // KV_LANES):
                z = _dot(h, w_ref[:, start + j * KV_LANES:start + (j + 1) * KV_LANES])
                o_ref[:, j * aug:j * aug + KV_LANES] = jnp.where(is_k, z, 0.0).astype(o_ref.dtype)
                o_ref[:, j * aug + KV_LANES:(j + 1) * aug] = onehot
            continue
        for c0 in range(0, width, chunk):
            cw = min(chunk, width - c0)
            z = _dot(h, w_ref[:, start + c0:start + c0 + cw])
            if mode == "f32t":
                o_ref[0, c0:c0 + cw, :] = z.T
                continue
            if mode == "q":
                z = z * SCALE
            o_ref[:, c0:c0 + cw] = z.astype(o_ref.dtype)


_PROJ_DTYPES = {"f32": F32, "bf16": BF16, "q": BF16, "kaug": BF16, "f32t": F32}


def _proj(x2d, g, w_bf16, splits, tm, seq_len=None):
    m_rows, d = x2d.shape
    n = w_bf16.shape[1]
    tm = min(tm, m_rows)
    assert m_rows % tm == 0

    out_specs, out_shape = [], []
    for _, width, mode in splits:
        if mode == "f32t":
            per_seq = seq_len // tm
            assert seq_len % tm == 0
            out_specs.append(pl.BlockSpec((1, width, tm), lambda i: (i // per_seq, 0, i % per_seq)))
            out_shape.append(jax.ShapeDtypeStruct((m_rows // seq_len, width, seq_len), F32))
            continue
        if mode == "kaug":
            width = width // KV_LANES * (KV_LANES + seq_len // B_SEL)
        out_specs.append(pl.BlockSpec((tm, width), lambda i: (i, 0)))
        out_shape.append(jax.ShapeDtypeStruct((m_rows, width), _PROJ_DTYPES[mode]))
    return pl.pallas_call(
        functools.partial(_proj_body, splits=tuple(splits), chunk=512, tm=tm, seq_len=seq_len),
        grid=(m_rows // tm,),
        in_specs=[pl.BlockSpec((tm, d), lambda i: (i, 0)),
                  pl.BlockSpec((1, d), lambda i: (0, 0)),
                  pl.BlockSpec((d, n), lambda i: (0, 0))],
        out_specs=out_specs,
        out_shape=out_shape,
        compiler_params=_params("parallel"),
        name="proj",
    )(x2d, g.reshape(1, d), w_bf16)


def _outproj_body(*refs, n_branch):
    if n_branch == 1:
        o_ref, gate_ref, x_ref, g_ref, w_ref, y_ref = refs
        o = o_ref[...]
    else:
        o_refs = refs[:n_branch]
        l_refs = refs[n_branch:2 * n_branch]
        gate_ref, x_ref, g_ref, w_ref, y_ref = refs[2 * n_branch:]
        ls = [r[...] for r in l_refs]
        m = functools.reduce(jnp.maximum, ls)
        es = [jnp.exp(l - m) for l in ls]
        den = _sum(es)
        o = _sum([(e / den) * r[...] for e, r in zip(es, o_refs)])
    gate = gate_ref[...]
    a = (o * (gate * jax.nn.sigmoid(gate))).astype(BF16)
    y = _dot(a, w_ref[...])
    ms = jnp.mean(y * y, axis=-1, keepdims=True)
    y_ref[...] = x_ref[...] + (y * lax.rsqrt(ms + RMS_EPS)) * g_ref[...]


def _outproj(os_, lses, gate, x2d, g, w_bf16, tm):
    m_rows, d = x2d.shape
    k = w_bf16.shape[0]
    tm = min(tm, m_rows)
    assert m_rows % tm == 0
    row = lambda width: pl.BlockSpec((tm, width), lambda i: (i, 0))
    n_branch = len(os_)
    ins = list(os_) + list(lses) + [gate, x2d, g.reshape(1, d), w_bf16]
    in_specs = ([row(k)] * (n_branch + len(lses)) + [row(k), row(d),
                pl.BlockSpec((1, d), lambda i: (0, 0)), pl.BlockSpec((k, d), lambda i: (0, 0))])
    return pl.pallas_call(
        functools.partial(_outproj_body, n_branch=n_branch),
        grid=(m_rows // tm,),
        in_specs=in_specs,
        out_specs=row(d),
        out_shape=jax.ShapeDtypeStruct((m_rows, d), F32),
        compiler_params=_params("parallel"),
        name="outproj",
    )(*ins)


SUB_Q = 128


def _head_rows(ref, base, g_heads, rows_per_head):
    rows = g_heads * rows_per_head
    gi = _fdiv(_iota((rows, 1), 0), rows_per_head)
    out = jnp.zeros((rows, 1), F32)
    for g in range(g_heads):
        out = jnp.where(gi == g, ref[base + g], out)
    return out


def _banded_body(*refs, tq, wsub, stride, window, g_heads, has_sink, with_lse, nsa_merge):
    it = iter(refs)
    slopes_ref = next(it)
    sinks_ref = next(it) if has_sink else None
    q_ref = next(it)
    kv_ref = next(it)
    if nsa_merge:
        oc_ref, os_ref, bgz_ref = next(it), next(it), next(it)
    o_ref = next(it)
    lse_ref = next(it) if with_lse else None
    bias_ref = next(it)

    k = pl.program_id(2)
    i = pl.program_id(3)
    span = wsub + SUB_Q
    rows = g_heads * SUB_Q
    n_var = wsub // SUB_Q + 1

    @pl.when(i == 0)
    def _():
        r = _fmod(_iota((rows, 1), 0), SUB_Q)
        c = _iota((1, span), 1)
        slope = _head_rows(slopes_ref, k * g_heads, g_heads, SUB_Q)
        for v in range(n_var):
            dist = (wsub - SUB_Q * v + r - c) * stride
            ok = (dist >= 0) & (dist <= window)
            bias_ref[v] = jnp.where(ok, -(slope * dist.astype(F32)), NEG)

    sink = _head_rows(sinks_ref, k * g_heads, g_heads, SUB_Q) if has_sink else None
    for sb in range(tq // SUB_Q):
        r0 = sb * SUB_Q
        q0 = i * tq + r0
        kst = pl.multiple_of(jnp.maximum(q0 - wsub, 0), SUB_Q)
        var = _fdiv(jnp.maximum(wsub - q0, 0), SUB_Q)
        kv = kv_ref[0, pl.ds(kst, span), :]
        qs = jnp.concatenate([q_ref[0, r0:r0 + SUB_Q, g * HEAD_DIM:(g + 1) * HEAD_DIM]
                              for g in range(g_heads)], axis=0)
        s = _dot_nt(qs, kv[:, :HEAD_DIM]) + bias_ref[var]
        m = jnp.max(s, axis=-1, keepdims=True)
        if has_sink:
            m = jnp.maximum(m, sink)
        e = jnp.exp(s - m)
        den = jnp.sum(e, axis=-1, keepdims=True)
        if has_sink:
            den = den + jnp.exp(sink - m)
        o = _dot(e.astype(BF16), kv[:, HEAD_DIM:]) / den
        if with_lse:
            lse = m + jnp.log(den)
        if nsa_merge:
            bg = jax.nn.sigmoid(bgz_ref[0, 0, r0:r0 + SUB_Q, :])
        for g in range(g_heads):
            lo, hi = g * HEAD_DIM, (g + 1) * HEAD_DIM
            og = o[g * SUB_Q:(g + 1) * SUB_Q]
            if nsa_merge:
                col = lambda j: bg[:, j * g_heads + g:j * g_heads + g + 1]
                og = (col(0) * oc_ref[0, r0:r0 + SUB_Q, lo:hi] + col(1) * os_ref[0, r0:r0 + SUB_Q, lo:hi]
                      + col(2) * og)
            o_ref[0, r0:r0 + SUB_Q, lo:hi] = og
            if with_lse:
                lse_ref[0, r0:r0 + SUB_Q, lo:hi] = jnp.broadcast_to(lse[g * SUB_Q:(g + 1) * SUB_Q],
                                                                    (SUB_Q, HEAD_DIM))


def _banded(q, kv, slopes, *, n_heads, n_kv, window, stride, q_group=0, sinks=None,
            with_lse=False, merge=None, tq=512):
    n, s, qc = q.shape
    kc = kv.shape[2]
    g_heads = n_heads // n_kv
    gw = g_heads * HEAD_DIM
    sp = s // stride
    wsub = window // stride
    tq = min(tq, sp)
    assert sp % tq == 0 and wsub % SUB_Q == 0 and tq % SUB_Q == 0 and wsub + SUB_Q <= sp
    qv = q.reshape(n, sp, stride * qc)
    kvv = kv.reshape(n, sp, stride * kc)
    q_blocks, kv_blocks, o_blocks = qc // gw, kc // KV_LANES, n_kv
    q_map = lambda b, r, k, i: (b, i, r * q_blocks + q_group * n_kv + k)
    kv_map = lambda b, r, k, i: (b, 0, r * kv_blocks + q_group * n_kv + k)
    o_map = lambda b, r, k, i: (b, i, r * o_blocks + k)
    ins = [slopes]
    in_specs = [_smem()]
    if sinks is not None:
        ins.append(sinks)
        in_specs.append(_smem())
    ins += [qv, kvv]
    in_specs += [pl.BlockSpec((1, tq, gw), q_map), pl.BlockSpec((1, sp, KV_LANES), kv_map)]
    if merge is not None:
        o_c, o_s, bgz = merge
        bgz = bgz.reshape(n, s, 3, n_kv, g_heads).transpose(0, 3, 1, 2, 4).reshape(n, n_kv, s, 3 * g_heads)
        ins += [o_c, o_s, bgz]
        in_specs += [pl.BlockSpec((1, tq, gw), o_map), pl.BlockSpec((1, tq, gw), o_map),
                     pl.BlockSpec((1, 1, tq, 3 * g_heads), lambda b, r, k, i: (b, k, i, 0))]
    o_sds = jax.ShapeDtypeStruct((n, sp, stride * n_heads * HEAD_DIM), F32)
    o_spec = pl.BlockSpec((1, tq, gw), o_map)
    out = pl.pallas_call(
        functools.partial(_banded_body, tq=tq, wsub=wsub, stride=stride, window=window, g_heads=g_heads,
                          has_sink=sinks is not None, with_lse=with_lse, nsa_merge=merge is not None),
        grid=(n, stride, n_kv, sp // tq),
        in_specs=in_specs,
        out_specs=[o_spec, o_spec] if with_lse else o_spec,
        out_shape=[o_sds, o_sds] if with_lse else o_sds,
        scratch_shapes=[pltpu.VMEM((wsub // SUB_Q + 1, g_heads * SUB_Q, wsub + SUB_Q), F32)],
        compiler_params=_params("parallel", "parallel", "parallel", "arbitrary"),
        name="banded_attn",
    )(*ins)
    if with_lse:
        return tuple(a.reshape(n, s, n_heads * HEAD_DIM) for a in out)
    return out.reshape(n, s, n_heads * HEAD_DIM)


POOL_ROWS = 1024


def _pool_body(*refs, n_in, rows_in, paged):
    if paged:
        refs = refs[1:]
    p_ref = refs[0]
    in_refs = refs[1:1 + n_in]
    ce_ref, co_ref = refs[1 + n_in:]
    is_k = _fmod(_iota((1, ce_ref.shape[2]), 1), KV_LANES) < HEAD_DIM
    for parity, out_ref in ((0, ce_ref), (1, co_ref)):
        acc_k = None
        acc_v = None
        for j, r in enumerate(in_refs):
            rows = r[0].astype(BF16)
            pk = p_ref[parity, :, j * rows_in:(j + 1) * rows_in]
            pv = p_ref[2 + parity, :, j * rows_in:(j + 1) * rows_in]
            dk, dv = _dot(pk, rows), _dot(pv, rows)
            acc_k = dk if acc_k is None else acc_k + dk
            acc_v = dv if acc_v is None else acc_v + dv
        out_ref[0] = jnp.where(is_k, acc_k, acc_v)


def _pool_mats(pool_w):
    nb = POOL_ROWS // B_CMP
    eye = jnp.eye(nb, dtype=F32)
    mats = []
    for x in range(2):
        full = jnp.kron(eye, pool_w[x].astype(F32)[None, :])
        mats += [full[0::2], full[1::2]]
    return jnp.stack(mats).astype(BF16)


def _pool_prompt(rows, pmats):
    n, l, c = rows.shape
    assert l % POOL_ROWS == 0
    half = POOL_ROWS // B_CMP // 2
    sds = jax.ShapeDtypeStruct((n, l // B_CMP // 2, c), F32)
    out_spec = pl.BlockSpec((1, half, c), lambda b, j: (b, j, 0))
    return pl.pallas_call(
        functools.partial(_pool_body, n_in=1, rows_in=POOL_ROWS, paged=False),
        grid=(n, l // POOL_ROWS),
        in_specs=[pl.BlockSpec(pmats.shape, lambda b, j: (0, 0, 0)),
                  pl.BlockSpec((1, POOL_ROWS, c), lambda b, j: (b, j, 0))],
        out_specs=[out_spec, out_spec],
        out_shape=[sds, sds],
        compiler_params=_params("parallel", "parallel"),
        name="nsa_pool",
    )(pmats, rows)


def _pages_view(cache):
    return jnp.transpose(cache, (0, 2, 3, 4, 1))


def _pool_paged_body(pt_ref, wt_ref, *refs, per_step, n_kv, page):
    page_refs = refs[:per_step]
    out_ref = refs[per_step]
    j = pl.program_id(1)
    width = out_ref.shape[4]
    half = width // 2

    @pl.when(j == 0)
    def _():
        out_ref[...] = jnp.zeros(out_ref.shape, F32)

    r = _iota((page, 1), 0)
    col = _iota((1, width), 1)
    blk_in_page = _fdiv(r, B_CMP)
    per_page = page // B_CMP
    for u in range(per_step):
        pg = j * per_step + u
        target = jnp.where(_fmod(blk_in_page, 2) == 1, half, 0) + _fdiv(per_page * pg + blk_in_page, 2)
        hit = col == target
        for x in range(2):
            pw = jnp.where(hit, wt_ref[x], 0.0).astype(BF16)
            rows_t = page_refs[u][0, :, x].reshape(n_kv * HEAD_DIM, page).astype(BF16)
            out_ref[0, :, x] += _dot(rows_t, pw).reshape(n_kv, HEAD_DIM, width)


def _pool_paged(cache_t, page_table, pool_w, per_step=16):
    _, n_kv, _, _, page = cache_t.shape
    n, n_pg = page_table.shape
    per_step = min(per_step, n_pg)
    assert n_pg % per_step == 0 and page % (2 * B_CMP) == 0
    width = n_pg * page // B_CMP
    wt = jnp.broadcast_to(jnp.tile(pool_w.astype(F32), (1, page // B_CMP))[:, :, None], (2, page, width))
    page_spec = lambda u: pl.BlockSpec((1, n_kv, 2, HEAD_DIM, page),
                                       lambda b, j, pt: (pt[b, j * per_step + u], 0, 0, 0, 0))
    return pl.pallas_call(
        functools.partial(_pool_paged_body, per_step=per_step, n_kv=n_kv, page=page),
        grid_spec=pltpu.PrefetchScalarGridSpec(
            num_scalar_prefetch=1,
            grid=(n, n_pg // per_step),
            in_specs=[pl.BlockSpec(wt.shape, lambda b, j, pt: (0, 0, 0))]
                     + [page_spec(u) for u in range(per_step)],
            out_specs=pl.BlockSpec((1, n_kv, 2, HEAD_DIM, width), lambda b, j, pt: (b, 0, 0, 0, 0))),
        out_shape=jax.ShapeDtypeStruct((n, n_kv, 2, HEAD_DIM, width), F32),
        compiler_params=_params("parallel", "arbitrary"),
        name="nsa_pool_paged",
    )(page_table, wt, *([cache_t] * per_step))


def _topk_mask(score, n_sel):
    rows, width = score.shape
    lane = _iota((rows, width), 1).astype(F32)
    sel = jnp.zeros((rows, width), F32)
    sc = score
    for _ in range(n_sel):
        mx = jnp.max(sc, axis=-1, keepdims=True)
        idx = jnp.min(jnp.where(sc == mx, lane, float(width)), axis=-1, keepdims=True)
        hit = lane == idx
        sel = jnp.where(hit, 1.0, sel)
        sc = jnp.where(hit, -4.0, sc)
    return sel


def _topk_mask_t(score_t, n_sel):
    width, cols = score_t.shape
    cand = _iota((width, cols), 0).astype(F32)
    sel = jnp.zeros((width, cols), F32)
    sc = score_t
    for _ in range(n_sel):
        mx = jnp.max(sc, axis=0, keepdims=True)
        idx = jnp.min(jnp.where(sc == mx, cand, float(width)), axis=0, keepdims=True)
        hit = cand == idx
        sel = jnp.where(hit, 1.0, sel)
        sc = jnp.where(hit, -4.0, sc)
    return sel


def _sel_score(imp, blk, q_pos, g_heads):
    cur = _fdiv(q_pos, B_SEL)
    forced = (blk == 0) | (blk == cur) | (blk == cur - 1)
    return jnp.where(blk * B_SEL <= q_pos, imp + jnp.where(forced, g_heads + 1.0, 0.0), -1.0)


def _nsa_cmp_body(slopes_ref, q_ref, ce_ref, co_ref, oc_ref, selt_ref, any_ref, *, tq, g_heads, n_sel):
    k = pl.program_id(1)
    i = pl.program_id(2)
    nbs = ce_ref.shape[1]
    q_pos = i * tq + _iota((tq, 1), 0)
    blk = _iota((1, nbs), 1)
    dist_e = q_pos - ((2 * blk + 1) * B_CMP - 1)
    dist_o = q_pos - ((2 * blk + 2) * B_CMP - 1)
    ok_e, ok_o = dist_e >= 0, dist_o >= 0
    de, do = dist_e.astype(F32), dist_o.astype(F32)
    ce = ce_ref[0].astype(BF16)
    co = co_ref[0].astype(BF16)
    ke, ve, ko, vo = ce[:, :HEAD_DIM], ce[:, HEAD_DIM:], co[:, :HEAD_DIM], co[:, HEAD_DIM:]
    imp_e = jnp.zeros((tq, nbs), F32)
    imp_o = jnp.zeros((tq, nbs), F32)
    for g in range(g_heads):
        slope = slopes_ref[k * g_heads + g]
        lo, hi = g * HEAD_DIM, (g + 1) * HEAD_DIM
        qg = q_ref[0, :, lo:hi]
        se = jnp.where(ok_e, _dot_nt(qg, ke) - slope * de, NEG)
        so = jnp.where(ok_o, _dot_nt(qg, ko) - slope * do, NEG)
        m = jnp.maximum(jnp.max(se, axis=-1, keepdims=True), jnp.max(so, axis=-1, keepdims=True))
        ee = jnp.where(ok_e, jnp.exp(se - m), 0.0)
        eo = jnp.where(ok_o, jnp.exp(so - m), 0.0)
        den = jnp.sum(ee, axis=-1, keepdims=True) + jnp.sum(eo, axis=-1, keepdims=True)
        den = jnp.where(den > 0, den, 1.0)
        pe, po = ee / den, eo / den
        oc_ref[0, :, lo:hi] = _dot(pe.astype(BF16), ve) + _dot(po.astype(BF16), vo)
        imp_e = imp_e + pe
        imp_o = imp_o + po
    score = _sel_score(imp_e + imp_o, blk, q_pos, g_heads)
    sel_t = _topk_mask_t(score.T, n_sel)
    selt_ref[0, 0] = sel_t.astype(BF16)
    any_ref[0, 0, 0] = jnp.broadcast_to(jnp.max(sel_t, axis=1, keepdims=True), (nbs, KV_LANES))


def _nsa_cmp(q, cmp_e, cmp_o, slopes, *, n_heads, n_kv, tq):
    n, s, _ = q.shape
    g_heads = n_heads // n_kv
    gw = g_heads * HEAD_DIM
    nbs = cmp_e.shape[1]
    n_sel = min(B_TOPN, nbs)
    assert s % tq == 0
    return pl.pallas_call(
        functools.partial(_nsa_cmp_body, tq=tq, g_heads=g_heads, n_sel=n_sel),
        grid=(n, n_kv, s // tq),
        in_specs=[_smem(),
                  pl.BlockSpec((1, tq, gw), lambda b, k, i: (b, i, k)),
                  pl.BlockSpec((1, nbs, KV_LANES), lambda b, k, i: (b, 0, k)),
                  pl.BlockSpec((1, nbs, KV_LANES), lambda b, k, i: (b, 0, k))],
        out_specs=[pl.BlockSpec((1, tq, gw), lambda b, k, i: (b, i, k)),
                   pl.BlockSpec((1, 1, nbs, tq), lambda b, k, i: (b, k, 0, i)),
                   pl.BlockSpec((1, 1, 1, nbs, KV_LANES), lambda b, k, i: (b, k, i, 0, 0))],
        out_shape=[jax.ShapeDtypeStruct((n, s, n_heads * HEAD_DIM), F32),
                   jax.ShapeDtypeStruct((n, n_kv, nbs, s), BF16),
                   jax.ShapeDtypeStruct((n, n_kv, s // tq, nbs, KV_LANES), F32)],
        compiler_params=_params("parallel", "parallel", "parallel"),
        name="nsa_cmp_select",
    )(slopes, q, cmp_e, cmp_o)


def _flash_update(s, valid, vb, m_ref, l_ref, acc_ref, idx, v_transposed=False):
    s = jnp.where(valid, s, NEG)
    m_prev = m_ref[idx]
    m_new = jnp.maximum(m_prev, jnp.max(s, axis=-1, keepdims=True))
    alpha = jnp.exp(m_prev - m_new)
    e = jnp.where(valid, jnp.exp(s - m_new), 0.0)
    l_ref[idx] = alpha * l_ref[idx] + jnp.sum(e, axis=-1, keepdims=True)
    pv = _dot_nt(e.astype(BF16), vb) if v_transposed else _dot(e.astype(BF16), vb)
    acc_ref[idx] = alpha * acc_ref[idx] + pv
    m_ref[idx] = m_new


def _flash_init(m_ref, l_ref, acc_ref):
    m_ref[...] = jnp.full(m_ref.shape, NEG, F32)
    l_ref[...] = jnp.zeros(l_ref.shape, F32)
    acc_ref[...] = jnp.zeros(acc_ref.shape, F32)


def _flash_result(l_ref, acc_ref, idx):
    l = l_ref[idx]
    return acc_ref[idx] / jnp.where(l > 0, l, 1.0)


MASK_BIG = 1e30
M_INIT = -3e38


def _nsa_sel_body(flags_ref, slopes_ref, qt_ref, selt_ref, kaug_ref, kvt_ref, o_ref,
                  qa_ref, bias_ref, m_ref, l_ref, acc_ref, *, t, g_heads, n_kv, n_q):
    b = pl.program_id(0)
    k = pl.program_id(1)
    i = pl.program_id(2)
    width = g_heads * t
    selbias = ((selt_ref[0, 0].astype(F32) - 1.0) * MASK_BIG).astype(BF16)
    r_minus_c = (_iota((1, t), 1) - _iota((t, 1), 0)).astype(F32)
    lane_head = _fdiv(_iota((1, width), 1), t)
    slope = jnp.zeros((1, width), F32)
    for g in range(g_heads):
        cols = slice(g * t, (g + 1) * t)
        qa_ref[0:HEAD_DIM, cols] = qt_ref[0, g]
        qa_ref[HEAD_DIM:KV_LANES, cols] = jnp.zeros((HEAD_DIM, t), BF16)
        qa_ref[KV_LANES:, cols] = selbias
        bias_ref[:, cols] = slopes_ref[k * g_heads + g] * r_minus_c
        slope = jnp.where(lane_head == g, slopes_ref[k * g_heads + g], slope)
    m_ref[...] = jnp.full(m_ref.shape, M_INIT, F32)
    l_ref[...] = jnp.zeros(l_ref.shape, F32)
    acc_ref[...] = jnp.zeros(acc_ref.shape, F32)

    def chunk(c, diagonal):
        k0 = pl.multiple_of(c * t, t)
        ka = kaug_ref[0, pl.ds(k0, t), :]
        kvt = kvt_ref[0, 0, :, pl.ds(k0, t)]
        s = _dot(ka, qa_ref[...]) - bias_ref[...]
        if diagonal:
            causal = jnp.where(_iota((t, 1), 0) > _iota((1, t), 1), NEG, 0.0)
            s = s + jnp.concatenate([causal] * g_heads, axis=1)
        delta = slope * jnp.full((1, width), (i - c) * t, jnp.int32).astype(F32)
        m_prev = m_ref[...]
        m_new = jnp.maximum(m_prev, jnp.max(s, axis=0, keepdims=True) - delta)
        alpha = jnp.exp(m_prev - m_new)
        e = jnp.exp(s - (m_new + delta))
        l_ref[...] = alpha * l_ref[...] + jnp.sum(e, axis=0, keepdims=True)
        acc_ref[...] = alpha * acc_ref[...] + _dot(kvt, e.astype(BF16))
        m_ref[...] = m_new

    def maybe_chunk(c, carry):
        @pl.when(flags_ref[((b * n_kv + k) * n_q + i) * n_q + c] != 0)
        def _():
            chunk(c, False)
        return carry

    lax.fori_loop(0, i, maybe_chunk, 0)
    chunk(i, True)
    l = l_ref[...]
    o_t = acc_ref[HEAD_DIM:KV_LANES, :] / jnp.where(l > 0, l, 1.0)
    for g in range(g_heads):
        o_ref[0, :, g * HEAD_DIM:(g + 1) * HEAD_DIM] = o_t[:, g * t:(g + 1) * t].T


def _nsa_sel(qt, selt, kaug, kvt, flags, slopes, *, n_heads, n_kv, t):
    n, _, _, s = qt.shape
    g_heads = n_heads // n_kv
    gw = g_heads * HEAD_DIM
    nbs = selt.shape[2]
    aug = KV_LANES + nbs
    n_q = s // t
    assert s % t == 0 and t % B_SEL == 0
    return pl.pallas_call(
        functools.partial(_nsa_sel_body, t=t, g_heads=g_heads, n_kv=n_kv, n_q=n_q),
        grid_spec=pltpu.PrefetchScalarGridSpec(
            num_scalar_prefetch=1,
            grid=(n, n_kv, n_q),
            in_specs=[_smem(),
                      pl.BlockSpec((1, g_heads, HEAD_DIM, t), lambda b, k, i, fl: (b, k, 0, i)),
                      pl.BlockSpec((1, 1, nbs, t), lambda b, k, i, fl: (b, k, 0, i)),
                      pl.BlockSpec((1, s, aug), lambda b, k, i, fl: (b, 0, k)),
                      pl.BlockSpec((1, 1, KV_LANES, s), lambda b, k, i, fl: (b, k, 0, 0))],
            out_specs=pl.BlockSpec((1, t, gw), lambda b, k, i, fl: (b, i, k)),
            scratch_shapes=[pltpu.VMEM((aug, g_heads * t), BF16), pltpu.VMEM((t, g_heads * t), F32),
                            pltpu.VMEM((1, g_heads * t), F32), pltpu.VMEM((1, g_heads * t), F32),
                            pltpu.VMEM((KV_LANES, g_heads * t), F32)]),
        out_shape=jax.ShapeDtypeStruct((n, s, n_heads * HEAD_DIM), F32),
        compiler_params=_params("parallel", "parallel", "parallel"),
        name="nsa_selected",
    )(flags, slopes, qt, selt, kaug, kvt)


T_PAD = 8
NEW_PAD = 16
KEY_PAD = 128


def _row_select(ref, base, g_heads, rows):
    gi = _fdiv(_iota((rows, 1), 0), T_PAD)
    out = jnp.zeros((rows, 1), F32)
    for g in range(g_heads):
        out = jnp.where(gi == g, ref[base + g], out)
    return out


def _pad_rows(x, lo, total):
    return jnp.pad(x, ((0, 0), (lo, total - lo - x.shape[1]), (0, 0)))


def _dec_window_body(*refs, lb, t_new, window, dilation, past_len, g_heads, n_kv, has_sink, with_lse):
    it = iter(refs)
    slopes_ref = next(it)
    sinks_ref = next(it) if has_sink else None
    q_ref, new_lo_ref, new_hi_ref, buf_ref = next(it), next(it), next(it), next(it)
    o_ref = next(it)
    lse_ref = next(it) if with_lse else None
    nbuf_ref = next(it)

    rows = g_heads * T_PAD
    span = lb + KEY_PAD
    t = _fmod(_iota((rows, 1), 0), T_PAD)
    r = _iota((1, span), 1)
    dist = lb + t - r
    valid = (dist >= 0) & (dist <= window) & (past_len - lb + r >= 0)
    if dilation > 1:
        valid = valid & (_fmod(dist, dilation) == 0)
    distf = dist.astype(F32)
    is_new = _iota((1, KEY_PAD), 1) >= KEY_PAD - t_new
    for k in range(n_kv):
        kt_buf, vt_buf = buf_ref[0, k, 0], buf_ref[0, k, 1]
        kt = jnp.concatenate([kt_buf, new_lo_ref[0, k, 0]], axis=1).astype(BF16)
        vt = jnp.concatenate([vt_buf, new_lo_ref[0, k, 1]], axis=1).astype(BF16)
        slope = _row_select(slopes_ref, k * g_heads, g_heads, rows)
        sink = _row_select(sinks_ref, k * g_heads, g_heads, rows) if has_sink else None
        s = _dot(q_ref[0, k], kt) - slope * distf
        p, m, den = _softmax_parts(s, valid, sink)
        o_ref[0, k] = _dot_nt(p.astype(BF16), vt)
        if with_lse:
            lse_ref[0, k] = jnp.broadcast_to(m + jnp.log(den), (rows, HEAD_DIM))
        for x, old in ((0, kt_buf), (1, vt_buf)):
            shifted = pltpu.roll(old, lb - t_new, axis=1)
            nbuf_ref[0, k, x] = shifted
            nbuf_ref[0, k, x, :, lb - KEY_PAD:lb] = jnp.where(is_new, new_hi_ref[0, k, x],
                                                              shifted[:, lb - KEY_PAD:lb])


def _state_view(buf):
    return jnp.transpose(buf, (0, 2, 3, 4, 1))


def _state_unview(buf_t):
    return jnp.transpose(buf_t, (0, 4, 1, 2, 3))


def _dec_window(qd, kv_new, buf, slopes, *, window, dilation, past_len, sinks=None, with_lse=False):
    n, n_kv, rows, _ = qd.shape
    g_heads = rows // T_PAD
    t_new = kv_new.shape[1]
    lb = buf.shape[1]
    assert lb % KEY_PAD == 0 and t_new <= T_PAD
    new_t = kv_new.reshape(n, t_new, n_kv, 2, HEAD_DIM).transpose(0, 2, 3, 4, 1)
    lane_pad = lambda lo: jnp.pad(new_t, ((0, 0),) * 4 + ((lo, KEY_PAD - t_new - lo),))
    new_lo, new_hi = lane_pad(0), lane_pad(KEY_PAD - t_new)
    buf_t = _state_view(buf)
    ins = [slopes]
    in_specs = [_smem()]
    if sinks is not None:
        ins.append(sinks)
        in_specs.append(_smem())
    ins += [qd, new_lo, new_hi, buf_t]
    o_spec = pl.BlockSpec((1, n_kv, rows, HEAD_DIM), lambda b: (b, 0, 0, 0))
    new_spec = pl.BlockSpec((1, n_kv, 2, HEAD_DIM, KEY_PAD), lambda b: (b, 0, 0, 0, 0))
    buf_spec = pl.BlockSpec((1, n_kv, 2, HEAD_DIM, lb), lambda b: (b, 0, 0, 0, 0))
    in_specs += [o_spec, new_spec, new_spec, buf_spec]
    o_sds = jax.ShapeDtypeStruct((n, n_kv, rows, HEAD_DIM), F32)
    out_specs = [o_spec] + ([o_spec] if with_lse else []) + [buf_spec]
    out_shape = [o_sds] + ([o_sds] if with_lse else []) + [jax.ShapeDtypeStruct(buf_t.shape, F32)]
    outs = pl.pallas_call(
        functools.partial(_dec_window_body, lb=lb, t_new=t_new, window=window, dilation=dilation,
                          past_len=past_len, g_heads=g_heads, n_kv=n_kv, has_sink=sinks is not None,
                          with_lse=with_lse),
        grid=(n,),
        in_specs=in_specs,
        out_specs=out_specs,
        out_shape=out_shape,
        compiler_params=_params("parallel"),
        name="decode_window",
    )(*ins)
    return tuple(outs[:-1]) + (_state_unview(outs[-1]),)


def _nsa_dec_cmp_body(slopes_ref, q_ref, cmpt_ref, cnew_ref, pnew_ref, oc_ref, sel_ref, *,
                      past_len, g_heads, n_kv, n_sel):
    rows = g_heads * T_PAD
    width = cmpt_ref.shape[4]
    nbs = width // 2
    t = _fmod(_iota((rows, 1), 0), T_PAD)
    q_pos = past_len + t
    lane = _iota((1, width), 1)
    cmp_blk = jnp.where(lane < nbs, 2 * lane, 2 * (lane - nbs) + 1)
    dist_c = q_pos - ((cmp_blk + 1) * B_CMP - 1)
    lane_n = _iota((1, NEW_PAD), 1)
    dist_ne = q_pos - ((2 * nbs + 1) * B_CMP - 1)
    dist_no = q_pos - ((2 * nbs + 2) * B_CMP - 1)
    cnew = cnew_ref[0].astype(BF16)
    is_k = _fmod(_iota((1, cnew.shape[1]), 1), KV_LANES) < HEAD_DIM
    new_e = jnp.where(is_k, _dot(pnew_ref[0], cnew), _dot(pnew_ref[1], cnew)).astype(BF16)
    new_o = jnp.zeros_like(new_e)
    new_sets = ((new_e, dist_ne, (dist_ne >= 0) & (lane_n == 0)),
                (new_o, dist_no, (dist_no >= 0) & (lane_n == 0)))
    ok_c = dist_c >= 0
    q_pos8 = past_len + _iota((T_PAD, 1), 0)
    blk = _iota((1, nbs), 1)
    lane_p = _iota((1, KEY_PAD), 1)
    for k in range(n_kv):
        c0 = k * KV_LANES
        slope = _row_select(slopes_ref, k * g_heads, g_heads, rows)
        q = q_ref[0, k]
        kt = cmpt_ref[0, k, 0].astype(BF16)
        vt = cmpt_ref[0, k, 1].astype(BF16)
        ss = [jnp.where(ok_c, _dot(q, kt) - slope * dist_c.astype(F32), NEG)]
        oks = [ok_c]
        for keys, dist, ok in new_sets:
            ss.append(jnp.where(ok, _dot_nt(q, keys[:, c0:c0 + HEAD_DIM]) - slope * dist.astype(F32), NEG))
            oks.append(ok)
        m = functools.reduce(jnp.maximum, [jnp.max(s, axis=-1, keepdims=True) for s in ss])
        es = [jnp.where(ok, jnp.exp(s - m), 0.0) for s, ok in zip(ss, oks)]
        den = _sum([jnp.sum(e, axis=-1, keepdims=True) for e in es])
        den = jnp.where(den > 0, den, 1.0)
        ps = [e / den for e in es]
        oc_ref[0, k] = (_dot_nt(ps[0].astype(BF16), vt)
                        + _sum([_dot(p.astype(BF16), keys[:, c0 + HEAD_DIM:c0 + KV_LANES])
                                for p, (keys, _, _) in zip(ps[1:], new_sets)]))
        gsum = lambda p: _sum([p[g * T_PAD:(g + 1) * T_PAD] for g in range(g_heads)])
        imp_c = gsum(ps[0])
        imp = imp_c[:, :nbs] + imp_c[:, nbs:]
        imp_new = gsum(ps[1]) + gsum(ps[2])
        score = _sel_score(imp, blk, q_pos8, g_heads)
        score_new = _sel_score(imp_new[:, 0:1], nbs + lane_p, q_pos8, g_heads)
        score_new = jnp.where(lane_p == 0, score_new, -3.0)
        sel_ref[0, k] = _topk_mask(jnp.concatenate([score, score_new], axis=1), n_sel)


def _nsa_dec_cmp(qd, cmpt, c_new, pnew, slopes, *, past_len):
    n, n_kv, rows, _ = qd.shape
    g_heads = rows // T_PAD
    width = cmpt.shape[4]
    nbs = width // 2
    c = c_new.shape[2]
    n_sel = min(B_TOPN, nbs + 1)
    cnew = _pad_rows(c_new, 0, NEW_PAD)
    q_spec = pl.BlockSpec((1, n_kv, rows, HEAD_DIM), lambda b: (b, 0, 0, 0))
    return pl.pallas_call(
        functools.partial(_nsa_dec_cmp_body, past_len=past_len, g_heads=g_heads, n_kv=n_kv, n_sel=n_sel),
        grid=(n,),
        in_specs=[_smem(), q_spec,
                  pl.BlockSpec((1, n_kv, 2, HEAD_DIM, width), lambda b: (b, 0, 0, 0, 0)),
                  pl.BlockSpec((1, NEW_PAD, c), lambda b: (b, 0, 0)),
                  pl.BlockSpec(pnew.shape, lambda b: (0, 0, 0))],
        out_specs=[q_spec, pl.BlockSpec((1, n_kv, T_PAD, nbs + KEY_PAD), lambda b: (b, 0, 0, 0))],
        out_shape=[jax.ShapeDtypeStruct((n, n_kv, rows, HEAD_DIM), F32),
                   jax.ShapeDtypeStruct((n, n_kv, T_PAD, nbs + KEY_PAD), F32)],
        compiler_params=_params("parallel"),
        name="nsa_decode_cmp_select",
    )(slopes, qd, cmpt, cnew, pnew)


def _nsa_dec_sel_body(*refs, past_len, g_heads, n_kv, per_step, page):
    logical_ref, count_ref = refs[1], refs[2]
    slopes_ref, q_ref, sel_ref, snew_ref = refs[3:7]
    page_refs = refs[7:7 + per_step]
    o_ref, m_ref, l_ref, acc_ref = refs[7 + per_step:]
    rows = g_heads * T_PAD
    b = pl.program_id(0)
    j = pl.program_id(1)
    nsel = sel_ref.shape[3]
    t = _fmod(_iota((rows, 1), 0), T_PAD)
    q_pos = past_len + t

    @pl.when(j == 0)
    def _():
        _flash_init(m_ref, l_ref, acc_ref)

    @pl.when(j * per_step < count_ref[b])
    def _():
        lane = _iota((1, page), 1)
        k_pos = jnp.concatenate([logical_ref[b, j * per_step + u] * page + lane for u in range(per_step)], axis=1)
        live = jnp.concatenate([jnp.full((1, page), j * per_step + u, jnp.int32) for u in range(per_step)],
                               axis=1) < count_ref[b]
        expand = (_iota((nsel, 1), 0) == _fdiv(k_pos, B_SEL)).astype(BF16)
        distf = (q_pos - k_pos).astype(F32)
        for k in range(n_kv):
            kt = jnp.concatenate([r[0, k, 0].astype(BF16) for r in page_refs], axis=1)
            vt = jnp.concatenate([r[0, k, 1].astype(BF16) for r in page_refs], axis=1)
            sel = jnp.concatenate([sel_ref[0, k]] * g_heads, axis=0).astype(BF16)
            valid = (_dot(sel, expand) > 0.5) & live & (k_pos <= q_pos)
            slope = _row_select(slopes_ref, k * g_heads, g_heads, rows)
            s = _dot(q_ref[0, k], kt) - slope * distf
            _flash_update(s, valid, vt, m_ref, l_ref, acc_ref, k, v_transposed=True)

    @pl.when(j == pl.num_programs(1) - 1)
    def _():
        n_past = past_len // B_SEL
        snew = snew_ref[0].astype(BF16)
        dist = t - _iota((1, NEW_PAD), 1)
        for k in range(n_kv):
            c0 = k * KV_LANES
            sel = jnp.concatenate([sel_ref[0, k]] * g_heads, axis=0)
            valid = (sel[:, n_past:n_past + 1] > 0.5) & (dist >= 0)
            slope = _row_select(slopes_ref, k * g_heads, g_heads, rows)
            s = _dot_nt(q_ref[0, k], snew[:, c0:c0 + HEAD_DIM]) - slope * dist.astype(F32)
            _flash_update(s, valid, snew[:, c0 + HEAD_DIM:c0 + KV_LANES], m_ref, l_ref, acc_ref, k)
            o_ref[0, k] = _flash_result(l_ref, acc_ref, k)


def _needed_pages(sel, page_table, t_new, page):
    n, n_pg = page_table.shape
    per_page = page // B_SEL
    picked = sel[:, :, :t_new, :n_pg * per_page].max(axis=(1, 2)) > 0
    need = picked.reshape(n, n_pg, per_page).any(axis=-1)
    count = need.sum(axis=1).astype(jnp.int32)
    order = jnp.argsort(jnp.logical_not(need), axis=1, stable=True).astype(jnp.int32)
    slot = jnp.minimum(jnp.arange(n_pg, dtype=jnp.int32)[None, :], jnp.maximum(count[:, None] - 1, 0))
    logical = jnp.take_along_axis(order, slot, axis=1)
    physical = jnp.take_along_axis(page_table, logical, axis=1)
    return physical, logical, count


def _nsa_dec_sel(qd, sel, s_new, cache_t, page_table, slopes, *, past_len, per_step=4):
    n, n_kv, rows, _ = qd.shape
    g_heads = rows // T_PAD
    page = cache_t.shape[4]
    t_new, c = s_new.shape[1:]
    n_pg = page_table.shape[1]
    per_step = min(per_step, n_pg)
    assert n_pg % per_step == 0 and past_len == n_pg * page
    snew = _pad_rows(s_new, 0, NEW_PAD)
    physical, logical, count = _needed_pages(sel, page_table, t_new, page)
    q_spec = pl.BlockSpec((1, n_kv, rows, HEAD_DIM), lambda b, j, *_: (b, 0, 0, 0))
    page_spec = lambda u: pl.BlockSpec((1, n_kv, 2, HEAD_DIM, page),
                                       lambda b, j, phys, *_: (phys[b, j * per_step + u], 0, 0, 0, 0))
    return pl.pallas_call(
        functools.partial(_nsa_dec_sel_body, past_len=past_len, g_heads=g_heads, n_kv=n_kv,
                          per_step=per_step, page=page),
        grid_spec=pltpu.PrefetchScalarGridSpec(
            num_scalar_prefetch=3,
            grid=(n, n_pg // per_step),
            in_specs=[_smem(), q_spec,
                      pl.BlockSpec((1, n_kv, T_PAD, sel.shape[3]), lambda b, j, *_: (b, 0, 0, 0)),
                      pl.BlockSpec((1, NEW_PAD, c), lambda b, j, *_: (b, 0, 0))]
                     + [page_spec(u) for u in range(per_step)],
            out_specs=q_spec,
            scratch_shapes=[pltpu.VMEM((n_kv, rows, 1), F32), pltpu.VMEM((n_kv, rows, 1), F32),
                            pltpu.VMEM((n_kv, rows, HEAD_DIM), F32)]),
        out_shape=jax.ShapeDtypeStruct((n, n_kv, rows, HEAD_DIM), F32),
        compiler_params=_params("parallel", "arbitrary"),
        name="nsa_decode_selected",
    )(physical, logical, count, slopes, qd, sel, snew, *([cache_t] * per_step))


def _nsa_merge_body(oc_ref, os_ref, ow_ref, bgz_ref, o_ref, *, n_heads):
    bg = jax.nn.sigmoid(bgz_ref[...])
    for h in range(n_heads):
        lo, hi = h * HEAD_DIM, (h + 1) * HEAD_DIM
        o_ref[:, lo:hi] = (bg[:, h:h + 1] * oc_ref[:, lo:hi]
                           + bg[:, n_heads + h:n_heads + h + 1] * os_ref[:, lo:hi]
                           + bg[:, 2 * n_heads + h:2 * n_heads + h + 1] * ow_ref[:, lo:hi])


def _nsa_merge(o_c, o_s, o_w, bgz, n_heads):
    m_rows, d = o_c.shape
    full = lambda a: pl.BlockSpec(a.shape, lambda i: (0, 0))
    return pl.pallas_call(
        functools.partial(_nsa_merge_body, n_heads=n_heads),
        grid=(1,),
        in_specs=[full(o_c), full(o_s), full(o_w), full(bgz)],
        out_specs=full(o_c),
        out_shape=jax.ShapeDtypeStruct((m_rows, d), F32),
        compiler_params=_params("arbitrary"),
        name="nsa_decode_merge",
    )(o_c, o_s, o_w, bgz)


def _to_decode_rows(q, n_kv):
    n, t, hd = q.shape
    g_heads = hd // HEAD_DIM // n_kv
    q = q.reshape(n, t, n_kv, g_heads, HEAD_DIM).transpose(0, 2, 3, 1, 4)
    q = jnp.pad(q, ((0, 0), (0, 0), (0, 0), (0, T_PAD - t), (0, 0)))
    return q.reshape(n, n_kv, g_heads * T_PAD, HEAD_DIM)


def _from_decode_rows(o, t):
    n, n_kv, rows, _ = o.shape
    g_heads = rows // T_PAD
    o = o.reshape(n, n_kv, g_heads, T_PAD, HEAD_DIM)[:, :, :, :t]
    return o.transpose(0, 3, 1, 2, 4).reshape(n * t, n_kv * g_heads * HEAD_DIM)


def _kv_state(rows, n_kv):
    return rows.reshape(*rows.shape[:-1], n_kv, 2, HEAD_DIM)


def _mixer_a(xp, xs, buf, g_pre, g_post, w_in, w_out, sinks, past_len, tm):
    n, s, d = xp.shape
    m, t, _ = xs.shape
    n_heads = d // HEAD_DIM
    n_kv = n_heads // 8
    nq, nkv = n_heads * HEAD_DIM, n_kv * KV_LANES
    slopes = _alibi_slopes(n_heads)
    w_in_b, w_out_b = w_in.astype(BF16), w_out.astype(BF16)
    splits = [(0, nq, "q"), (nq, nkv, "f32"), (nq, nkv, "bf16"), (nq + nkv, nq, "f32")]
    q, kv, kvb, gate = _proj(xp.reshape(n * s, d), g_pre, w_in_b, splits, tm)
    o = _banded(q.reshape(n, s, nq), kvb.reshape(n, s, nkv), slopes, n_heads=n_heads, n_kv=n_kv,
                window=A_WINDOW, stride=1, sinks=sinks)
    yp = _outproj([o.reshape(n * s, nq)], [], gate, xp.reshape(n * s, d), g_post, w_out_b, tm)
    keep = min(A_WINDOW, s)
    state_p = _kv_state(kv.reshape(n, s, nkv)[:, s - keep:], n_kv)
    qs, kvs, _, gs = _proj(xs.reshape(m * t, d), g_pre, w_in_b, splits, tm)
    od, nbuf = _dec_window(_to_decode_rows(qs.reshape(m, t, nq), n_kv), kvs.reshape(m, t, nkv), buf, slopes,
                           window=A_WINDOW, dilation=1, past_len=past_len, sinks=sinks)
    ys = _outproj([_from_decode_rows(od, t)], [], gs, xs.reshape(m * t, d), g_post, w_out_b, tm)
    return yp.reshape(n, s, d), ys.reshape(m, t, d), [state_p], [nbuf]


def _mixer_b(xp, xs, cache_cmp, cache_sel, buf_win, page_table, g_pre, g_post, w_in, w_out, cmp_pool,
             past_len, tm):
    n, s, d = xp.shape
    m, t, _ = xs.shape
    n_heads = d // HEAD_DIM
    n_kv = n_heads // 4
    nq, nkv = n_heads * HEAD_DIM, n_kv * KV_LANES
    n_bg = 3 * n_heads
    slopes = _alibi_slopes(n_heads)
    w_in_b, w_out_b = w_in.astype(BF16), w_out.astype(BF16)
    pmats = _pool_mats(cmp_pool)
    c0, s0, w0, g0, b0 = nq, nq + nkv, nq + 2 * nkv, nq + 3 * nkv, 2 * nq + 3 * nkv
    base = [(0, nq, "q"), (c0, nkv, "f32"), (s0, nkv, "f32"), (w0, nkv, "f32"), (g0, nq, "f32"),
            (b0, n_bg, "f32")]
    splits = [(0, nq, "q"), (c0, nkv, "f32"), (c0, nkv, "f32t"), (s0, nkv, "f32t"), (w0, nkv, "f32"),
              (g0, nq, "f32"), (b0, n_bg, "f32"), (s0, nkv, "bf16"), (s0, nkv, "kaug"), (w0, nkv, "bf16")]
    q, c_p, c_pt, s_pt, w_p, gate, bgz, s_pb, kaug, w_pb = _proj(xp.reshape(n * s, d), g_pre, w_in_b, splits, tm,
                                                                 seq_len=s)
    q = q.reshape(n, s, nq)
    cmp_e, cmp_o = _pool_prompt(c_p.reshape(n, s, nkv), pmats)
    t_sel = min(256, s)
    o_c, selt, sel_any = _nsa_cmp(q, cmp_e, cmp_o, slopes, n_heads=n_heads, n_kv=n_kv, tq=t_sel)
    per_tile = t_sel // B_SEL
    flags = (sel_any[:, :, :, :, 0].reshape(n, n_kv, s // t_sel, s // t_sel, per_tile).max(axis=-1) > 0)
    flags = flags.astype(jnp.int32).reshape(-1)
    qt = q.reshape(n, s, n_heads, HEAD_DIM).transpose(0, 2, 3, 1)
    kvt = s_pb.reshape(n, s, n_kv, KV_LANES).transpose(0, 2, 3, 1)
    o_s = _nsa_sel(qt, selt, kaug.reshape(n, s, -1), kvt, flags, slopes, n_heads=n_heads, n_kv=n_kv, t=t_sel)
    o = _banded(q, w_pb.reshape(n, s, nkv), slopes, n_heads=n_heads, n_kv=n_kv, window=B_WINDOW, stride=1,
                merge=(o_c, o_s, bgz.reshape(n, s, n_bg)))
    yp = _outproj([o.reshape(n * s, nq)], [], gate, xp.reshape(n * s, d), g_post, w_out_b, tm)
    keep = min(B_WINDOW, s)
    from_t = lambda a: a.reshape(n, n_kv, 2, HEAD_DIM, s).transpose(0, 4, 1, 2, 3)
    states_p = [from_t(c_pt), from_t(s_pt), _kv_state(w_p.reshape(n, s, nkv)[:, s - keep:], n_kv)]
    qs, c_s, s_s, w_s, gs, bgzs = _proj(xs.reshape(m * t, d), g_pre, w_in_b, base, tm)
    qd = _to_decode_rows(qs.reshape(m, t, nq), n_kv)
    c_s, s_s, w_s = (a.reshape(m, t, nkv) for a in (c_s, s_s, w_s))
    cmpt = _pool_paged(_pages_view(cache_cmp), page_table, cmp_pool)
    n_w = min(NEW_PAD, B_CMP)
    pnew = jnp.zeros((2, NEW_PAD, NEW_PAD), F32).at[:, 0, :n_w].set(cmp_pool.astype(F32)[:, :n_w]).astype(BF16)
    oc_d, sel_d = _nsa_dec_cmp(qd, cmpt, c_s, pnew, slopes, past_len=past_len)
    os_d = _nsa_dec_sel(qd, sel_d, s_s, _pages_view(cache_sel), page_table, slopes, past_len=past_len)
    ow_d, nwin = _dec_window(qd, w_s, buf_win, slopes, window=B_WINDOW, dilation=1, past_len=past_len)
    osm = _nsa_merge(_from_decode_rows(oc_d, t), _from_decode_rows(os_d, t), _from_decode_rows(ow_d, t),
                     bgzs, n_heads)
    ys = _outproj([osm], [], gs, xs.reshape(m * t, d), g_post, w_out_b, tm)
    states_s = [_kv_state(c_s, n_kv), _kv_state(s_s, n_kv), nwin]
    return yp.reshape(n, s, d), ys.reshape(m, t, d), states_p, states_s


def _mixer_c(xp, xs, bufs, g_pre, g_post, w_in, w_out, past_len, tm):
    n, s, d = xp.shape
    m, t, _ = xs.shape
    n_heads = d // HEAD_DIM
    n_kv = n_heads // 8
    ng = len(C_GROUPS)
    nq1, nkv1 = n_heads * HEAD_DIM, n_kv * KV_LANES
    nq, nkv = ng * nq1, ng * nkv1
    slopes = _alibi_slopes(n_heads)
    w_in_b, w_out_b = w_in.astype(BF16), w_out.astype(BF16)
    splits = [(0, nq, "q"), (nq, nkv, "f32"), (nq, nkv, "bf16"), (nq + nkv, nq1, "f32")]
    q, kv, kvb, gate = _proj(xp.reshape(n * s, d), g_pre, w_in_b, splits, tm)
    q, kv, kvb = q.reshape(n, s, nq), kv.reshape(n, s, nkv), kvb.reshape(n, s, nkv)
    outs, lses, states_p = [], [], []
    for gi, (w, dil) in enumerate(C_GROUPS):
        o, lse = _banded(q, kvb, slopes, n_heads=n_heads, n_kv=n_kv, window=w, stride=dil, q_group=gi,
                         with_lse=True)
        outs.append(o.reshape(n * s, nq1))
        lses.append(lse.reshape(n * s, nq1))
        keep = min(w, s)
        states_p.append(_kv_state(kv[:, s - keep:, gi * nkv1:(gi + 1) * nkv1], n_kv))
    yp = _outproj(outs, lses, gate, xp.reshape(n * s, d), g_post, w_out_b, tm)
    qs, kvs, _, gs = _proj(xs.reshape(m * t, d), g_pre, w_in_b, splits, tm)
    qs, kvs = qs.reshape(m, t, nq), kvs.reshape(m, t, nkv)
    outs, lses, states_s = [], [], []
    for gi, (w, dil) in enumerate(C_GROUPS):
        buf = bufs[gi]
        assert buf.shape[1] == w
        od, ld, nbuf = _dec_window(_to_decode_rows(qs[:, :, gi * nq1:(gi + 1) * nq1], n_kv),
                                   kvs[:, :, gi * nkv1:(gi + 1) * nkv1], buf, slopes,
                                   window=w, dilation=dil, past_len=past_len, with_lse=True)
        outs.append(_from_decode_rows(od, t))
        lses.append(_from_decode_rows(ld, t))
        states_s.append(nbuf)
    ys = _outproj(outs, lses, gs, xs.reshape(m * t, d), g_post, w_out_b, tm)
    return yp.reshape(n, s, d), ys.reshape(m, t, d), states_p, states_s


def kernel(x_prompt, x_sample, state_a0_kv, cache_b1_cmp_kv, cache_b1_sel_kv, state_b1_win_kv,
           state_c2_g0_kv, state_c2_g1_kv, state_c2_g2_kv, state_a3_kv, page_table, norm_g,
           w_in_0, w_out_0, sinks_0, w_in_1, w_out_1, cmp_pool_1, w_in_2, w_out_2,
           w_in_3, w_out_3, sinks_3):
    past_len = page_table.shape[1] * cache_b1_cmp_kv.shape[1]
    tm = 256
    xp, xs = x_prompt, x_sample
    xp, xs, (a0_p,), (a0_s,) = _mixer_a(xp, xs, state_a0_kv, norm_g[0, 0], norm_g[0, 1], w_in_0, w_out_0,
                                        sinks_0, past_len, tm)
    xp, xs, (b1_cmp_p, b1_sel_p, b1_win_p), (b1_cmp_s, b1_sel_s, b1_win_s) = _mixer_b(
        xp, xs, cache_b1_cmp_kv, cache_b1_sel_kv, state_b1_win_kv, page_table, norm_g[1, 0], norm_g[1, 1],
        w_in_1, w_out_1, cmp_pool_1, past_len, tm)
    xp, xs, (c2_g0_p, c2_g1_p, c2_g2_p), (c2_g0_s, c2_g1_s, c2_g2_s) = _mixer_c(
        xp, xs, (state_c2_g0_kv, state_c2_g1_kv, state_c2_g2_kv), norm_g[2, 0], norm_g[2, 1], w_in_2, w_out_2,
        past_len, tm)
    xp, xs, (a3_p,), (a3_s,) = _mixer_a(xp, xs, state_a3_kv, norm_g[3, 0], norm_g[3, 1], w_in_3, w_out_3,
                                        sinks_3, past_len, tm)
    return (xp, xs, a0_p, a0_s, b1_cmp_p, b1_cmp_s, b1_sel_p, b1_sel_s, b1_win_p, b1_win_s,
            c2_g0_p, c2_g0_s, c2_g1_p, c2_g1_s, c2_g2_p, c2_g2_s, a3_p, a3_s)
```

```python
import functools

import jax
import jax.numpy as jnp
from jax import lax
from jax.experimental import pallas as pl
from jax.experimental.pallas import tpu as pltpu

F32 = jnp.float32
BF16 = jnp.bfloat16

HEAD_DIM = 64
KV_LANES = 2 * HEAD_DIM
RMS_EPS = 1e-6
NEG = -1e30
SCALE = HEAD_DIM ** -0.5

A_WINDOW = 128
B_CMP = 32
B_SEL = 64
B_TOPN = 16
B_WINDOW = 512
C_GROUPS = ((128, 1), (512, 4), (2048, 16))

VMEM_LIMIT_BYTES = 48 * 1024 * 1024


def _params(*sem):
    return pltpu.CompilerParams(dimension_semantics=sem, vmem_limit_bytes=VMEM_LIMIT_BYTES)


def _smem():
    return pl.BlockSpec(memory_space=pltpu.SMEM)


def _dot_nt(a, b):
    return lax.dot_general(a, b, (((1,), (1,)), ((), ())), preferred_element_type=F32)


def _dot(a, b):
    return jnp.dot(a, b, preferred_element_type=F32)


def _iota(shape, dim):
    return lax.broadcasted_iota(jnp.int32, shape, dim)


def _log2(d):
    assert d > 0 and d & (d - 1) == 0
    return d.bit_length() - 1


def _fdiv(x, d):
    return lax.shift_right_arithmetic(x, _log2(d))


def _fmod(x, d):
    return x & (d - 1)


def _sum(xs):
    return functools.reduce(lambda a, b: a + b, xs)


def _alibi_slopes(n_heads):
    h = jnp.arange(1, n_heads + 1, dtype=F32)
    return jnp.exp2(-8.0 * h / n_heads)


def _softmax_parts(s, valid, sink=None):
    s = jnp.where(valid, s, NEG)
    m = jnp.max(s, axis=-1, keepdims=True)
    if sink is not None:
        m = jnp.maximum(m, sink)
    e = jnp.where(valid, jnp.exp(s - m), 0.0)
    den = jnp.sum(e, axis=-1, keepdims=True)
    if sink is not None:
        den = den + jnp.exp(sink - m)
    den = jnp.where(den > 0, den, 1.0)
    return e / den, m, den


def _proj_body(x_ref, g_ref, w_ref, *out_refs, splits, chunk, tm, seq_len):
    x = x_ref[...]
    ms = jnp.mean(x * x, axis=-1, keepdims=True)
    h = ((x * lax.rsqrt(ms + RMS_EPS)) * g_ref[...]).astype(BF16)
    h_by_dilation = {}
    for o_ref, (start, width, mode) in zip(out_refs, splits):
        if "@" in mode:
            mode, dil = mode.split("@")
            dil = int(dil)
            per = tm // dil
            if dil not in h_by_dilation:
                dst = _iota((tm, 1), 0)
                src = _fmod(dst, per) * dil + _fdiv(dst, per)
                perm = (_iota((1, tm), 1) == src).astype(BF16)
                h_by_dilation[dil] = _dot(perm, h).astype(BF16)
            for c0 in range(0, width, chunk):
                cw = min(chunk, width - c0)
                z = _dot(h_by_dilation[dil], w_ref[:, start + c0:start + c0 + cw])
                if mode == "q":
                    z = z * SCALE
                z = z.astype(o_ref.dtype)
                for r in range(dil):
                    o_ref[:, r * width + c0:r * width + c0 + cw] = z[r * per:(r + 1) * per]
            continue
        if mode == "kaug":
            nbs = seq_len // B_SEL
            aug = KV_LANES + nbs
            pos = _fmod(pl.program_id(0) * tm + _iota((tm, 1), 0), seq_len)
            onehot = (_iota((1, nbs), 1) == _fdiv(pos, B_SEL)).astype(o_ref.dtype)
            is_k = _iota((1, KV_LANES), 1) < HEAD_DIM
            for j in range(width // KV_LANES):
                z = _dot(h, w_ref[:, start + j * KV_LANES:start + (j + 1) * KV_LANES])
                o_ref[:, j * aug:j * aug + KV_LANES] = jnp.where(is_k, z, 0.0).astype(o_ref.dtype)
                o_ref[:, j * aug + KV_LANES:(j + 1) * aug] = onehot
            continue
        for c0 in range(0, width, chunk):
            cw = min(chunk, width - c0)
            z = _dot(h, w_ref[:, start + c0:start + c0 + cw])
            if mode == "f32t":
                o_ref[0, c0:c0 + cw, :] = z.T
                continue
            if mode == "q":
                z = z * SCALE
            o_ref[:, c0:c0 + cw] = z.astype(o_ref.dtype)


_PROJ_DTYPES = {"f32": F32, "bf16": BF16, "q": BF16, "kaug": BF16, "f32t": F32}


def _proj(x2d, g, w_bf16, splits, tm, seq_len=None):
    m_rows, d = x2d.shape
    n = w_bf16.shape[1]
    tm = min(tm, m_rows)
    assert m_rows % tm == 0

    out_specs, out_shape = [], []
    for _, width, mode in splits:
        if "@" in mode:
            mode, dil = mode.split("@")
            dil = int(dil)
            assert tm % (dil * NEW_PAD) == 0
            out_specs.append(pl.BlockSpec((tm // dil, dil * width), lambda i: (i, 0)))
            out_shape.append(jax.ShapeDtypeStruct((m_rows // dil, dil * width), _PROJ_DTYPES[mode]))
            continue
        if mode == "f32t":
            per_seq = seq_len // tm
            assert seq_len % tm == 0
            out_specs.append(pl.BlockSpec((1, width, tm), lambda i: (i // per_seq, 0, i % per_seq)))
            out_shape.append(jax.ShapeDtypeStruct((m_rows // seq_len, width, seq_len), F32))
            continue
        if mode == "kaug":
            width = width // KV_LANES * (KV_LANES + seq_len // B_SEL)
        out_specs.append(pl.BlockSpec((tm, width), lambda i: (i, 0)))
        out_shape.append(jax.ShapeDtypeStruct((m_rows, width), _PROJ_DTYPES[mode]))
    return pl.pallas_call(
        functools.partial(_proj_body, splits=tuple(splits), chunk=512, tm=tm, seq_len=seq_len),
        grid=(m_rows // tm,),
        in_specs=[pl.BlockSpec((tm, d), lambda i: (i, 0)),
                  pl.BlockSpec((1, d), lambda i: (0, 0)),
                  pl.BlockSpec((d, n), lambda i: (0, 0))],
        out_specs=out_specs,
        out_shape=out_shape,
        compiler_params=_params("parallel"),
        name="proj",
    )(x2d, g.reshape(1, d), w_bf16)


def _outproj_body(*refs, n_branch):
    if n_branch == 1:
        o_ref, gate_ref, x_ref, g_ref, w_ref, y_ref = refs
        o = o_ref[...]
    else:
        o_refs = refs[:n_branch]
        l_refs = refs[n_branch:2 * n_branch]
        gate_ref, x_ref, g_ref, w_ref, y_ref = refs[2 * n_branch:]
        ls = [r[...] for r in l_refs]
        m = functools.reduce(jnp.maximum, ls)
        es = [jnp.exp(l - m) for l in ls]
        den = _sum(es)
        o = _sum([(e / den) * r[...] for e, r in zip(es, o_refs)])
    gate = gate_ref[...]
    a = (o * (gate * jax.nn.sigmoid(gate))).astype(BF16)
    y = _dot(a, w_ref[...])
    ms = jnp.mean(y * y, axis=-1, keepdims=True)
    y_ref[...] = x_ref[...] + (y * lax.rsqrt(ms + RMS_EPS)) * g_ref[...]


def _outproj(os_, lses, gate, x2d, g, w_bf16, tm):
    m_rows, d = x2d.shape
    k = w_bf16.shape[0]
    tm = min(tm, m_rows)
    assert m_rows % tm == 0
    row = lambda width: pl.BlockSpec((tm, width), lambda i: (i, 0))
    n_branch = len(os_)
    ins = list(os_) + list(lses) + [gate, x2d, g.reshape(1, d), w_bf16]
    in_specs = ([row(k)] * (n_branch + len(lses)) + [row(k), row(d),
                pl.BlockSpec((1, d), lambda i: (0, 0)), pl.BlockSpec((k, d), lambda i: (0, 0))])
    return pl.pallas_call(
        functools.partial(_outproj_body, n_branch=n_branch),
        grid=(m_rows // tm,),
        in_specs=in_specs,
        out_specs=row(d),
        out_shape=jax.ShapeDtypeStruct((m_rows, d), F32),
        compiler_params=_params("parallel"),
        name="outproj",
    )(*ins)


SUB_Q = 128


def _head_rows(ref, base, g_heads, rows_per_head):
    rows = g_heads * rows_per_head
    gi = _fdiv(_iota((rows, 1), 0), rows_per_head)
    out = jnp.zeros((rows, 1), F32)
    for g in range(g_heads):
        out = jnp.where(gi == g, ref[base + g], out)
    return out


def _banded_body(*refs, tq, wsub, stride, window, g_heads, has_sink, with_lse, nsa_merge):
    it = iter(refs)
    slopes_ref = next(it)
    sinks_ref = next(it) if has_sink else None
    q_ref = next(it)
    kv_ref = next(it)
    if nsa_merge:
        oc_ref, os_ref, bgz_ref = next(it), next(it), next(it)
    o_ref = next(it)
    lse_ref = next(it) if with_lse else None
    bias_ref = next(it)

    k = pl.program_id(2)
    i = pl.program_id(3)
    span = wsub + SUB_Q
    rows = g_heads * SUB_Q
    n_var = wsub // SUB_Q + 1

    @pl.when(i == 0)
    def _():
        r = _fmod(_iota((rows, 1), 0), SUB_Q)
        c = _iota((1, span), 1)
        slope = _head_rows(slopes_ref, k * g_heads, g_heads, SUB_Q)
        for v in range(n_var):
            dist = (wsub - SUB_Q * v + r - c) * stride
            ok = (dist >= 0) & (dist <= window)
            bias_ref[v] = jnp.where(ok, -(slope * dist.astype(F32)), NEG)

    sink = _head_rows(sinks_ref, k * g_heads, g_heads, SUB_Q) if has_sink else None
    for sb in range(tq // SUB_Q):
        r0 = sb * SUB_Q
        q0 = i * tq + r0
        kst = pl.multiple_of(jnp.maximum(q0 - wsub, 0), SUB_Q)
        var = _fdiv(jnp.maximum(wsub - q0, 0), SUB_Q)
        kv = kv_ref[0, pl.ds(kst, span), :]
        qs = jnp.concatenate([q_ref[0, r0:r0 + SUB_Q, g * HEAD_DIM:(g + 1) * HEAD_DIM]
                              for g in range(g_heads)], axis=0)
        s = _dot_nt(qs, kv[:, :HEAD_DIM]) + bias_ref[var]
        m = jnp.max(s, axis=-1, keepdims=True)
        if has_sink:
            m = jnp.maximum(m, sink)
        e = jnp.exp(s - m)
        den = jnp.sum(e, axis=-1, keepdims=True)
        if has_sink:
            den = den + jnp.exp(sink - m)
        o = _dot(e.astype(BF16), kv[:, HEAD_DIM:]) / den
        if with_lse:
            lse = m + jnp.log(den)
        if nsa_merge:
            bg = jax.nn.sigmoid(bgz_ref[0, 0, r0:r0 + SUB_Q, :])
        for g in range(g_heads):
            lo, hi = g * HEAD_DIM, (g + 1) * HEAD_DIM
            og = o[g * SUB_Q:(g + 1) * SUB_Q]
            if nsa_merge:
                col = lambda j: bg[:, j * g_heads + g:j * g_heads + g + 1]
                og = (col(0) * oc_ref[0, r0:r0 + SUB_Q, lo:hi] + col(1) * os_ref[0, r0:r0 + SUB_Q, lo:hi]
                      + col(2) * og)
            o_ref[0, r0:r0 + SUB_Q, lo:hi] = og
            if with_lse:
                lse_ref[0, r0:r0 + SUB_Q, lo:hi] = jnp.broadcast_to(lse[g * SUB_Q:(g + 1) * SUB_Q],
                                                                    (SUB_Q, HEAD_DIM))


def _banded(q, kv, slopes, *, n_heads, n_kv, window, stride, q_group=0, sinks=None,
            with_lse=False, merge=None, tq=512, viewed=False):
    g_heads = n_heads // n_kv
    gw = g_heads * HEAD_DIM
    if viewed:
        (n, sp, qc), kc = q.shape, kv.shape[2] // stride
        qc, s = qc // stride, sp * stride
        qv, kvv = q, kv
    else:
        n, s, qc = q.shape
        kc = kv.shape[2]
        sp = s // stride
        qv = q.reshape(n, sp, stride * qc)
        kvv = kv.reshape(n, sp, stride * kc)
    wsub = window // stride
    tq = min(tq, sp)
    assert sp % tq == 0 and wsub % SUB_Q == 0 and tq % SUB_Q == 0 and wsub + SUB_Q <= sp
    q_blocks, kv_blocks, o_blocks = qc // gw, kc // KV_LANES, n_kv
    q_map = lambda b, r, k, i: (b, i, r * q_blocks + q_group * n_kv + k)
    kv_map = lambda b, r, k, i: (b, 0, r * kv_blocks + q_group * n_kv + k)
    o_map = lambda b, r, k, i: (b, i, r * o_blocks + k)
    ins = [slopes]
    in_specs = [_smem()]
    if sinks is not None:
        ins.append(sinks)
        in_specs.append(_smem())
    ins += [qv, kvv]
    in_specs += [pl.BlockSpec((1, tq, gw), q_map), pl.BlockSpec((1, sp, KV_LANES), kv_map)]
    if merge is not None:
        o_c, o_s, bgz = merge
        bgz = bgz.reshape(n, s, 3, n_kv, g_heads).transpose(0, 3, 1, 2, 4).reshape(n, n_kv, s, 3 * g_heads)
        ins += [o_c, o_s, bgz]
        in_specs += [pl.BlockSpec((1, tq, gw), o_map), pl.BlockSpec((1, tq, gw), o_map),
                     pl.BlockSpec((1, 1, tq, 3 * g_heads), lambda b, r, k, i: (b, k, i, 0))]
    o_sds = jax.ShapeDtypeStruct((n, sp, stride * n_heads * HEAD_DIM), F32)
    o_spec = pl.BlockSpec((1, tq, gw), o_map)
    out = pl.pallas_call(
        functools.partial(_banded_body, tq=tq, wsub=wsub, stride=stride, window=window, g_heads=g_heads,
                          has_sink=sinks is not None, with_lse=with_lse, nsa_merge=merge is not None),
        grid=(n, stride, n_kv, sp // tq),
        in_specs=in_specs,
        out_specs=[o_spec, o_spec] if with_lse else o_spec,
        out_shape=[o_sds, o_sds] if with_lse else o_sds,
        scratch_shapes=[pltpu.VMEM((wsub // SUB_Q + 1, g_heads * SUB_Q, wsub + SUB_Q), F32)],
        compiler_params=_params("parallel", "parallel", "parallel", "arbitrary"),
        name="banded_attn",
    )(*ins)
    if with_lse:
        return tuple(a.reshape(n, s, n_heads * HEAD_DIM) for a in out)
    return out.reshape(n, s, n_heads * HEAD_DIM)


POOL_ROWS = 1024


def _pool_body(*refs, n_in, rows_in, paged):
    if paged:
        refs = refs[1:]
    p_ref = refs[0]
    in_refs = refs[1:1 + n_in]
    ce_ref, co_ref = refs[1 + n_in:]
    is_k = _fmod(_iota((1, ce_ref.shape[2]), 1), KV_LANES) < HEAD_DIM
    for parity, out_ref in ((0, ce_ref), (1, co_ref)):
        acc_k = None
        acc_v = None
        for j, r in enumerate(in_refs):
            rows = r[0].astype(BF16)
            pk = p_ref[parity, :, j * rows_in:(j + 1) * rows_in]
            pv = p_ref[2 + parity, :, j * rows_in:(j + 1) * rows_in]
            dk, dv = _dot(pk, rows), _dot(pv, rows)
            acc_k = dk if acc_k is None else acc_k + dk
            acc_v = dv if acc_v is None else acc_v + dv
        out_ref[0] = jnp.where(is_k, acc_k, acc_v)


def _pool_mats(pool_w):
    nb = POOL_ROWS // B_CMP
    eye = jnp.eye(nb, dtype=F32)
    mats = []
    for x in range(2):
        full = jnp.kron(eye, pool_w[x].astype(F32)[None, :])
        mats += [full[0::2], full[1::2]]
    return jnp.stack(mats).astype(BF16)


def _pool_prompt(rows, pmats):
    n, l, c = rows.shape
    assert l % POOL_ROWS == 0
    half = POOL_ROWS // B_CMP // 2
    sds = jax.ShapeDtypeStruct((n, l // B_CMP // 2, c), F32)
    out_spec = pl.BlockSpec((1, half, c), lambda b, j: (b, j, 0))
    return pl.pallas_call(
        functools.partial(_pool_body, n_in=1, rows_in=POOL_ROWS, paged=False),
        grid=(n, l // POOL_ROWS),
        in_specs=[pl.BlockSpec(pmats.shape, lambda b, j: (0, 0, 0)),
                  pl.BlockSpec((1, POOL_ROWS, c), lambda b, j: (b, j, 0))],
        out_specs=[out_spec, out_spec],
        out_shape=[sds, sds],
        compiler_params=_params("parallel", "parallel"),
        name="nsa_pool",
    )(pmats, rows)


def _pages_view(cache):
    return jnp.transpose(cache, (0, 2, 3, 4, 1))


def _pool_paged_body(pt_ref, wt_ref, *refs, per_step, n_kv, page):
    page_refs = refs[:per_step]
    out_ref = refs[per_step]
    j = pl.program_id(1)
    width = out_ref.shape[4]
    half = width // 2

    @pl.when(j == 0)
    def _():
        out_ref[...] = jnp.zeros(out_ref.shape, F32)

    r = _iota((page, 1), 0)
    col = _iota((1, width), 1)
    blk_in_page = _fdiv(r, B_CMP)
    per_page = page // B_CMP
    hits = []
    for u in range(per_step):
        pg = j * per_step + u
        target = jnp.where(_fmod(blk_in_page, 2) == 1, half, 0) + _fdiv(per_page * pg + blk_in_page, 2)
        hits.append(col == target)
    for x in range(2):
        pw = jnp.concatenate([jnp.where(hit, wt_ref[x], 0.0).astype(BF16) for hit in hits], axis=0)
        rows_t = jnp.concatenate([ref[0, :, x].reshape(n_kv * HEAD_DIM, page).astype(BF16)
                                  for ref in page_refs], axis=1)
        out_ref[0, :, x] += _dot(rows_t, pw).reshape(n_kv, HEAD_DIM, width)


def _pool_paged(cache_t, page_table, pool_w, per_step=16):
    _, n_kv, _, _, page = cache_t.shape
    n, n_pg = page_table.shape
    per_step = min(per_step, n_pg)
    assert n_pg % per_step == 0 and page % (2 * B_CMP) == 0
    width = n_pg * page // B_CMP
    wt = jnp.broadcast_to(jnp.tile(pool_w.astype(F32), (1, page // B_CMP))[:, :, None], (2, page, width))
    page_spec = lambda u: pl.BlockSpec((1, n_kv, 2, HEAD_DIM, page),
                                       lambda b, j, pt: (pt[b, j * per_step + u], 0, 0, 0, 0))
    return pl.pallas_call(
        functools.partial(_pool_paged_body, per_step=per_step, n_kv=n_kv, page=page),
        grid_spec=pltpu.PrefetchScalarGridSpec(
            num_scalar_prefetch=1,
            grid=(n, n_pg // per_step),
            in_specs=[pl.BlockSpec(wt.shape, lambda b, j, pt: (0, 0, 0))]
                     + [page_spec(u) for u in range(per_step)],
            out_specs=pl.BlockSpec((1, n_kv, 2, HEAD_DIM, width), lambda b, j, pt: (b, 0, 0, 0, 0))),
        out_shape=jax.ShapeDtypeStruct((n, n_kv, 2, HEAD_DIM, width), F32),
        compiler_params=_params("parallel", "arbitrary"),
        name="nsa_pool_paged",
    )(page_table, wt, *([cache_t] * per_step))


def _topk_mask(score, n_sel):
    rows, width = score.shape
    lane = _iota((rows, width), 1).astype(F32)
    sel = jnp.zeros((rows, width), F32)
    sc = score
    for _ in range(n_sel):
        mx = jnp.max(sc, axis=-1, keepdims=True)
        idx = jnp.min(jnp.where(sc == mx, lane, float(width)), axis=-1, keepdims=True)
        hit = lane == idx
        sel = jnp.where(hit, 1.0, sel)
        sc = jnp.where(hit, -4.0, sc)
    return sel


def _topk_mask_t(score_t, n_sel):
    width, cols = score_t.shape
    cand = _iota((width, cols), 0).astype(F32)
    sel = jnp.zeros((width, cols), F32)
    sc = score_t
    for _ in range(n_sel):
        mx = jnp.max(sc, axis=0, keepdims=True)
        idx = jnp.min(jnp.where(sc == mx, cand, float(width)), axis=0, keepdims=True)
        hit = cand == idx
        sel = jnp.where(hit, 1.0, sel)
        sc = jnp.where(hit, -4.0, sc)
    return sel


def _sel_score(imp, blk, q_pos, g_heads):
    cur = _fdiv(q_pos, B_SEL)
    forced = (blk == 0) | (blk == cur) | (blk == cur - 1)
    return jnp.where(blk * B_SEL <= q_pos, imp + jnp.where(forced, g_heads + 1.0, 0.0), -1.0)


def _nsa_cmp_body(slopes_ref, q_ref, ce_ref, co_ref, oc_ref, selt_ref, any_ref, *, tq, g_heads, n_sel):
    k = pl.program_id(1)
    i = pl.program_id(2)
    nbs = ce_ref.shape[1]
    q_pos = i * tq + _iota((tq, 1), 0)
    blk = _iota((1, nbs), 1)
    dist_e = q_pos - ((2 * blk + 1) * B_CMP - 1)
    dist_o = q_pos - ((2 * blk + 2) * B_CMP - 1)
    ok_e, ok_o = dist_e >= 0, dist_o >= 0
    de, do = dist_e.astype(F32), dist_o.astype(F32)
    ce = ce_ref[0].astype(BF16)
    co = co_ref[0].astype(BF16)
    ke, ve, ko, vo = ce[:, :HEAD_DIM], ce[:, HEAD_DIM:], co[:, :HEAD_DIM], co[:, HEAD_DIM:]
    imp_e = jnp.zeros((tq, nbs), F32)
    imp_o = jnp.zeros((tq, nbs), F32)
    for g in range(g_heads):
        slope = slopes_ref[k * g_heads + g]
        lo, hi = g * HEAD_DIM, (g + 1) * HEAD_DIM
        qg = q_ref[0, :, lo:hi]
        se = jnp.where(ok_e, _dot_nt(qg, ke) - slope * de, NEG)
        so = jnp.where(ok_o, _dot_nt(qg, ko) - slope * do, NEG)
        m = jnp.maximum(jnp.max(se, axis=-1, keepdims=True), jnp.max(so, axis=-1, keepdims=True))
        ee = jnp.where(ok_e, jnp.exp(se - m), 0.0)
        eo = jnp.where(ok_o, jnp.exp(so - m), 0.0)
        den = jnp.sum(ee, axis=-1, keepdims=True) + jnp.sum(eo, axis=-1, keepdims=True)
        den = jnp.where(den > 0, den, 1.0)
        pe, po = ee / den, eo / den
        oc_ref[0, :, lo:hi] = _dot(pe.astype(BF16), ve) + _dot(po.astype(BF16), vo)
        imp_e = imp_e + pe
        imp_o = imp_o + po
    score = _sel_score(imp_e + imp_o, blk, q_pos, g_heads)
    sel_t = _topk_mask_t(score.T, n_sel)
    selt_ref[0, 0] = sel_t.astype(BF16)
    any_ref[0, 0, 0] = jnp.broadcast_to(jnp.max(sel_t, axis=1, keepdims=True), (nbs, KV_LANES))


def _nsa_cmp(q, cmp_e, cmp_o, slopes, *, n_heads, n_kv, tq):
    n, s, _ = q.shape
    g_heads = n_heads // n_kv
    gw = g_heads * HEAD_DIM
    nbs = cmp_e.shape[1]
    n_sel = min(B_TOPN, nbs)
    assert s % tq == 0
    return pl.pallas_call(
        functools.partial(_nsa_cmp_body, tq=tq, g_heads=g_heads, n_sel=n_sel),
        grid=(n, n_kv, s // tq),
        in_specs=[_smem(),
                  pl.BlockSpec((1, tq, gw), lambda b, k, i: (b, i, k)),
                  pl.BlockSpec((1, nbs, KV_LANES), lambda b, k, i: (b, 0, k)),
                  pl.BlockSpec((1, nbs, KV_LANES), lambda b, k, i: (b, 0, k))],
        out_specs=[pl.BlockSpec((1, tq, gw), lambda b, k, i: (b, i, k)),
                   pl.BlockSpec((1, 1, nbs, tq), lambda b, k, i: (b, k, 0, i)),
                   pl.BlockSpec((1, 1, 1, nbs, KV_LANES), lambda b, k, i: (b, k, i, 0, 0))],
        out_shape=[jax.ShapeDtypeStruct((n, s, n_heads * HEAD_DIM), F32),
                   jax.ShapeDtypeStruct((n, n_kv, nbs, s), BF16),
                   jax.ShapeDtypeStruct((n, n_kv, s // tq, nbs, KV_LANES), F32)],
        compiler_params=_params("parallel", "parallel", "parallel"),
        name="nsa_cmp_select",
    )(slopes, q, cmp_e, cmp_o)


def _flash_update(s, valid, vb, m_ref, l_ref, acc_ref, idx, v_transposed=False):
    s = jnp.where(valid, s, NEG)
    m_prev = m_ref[idx]
    m_new = jnp.maximum(m_prev, jnp.max(s, axis=-1, keepdims=True))
    alpha = jnp.exp(m_prev - m_new)
    e = jnp.where(valid, jnp.exp(s - m_new), 0.0)
    l_ref[idx] = alpha * l_ref[idx] + jnp.sum(e, axis=-1, keepdims=True)
    pv = _dot_nt(e.astype(BF16), vb) if v_transposed else _dot(e.astype(BF16), vb)
    acc_ref[idx] = alpha * acc_ref[idx] + pv
    m_ref[idx] = m_new


def _flash_init(m_ref, l_ref, acc_ref):
    m_ref[...] = jnp.full(m_ref.shape, NEG, F32)
    l_ref[...] = jnp.zeros(l_ref.shape, F32)
    acc_ref[...] = jnp.zeros(acc_ref.shape, F32)


def _flash_result(l_ref, acc_ref, idx):
    l = l_ref[idx]
    return acc_ref[idx] / jnp.where(l > 0, l, 1.0)


MASK_BIG = 1e30
M_INIT = -3e38


def _nsa_sel_body(flags_ref, slopes_ref, qt_ref, selt_ref, kaug_ref, kvt_ref, o_ref,
                  qa_ref, bias_ref, m_ref, l_ref, acc_ref, *, t, g_heads, n_kv, n_q):
    b = pl.program_id(0)
    k = pl.program_id(1)
    i = pl.program_id(2)
    width = g_heads * t
    selbias = ((selt_ref[0, 0].astype(F32) - 1.0) * MASK_BIG).astype(BF16)
    r_minus_c = (_iota((1, t), 1) - _iota((t, 1), 0)).astype(F32)
    lane_head = _fdiv(_iota((1, width), 1), t)
    slope = jnp.zeros((1, width), F32)
    for g in range(g_heads):
        cols = slice(g * t, (g + 1) * t)
        qa_ref[0:HEAD_DIM, cols] = qt_ref[0, g]
        qa_ref[HEAD_DIM:KV_LANES, cols] = jnp.zeros((HEAD_DIM, t), BF16)
        qa_ref[KV_LANES:, cols] = selbias
        bias_ref[:, cols] = slopes_ref[k * g_heads + g] * r_minus_c
        slope = jnp.where(lane_head == g, slopes_ref[k * g_heads + g], slope)
    m_ref[...] = jnp.full(m_ref.shape, M_INIT, F32)
    l_ref[...] = jnp.zeros(l_ref.shape, F32)
    acc_ref[...] = jnp.zeros(acc_ref.shape, F32)

    def chunk(c, diagonal):
        k0 = pl.multiple_of(c * t, t)
        ka = kaug_ref[0, pl.ds(k0, t), :]
        kvt = kvt_ref[0, 0, :, pl.ds(k0, t)]
        s = _dot(ka, qa_ref[...]) - bias_ref[...]
        if diagonal:
            causal = jnp.where(_iota((t, 1), 0) > _iota((1, t), 1), NEG, 0.0)
            s = s + jnp.concatenate([causal] * g_heads, axis=1)
        delta = slope * jnp.full((1, width), (i - c) * t, jnp.int32).astype(F32)
        m_prev = m_ref[...]
        m_new = jnp.maximum(m_prev, jnp.max(s, axis=0, keepdims=True) - delta)
        alpha = jnp.exp(m_prev - m_new)
        e = jnp.exp(s - (m_new + delta))
        l_ref[...] = alpha * l_ref[...] + jnp.sum(e, axis=0, keepdims=True)
        acc_ref[...] = alpha * acc_ref[...] + _dot(kvt, e.astype(BF16))
        m_ref[...] = m_new

    def maybe_chunk(c, carry):
        @pl.when(flags_ref[((b * n_kv + k) * n_q + i) * n_q + c] != 0)
        def _():
            chunk(c, False)
        return carry

    lax.fori_loop(0, i, maybe_chunk, 0)
    chunk(i, True)
    l = l_ref[...]
    o_t = acc_ref[HEAD_DIM:KV_LANES, :] / jnp.where(l > 0, l, 1.0)
    for g in range(g_heads):
        o_ref[0, :, g * HEAD_DIM:(g + 1) * HEAD_DIM] = o_t[:, g * t:(g + 1) * t].T


def _nsa_sel(qt, selt, kaug, kvt, flags, slopes, *, n_heads, n_kv, t):
    n, _, _, s = qt.shape
    g_heads = n_heads // n_kv
    gw = g_heads * HEAD_DIM
    nbs = selt.shape[2]
    aug = KV_LANES + nbs
    n_q = s // t
    assert s % t == 0 and t % B_SEL == 0
    return pl.pallas_call(
        functools.partial(_nsa_sel_body, t=t, g_heads=g_heads, n_kv=n_kv, n_q=n_q),
        grid_spec=pltpu.PrefetchScalarGridSpec(
            num_scalar_prefetch=1,
            grid=(n, n_kv, n_q),
            in_specs=[_smem(),
                      pl.BlockSpec((1, g_heads, HEAD_DIM, t), lambda b, k, i, fl: (b, k, 0, i)),
                      pl.BlockSpec((1, 1, nbs, t), lambda b, k, i, fl: (b, k, 0, i)),
                      pl.BlockSpec((1, s, aug), lambda b, k, i, fl: (b, 0, k)),
                      pl.BlockSpec((1, 1, KV_LANES, s), lambda b, k, i, fl: (b, k, 0, 0))],
            out_specs=pl.BlockSpec((1, t, gw), lambda b, k, i, fl: (b, i, k)),
            scratch_shapes=[pltpu.VMEM((aug, g_heads * t), BF16), pltpu.VMEM((t, g_heads * t), F32),
                            pltpu.VMEM((1, g_heads * t), F32), pltpu.VMEM((1, g_heads * t), F32),
                            pltpu.VMEM((KV_LANES, g_heads * t), F32)]),
        out_shape=jax.ShapeDtypeStruct((n, s, n_heads * HEAD_DIM), F32),
        compiler_params=_params("parallel", "parallel", "parallel"),
        name="nsa_selected",
    )(flags, slopes, qt, selt, kaug, kvt)


T_PAD = 8
NEW_PAD = 16
KEY_PAD = 128


def _row_select(ref, base, g_heads, rows):
    gi = _fdiv(_iota((rows, 1), 0), T_PAD)
    out = jnp.zeros((rows, 1), F32)
    for g in range(g_heads):
        out = jnp.where(gi == g, ref[base + g], out)
    return out


def _pad_rows(x, lo, total):
    return jnp.pad(x, ((0, 0), (lo, total - lo - x.shape[1]), (0, 0)))


def _dec_window_body(*refs, lb, t_new, window, dilation, past_len, g_heads, n_kv, has_sink, with_lse):
    it = iter(refs)
    slopes_ref = next(it)
    sinks_ref = next(it) if has_sink else None
    q_ref, new_lo_ref, new_hi_ref, buf_ref = next(it), next(it), next(it), next(it)
    o_ref = next(it)
    lse_ref = next(it) if with_lse else None
    nbuf_ref = next(it)

    rows = g_heads * T_PAD
    span = lb + KEY_PAD
    t = _fmod(_iota((rows, 1), 0), T_PAD)
    r = _iota((1, span), 1)
    dist = lb + t - r
    valid = (dist >= 0) & (dist <= window) & (past_len - lb + r >= 0)
    if dilation > 1:
        valid = valid & (_fmod(dist, dilation) == 0)
    distf = dist.astype(F32)
    is_new = _iota((1, KEY_PAD), 1) >= KEY_PAD - t_new
    for k in range(n_kv):
        kt_buf, vt_buf = buf_ref[0, k, 0], buf_ref[0, k, 1]
        kt = jnp.concatenate([kt_buf, new_lo_ref[0, k, 0]], axis=1).astype(BF16)
        vt = jnp.concatenate([vt_buf, new_lo_ref[0, k, 1]], axis=1).astype(BF16)
        slope = _row_select(slopes_ref, k * g_heads, g_heads, rows)
        sink = _row_select(sinks_ref, k * g_heads, g_heads, rows) if has_sink else None
        s = _dot(q_ref[0, k], kt) - slope * distf
        p, m, den = _softmax_parts(s, valid, sink)
        o_ref[0, k] = _dot_nt(p.astype(BF16), vt)
        if with_lse:
            lse_ref[0, k] = jnp.broadcast_to(m + jnp.log(den), (rows, HEAD_DIM))
        for x, old in ((0, kt_buf), (1, vt_buf)):
            shifted = pltpu.roll(old, lb - t_new, axis=1)
            nbuf_ref[0, k, x] = shifted
            nbuf_ref[0, k, x, :, lb - KEY_PAD:lb] = jnp.where(is_new, new_hi_ref[0, k, x],
                                                              shifted[:, lb - KEY_PAD:lb])


def _state_view(buf):
    return jnp.transpose(buf, (0, 2, 3, 4, 1))


def _state_unview(buf_t):
    return jnp.transpose(buf_t, (0, 4, 1, 2, 3))


def _dec_window(qd, kv_new, buf, slopes, *, window, dilation, past_len, sinks=None, with_lse=False):
    n, n_kv, rows, _ = qd.shape
    g_heads = rows // T_PAD
    t_new = kv_new.shape[1]
    lb = buf.shape[1]
    assert lb % KEY_PAD == 0 and t_new <= T_PAD
    new_t = kv_new.reshape(n, t_new, n_kv, 2, HEAD_DIM).transpose(0, 2, 3, 4, 1)
    lane_pad = lambda lo: jnp.pad(new_t, ((0, 0),) * 4 + ((lo, KEY_PAD - t_new - lo),))
    new_lo, new_hi = lane_pad(0), lane_pad(KEY_PAD - t_new)
    buf_t = _state_view(buf)
    ins = [slopes]
    in_specs = [_smem()]
    if sinks is not None:
        ins.append(sinks)
        in_specs.append(_smem())
    ins += [qd, new_lo, new_hi, buf_t]
    o_spec = pl.BlockSpec((1, n_kv, rows, HEAD_DIM), lambda b: (b, 0, 0, 0))
    new_spec = pl.BlockSpec((1, n_kv, 2, HEAD_DIM, KEY_PAD), lambda b: (b, 0, 0, 0, 0))
    buf_spec = pl.BlockSpec((1, n_kv, 2, HEAD_DIM, lb), lambda b: (b, 0, 0, 0, 0))
    in_specs += [o_spec, new_spec, new_spec, buf_spec]
    o_sds = jax.ShapeDtypeStruct((n, n_kv, rows, HEAD_DIM), F32)
    out_specs = [o_spec] + ([o_spec] if with_lse else []) + [buf_spec]
    out_shape = [o_sds] + ([o_sds] if with_lse else []) + [jax.ShapeDtypeStruct(buf_t.shape, F32)]
    outs = pl.pallas_call(
        functools.partial(_dec_window_body, lb=lb, t_new=t_new, window=window, dilation=dilation,
                          past_len=past_len, g_heads=g_heads, n_kv=n_kv, has_sink=sinks is not None,
                          with_lse=with_lse),
        grid=(n,),
        in_specs=in_specs,
        out_specs=out_specs,
        out_shape=out_shape,
        compiler_params=_params("parallel"),
        name="decode_window",
    )(*ins)
    return tuple(outs[:-1]) + (_state_unview(outs[-1]),)


def _nsa_dec_cmp_body(slopes_ref, q_ref, cmpt_ref, cnew_ref, pnew_ref, oc_ref, sel_ref, *,
                      past_len, g_heads, n_kv, n_sel):
    rows = g_heads * T_PAD
    width = cmpt_ref.shape[4]
    nbs = width // 2
    t = _fmod(_iota((rows, 1), 0), T_PAD)
    q_pos = past_len + t
    lane = _iota((1, width), 1)
    cmp_blk = jnp.where(lane < nbs, 2 * lane, 2 * (lane - nbs) + 1)
    dist_c = q_pos - ((cmp_blk + 1) * B_CMP - 1)
    lane_n = _iota((1, NEW_PAD), 1)
    dist_ne = q_pos - ((2 * nbs + 1) * B_CMP - 1)
    dist_no = q_pos - ((2 * nbs + 2) * B_CMP - 1)
    cnew = cnew_ref[0].astype(BF16)
    is_k = _fmod(_iota((1, cnew.shape[1]), 1), KV_LANES) < HEAD_DIM
    new_e = jnp.where(is_k, _dot(pnew_ref[0], cnew), _dot(pnew_ref[1], cnew)).astype(BF16)
    new_o = jnp.zeros_like(new_e)
    new_sets = ((new_e, dist_ne, (dist_ne >= 0) & (lane_n == 0)),
                (new_o, dist_no, (dist_no >= 0) & (lane_n == 0)))
    ok_c = dist_c >= 0
    q_pos8 = past_len + _iota((T_PAD, 1), 0)
    blk = _iota((1, nbs), 1)
    lane_p = _iota((1, KEY_PAD), 1)
    for k in range(n_kv):
        c0 = k * KV_LANES
        slope = _row_select(slopes_ref, k * g_heads, g_heads, rows)
        q = q_ref[0, k]
        kt = cmpt_ref[0, k, 0].astype(BF16)
        vt = cmpt_ref[0, k, 1].astype(BF16)
        ss = [jnp.where(ok_c, _dot(q, kt) - slope * dist_c.astype(F32), NEG)]
        oks = [ok_c]
        for keys, dist, ok in new_sets:
            ss.append(jnp.where(ok, _dot_nt(q, keys[:, c0:c0 + HEAD_DIM]) - slope * dist.astype(F32), NEG))
            oks.append(ok)
        m = functools.reduce(jnp.maximum, [jnp.max(s, axis=-1, keepdims=True) for s in ss])
        es = [jnp.where(ok, jnp.exp(s - m), 0.0) for s, ok in zip(ss, oks)]
        den = _sum([jnp.sum(e, axis=-1, keepdims=True) for e in es])
        den = jnp.where(den > 0, den, 1.0)
        ps = [e / den for e in es]
        oc_ref[0, k] = (_dot_nt(ps[0].astype(BF16), vt)
                        + _sum([_dot(p.astype(BF16), keys[:, c0 + HEAD_DIM:c0 + KV_LANES])
                                for p, (keys, _, _) in zip(ps[1:], new_sets)]))
        gsum = lambda p: _sum([p[g * T_PAD:(g + 1) * T_PAD] for g in range(g_heads)])
        imp_c = gsum(ps[0])
        imp = imp_c[:, :nbs] + imp_c[:, nbs:]
        imp_new = gsum(ps[1]) + gsum(ps[2])
        score = _sel_score(imp, blk, q_pos8, g_heads)
        score_new = _sel_score(imp_new[:, 0:1], nbs + lane_p, q_pos8, g_heads)
        score_new = jnp.where(lane_p == 0, score_new, -3.0)
        sel_ref[0, k] = _topk_mask(jnp.concatenate([score, score_new], axis=1), n_sel)


def _nsa_dec_cmp(qd, cmpt, c_new, pnew, slopes, *, past_len):
    n, n_kv, rows, _ = qd.shape
    g_heads = rows // T_PAD
    width = cmpt.shape[4]
    nbs = width // 2
    c = c_new.shape[2]
    n_sel = min(B_TOPN, nbs + 1)
    cnew = _pad_rows(c_new, 0, NEW_PAD)
    q_spec = pl.BlockSpec((1, n_kv, rows, HEAD_DIM), lambda b: (b, 0, 0, 0))
    return pl.pallas_call(
        functools.partial(_nsa_dec_cmp_body, past_len=past_len, g_heads=g_heads, n_kv=n_kv, n_sel=n_sel),
        grid=(n,),
        in_specs=[_smem(), q_spec,
                  pl.BlockSpec((1, n_kv, 2, HEAD_DIM, width), lambda b: (b, 0, 0, 0, 0)),
                  pl.BlockSpec((1, NEW_PAD, c), lambda b: (b, 0, 0)),
                  pl.BlockSpec(pnew.shape, lambda b: (0, 0, 0))],
        out_specs=[q_spec, pl.BlockSpec((1, n_kv, T_PAD, nbs + KEY_PAD), lambda b: (b, 0, 0, 0))],
        out_shape=[jax.ShapeDtypeStruct((n, n_kv, rows, HEAD_DIM), F32),
                   jax.ShapeDtypeStruct((n, n_kv, T_PAD, nbs + KEY_PAD), F32)],
        compiler_params=_params("parallel"),
        name="nsa_decode_cmp_select",
    )(slopes, qd, cmpt, cnew, pnew)


def _nsa_dec_sel_body(*refs, past_len, g_heads, n_kv, per_step, page):
    logical_ref, count_ref = refs[1], refs[2]
    slopes_ref, q_ref, sel_ref, snew_ref = refs[3:7]
    page_refs = refs[7:7 + per_step]
    o_ref, m_ref, l_ref, acc_ref = refs[7 + per_step:]
    rows = g_heads * T_PAD
    b = pl.program_id(0)
    j = pl.program_id(1)
    nsel = sel_ref.shape[3]
    t = _fmod(_iota((rows, 1), 0), T_PAD)
    q_pos = past_len + t

    @pl.when(j == 0)
    def _():
        _flash_init(m_ref, l_ref, acc_ref)

    @pl.when(j * per_step < count_ref[b])
    def _():
        lane = _iota((1, page), 1)
        k_pos = jnp.concatenate([logical_ref[b, j * per_step + u] * page + lane for u in range(per_step)], axis=1)
        live = jnp.concatenate([jnp.full((1, page), j * per_step + u, jnp.int32) for u in range(per_step)],
                               axis=1) < count_ref[b]
        expand = (_iota((nsel, 1), 0) == _fdiv(k_pos, B_SEL)).astype(BF16)
        distf = (q_pos - k_pos).astype(F32)
        for k in range(n_kv):
            kt = jnp.concatenate([r[0, k, 0].astype(BF16) for r in page_refs], axis=1)
            vt = jnp.concatenate([r[0, k, 1].astype(BF16) for r in page_refs], axis=1)
            sel = jnp.concatenate([sel_ref[0, k]] * g_heads, axis=0).astype(BF16)
            valid = (_dot(sel, expand) > 0.5) & live & (k_pos <= q_pos)
            slope = _row_select(slopes_ref, k * g_heads, g_heads, rows)
            s = _dot(q_ref[0, k], kt) - slope * distf
            _flash_update(s, valid, vt, m_ref, l_ref, acc_ref, k, v_transposed=True)

    @pl.when(j == pl.num_programs(1) - 1)
    def _():
        n_past = past_len // B_SEL
        snew = snew_ref[0].astype(BF16)
        dist = t - _iota((1, NEW_PAD), 1)
        for k in range(n_kv):
            c0 = k * KV_LANES
            sel = jnp.concatenate([sel_ref[0, k]] * g_heads, axis=0)
            valid = (sel[:, n_past:n_past + 1] > 0.5) & (dist >= 0)
            slope = _row_select(slopes_ref, k * g_heads, g_heads, rows)
            s = _dot_nt(q_ref[0, k], snew[:, c0:c0 + HEAD_DIM]) - slope * dist.astype(F32)
            _flash_update(s, valid, snew[:, c0 + HEAD_DIM:c0 + KV_LANES], m_ref, l_ref, acc_ref, k)
            o_ref[0, k] = _flash_result(l_ref, acc_ref, k)


def _needed_pages(sel, page_table, t_new, page):
    n, n_pg = page_table.shape
    per_page = page // B_SEL
    picked = sel[:, :, :t_new, :n_pg * per_page].max(axis=(1, 2)) > 0
    need = picked.reshape(n, n_pg, per_page).any(axis=-1)
    count = need.sum(axis=1).astype(jnp.int32)
    order = jnp.argsort(jnp.logical_not(need), axis=1, stable=True).astype(jnp.int32)
    slot = jnp.minimum(jnp.arange(n_pg, dtype=jnp.int32)[None, :], jnp.maximum(count[:, None] - 1, 0))
    logical = jnp.take_along_axis(order, slot, axis=1)
    physical = jnp.take_along_axis(page_table, logical, axis=1)
    return physical, logical, count


def _nsa_dec_sel(qd, sel, s_new, cache_t, page_table, slopes, *, past_len, per_step=4):
    n, n_kv, rows, _ = qd.shape
    g_heads = rows // T_PAD
    page = cache_t.shape[4]
    t_new, c = s_new.shape[1:]
    n_pg = page_table.shape[1]
    per_step = min(per_step, n_pg)
    assert n_pg % per_step == 0 and past_len == n_pg * page
    snew = _pad_rows(s_new, 0, NEW_PAD)
    physical, logical, count = _needed_pages(sel, page_table, t_new, page)
    q_spec = pl.BlockSpec((1, n_kv, rows, HEAD_DIM), lambda b, j, *_: (b, 0, 0, 0))
    page_spec = lambda u: pl.BlockSpec((1, n_kv, 2, HEAD_DIM, page),
                                       lambda b, j, phys, *_: (phys[b, j * per_step + u], 0, 0, 0, 0))
    return pl.pallas_call(
        functools.partial(_nsa_dec_sel_body, past_len=past_len, g_heads=g_heads, n_kv=n_kv,
                          per_step=per_step, page=page),
        grid_spec=pltpu.PrefetchScalarGridSpec(
            num_scalar_prefetch=3,
            grid=(n, n_pg // per_step),
            in_specs=[_smem(), q_spec,
                      pl.BlockSpec((1, n_kv, T_PAD, sel.shape[3]), lambda b, j, *_: (b, 0, 0, 0)),
                      pl.BlockSpec((1, NEW_PAD, c), lambda b, j, *_: (b, 0, 0))]
                     + [page_spec(u) for u in range(per_step)],
            out_specs=q_spec,
            scratch_shapes=[pltpu.VMEM((n_kv, rows, 1), F32), pltpu.VMEM((n_kv, rows, 1), F32),
                            pltpu.VMEM((n_kv, rows, HEAD_DIM), F32)]),
        out_shape=jax.ShapeDtypeStruct((n, n_kv, rows, HEAD_DIM), F32),
        compiler_params=_params("parallel", "arbitrary"),
        name="nsa_decode_selected",
    )(physical, logical, count, slopes, qd, sel, snew, *([cache_t] * per_step))


def _nsa_merge_body(oc_ref, os_ref, ow_ref, bgz_ref, o_ref, *, n_heads):
    bg = jax.nn.sigmoid(bgz_ref[...])
    for h in range(n_heads):
        lo, hi = h * HEAD_DIM, (h + 1) * HEAD_DIM
        o_ref[:, lo:hi] = (bg[:, h:h + 1] * oc_ref[:, lo:hi]
                           + bg[:, n_heads + h:n_heads + h + 1] * os_ref[:, lo:hi]
                           + bg[:, 2 * n_heads + h:2 * n_heads + h + 1] * ow_ref[:, lo:hi])


def _nsa_merge(o_c, o_s, o_w, bgz, n_heads):
    m_rows, d = o_c.shape
    full = lambda a: pl.BlockSpec(a.shape, lambda i: (0, 0))
    return pl.pallas_call(
        functools.partial(_nsa_merge_body, n_heads=n_heads),
        grid=(1,),
        in_specs=[full(o_c), full(o_s), full(o_w), full(bgz)],
        out_specs=full(o_c),
        out_shape=jax.ShapeDtypeStruct((m_rows, d), F32),
        compiler_params=_params("arbitrary"),
        name="nsa_decode_merge",
    )(o_c, o_s, o_w, bgz)


def _to_decode_rows(q, n_kv):
    n, t, hd = q.shape
    g_heads = hd // HEAD_DIM // n_kv
    q = q.reshape(n, t, n_kv, g_heads, HEAD_DIM).transpose(0, 2, 3, 1, 4)
    q = jnp.pad(q, ((0, 0), (0, 0), (0, 0), (0, T_PAD - t), (0, 0)))
    return q.reshape(n, n_kv, g_heads * T_PAD, HEAD_DIM)


def _from_decode_rows(o, t):
    n, n_kv, rows, _ = o.shape
    g_heads = rows // T_PAD
    o = o.reshape(n, n_kv, g_heads, T_PAD, HEAD_DIM)[:, :, :, :t]
    return o.transpose(0, 3, 1, 2, 4).reshape(n * t, n_kv * g_heads * HEAD_DIM)


def _kv_state(rows, n_kv):
    return rows.reshape(*rows.shape[:-1], n_kv, 2, HEAD_DIM)


def _mixer_a(xp, xs, buf, g_pre, g_post, w_in, w_out, sinks, past_len, tm):
    n, s, d = xp.shape
    m, t, _ = xs.shape
    n_heads = d // HEAD_DIM
    n_kv = n_heads // 8
    nq, nkv = n_heads * HEAD_DIM, n_kv * KV_LANES
    slopes = _alibi_slopes(n_heads)
    w_in_b, w_out_b = w_in.astype(BF16), w_out.astype(BF16)
    splits = [(0, nq, "q"), (nq, nkv, "f32"), (nq, nkv, "bf16"), (nq + nkv, nq, "f32")]
    q, kv, kvb, gate = _proj(xp.reshape(n * s, d), g_pre, w_in_b, splits, tm)
    o = _banded(q.reshape(n, s, nq), kvb.reshape(n, s, nkv), slopes, n_heads=n_heads, n_kv=n_kv,
                window=A_WINDOW, stride=1, sinks=sinks)
    yp = _outproj([o.reshape(n * s, nq)], [], gate, xp.reshape(n * s, d), g_post, w_out_b, tm)
    keep = min(A_WINDOW, s)
    state_p = _kv_state(kv.reshape(n, s, nkv)[:, s - keep:], n_kv)
    qs, kvs, _, gs = _proj(xs.reshape(m * t, d), g_pre, w_in_b, splits, tm)
    od, nbuf = _dec_window(_to_decode_rows(qs.reshape(m, t, nq), n_kv), kvs.reshape(m, t, nkv), buf, slopes,
                           window=A_WINDOW, dilation=1, past_len=past_len, sinks=sinks)
    ys = _outproj([_from_decode_rows(od, t)], [], gs, xs.reshape(m * t, d), g_post, w_out_b, tm)
    return yp.reshape(n, s, d), ys.reshape(m, t, d), [state_p], [nbuf]


def _mixer_b(xp, xs, cache_cmp, cache_sel, buf_win, page_table, g_pre, g_post, w_in, w_out, cmp_pool,
             past_len, tm):
    n, s, d = xp.shape
    m, t, _ = xs.shape
    n_heads = d // HEAD_DIM
    n_kv = n_heads // 4
    nq, nkv = n_heads * HEAD_DIM, n_kv * KV_LANES
    n_bg = 3 * n_heads
    slopes = _alibi_slopes(n_heads)
    w_in_b, w_out_b = w_in.astype(BF16), w_out.astype(BF16)
    pmats = _pool_mats(cmp_pool)
    c0, s0, w0, g0, b0 = nq, nq + nkv, nq + 2 * nkv, nq + 3 * nkv, 2 * nq + 3 * nkv
    base = [(0, nq, "q"), (c0, nkv, "f32"), (s0, nkv, "f32"), (w0, nkv, "f32"), (g0, nq, "f32"),
            (b0, n_bg, "f32")]
    splits = [(0, nq, "q"), (c0, nkv, "f32"), (c0, nkv, "f32t"), (s0, nkv, "f32t"), (w0, nkv, "f32"),
              (g0, nq, "f32"), (b0, n_bg, "f32"), (s0, nkv, "bf16"), (s0, nkv, "kaug"), (w0, nkv, "bf16")]
    q, c_p, c_pt, s_pt, w_p, gate, bgz, s_pb, kaug, w_pb = _proj(xp.reshape(n * s, d), g_pre, w_in_b, splits, tm,
                                                                 seq_len=s)
    q = q.reshape(n, s, nq)
    cmp_e, cmp_o = _pool_prompt(c_p.reshape(n, s, nkv), pmats)
    t_sel = min(256, s)
    o_c, selt, sel_any = _nsa_cmp(q, cmp_e, cmp_o, slopes, n_heads=n_heads, n_kv=n_kv, tq=t_sel)
    per_tile = t_sel // B_SEL
    flags = (sel_any[:, :, :, :, 0].reshape(n, n_kv, s // t_sel, s // t_sel, per_tile).max(axis=-1) > 0)
    flags = flags.astype(jnp.int32).reshape(-1)
    qt = q.reshape(n, s, n_heads, HEAD_DIM).transpose(0, 2, 3, 1)
    kvt = s_pb.reshape(n, s, n_kv, KV_LANES).transpose(0, 2, 3, 1)
    o_s = _nsa_sel(qt, selt, kaug.reshape(n, s, -1), kvt, flags, slopes, n_heads=n_heads, n_kv=n_kv, t=t_sel)
    o = _banded(q, w_pb.reshape(n, s, nkv), slopes, n_heads=n_heads, n_kv=n_kv, window=B_WINDOW, stride=1,
                merge=(o_c, o_s, bgz.reshape(n, s, n_bg)))
    yp = _outproj([o.reshape(n * s, nq)], [], gate, xp.reshape(n * s, d), g_post, w_out_b, tm)
    keep = min(B_WINDOW, s)
    from_t = lambda a: a.reshape(n, n_kv, 2, HEAD_DIM, s).transpose(0, 4, 1, 2, 3)
    states_p = [from_t(c_pt), from_t(s_pt), _kv_state(w_p.reshape(n, s, nkv)[:, s - keep:], n_kv)]
    qs, c_s, s_s, w_s, gs, bgzs = _proj(xs.reshape(m * t, d), g_pre, w_in_b, base, tm)
    qd = _to_decode_rows(qs.reshape(m, t, nq), n_kv)
    c_s, s_s, w_s = (a.reshape(m, t, nkv) for a in (c_s, s_s, w_s))
    cmpt = _pool_paged(_pages_view(cache_cmp), page_table, cmp_pool)
    n_w = min(NEW_PAD, B_CMP)
    pnew = jnp.zeros((2, NEW_PAD, NEW_PAD), F32).at[:, 0, :n_w].set(cmp_pool.astype(F32)[:, :n_w]).astype(BF16)
    oc_d, sel_d = _nsa_dec_cmp(qd, cmpt, c_s, pnew, slopes, past_len=past_len)
    os_d = _nsa_dec_sel(qd, sel_d, s_s, _pages_view(cache_sel), page_table, slopes, past_len=past_len)
    ow_d, nwin = _dec_window(qd, w_s, buf_win, slopes, window=B_WINDOW, dilation=1, past_len=past_len)
    osm = _nsa_merge(_from_decode_rows(oc_d, t), _from_decode_rows(os_d, t), _from_decode_rows(ow_d, t),
                     bgzs, n_heads)
    ys = _outproj([osm], [], gs, xs.reshape(m * t, d), g_post, w_out_b, tm)
    states_s = [_kv_state(c_s, n_kv), _kv_state(s_s, n_kv), nwin]
    return yp.reshape(n, s, d), ys.reshape(m, t, d), states_p, states_s


def _mixer_c(xp, xs, bufs, g_pre, g_post, w_in, w_out, past_len, tm):
    n, s, d = xp.shape
    m, t, _ = xs.shape
    n_heads = d // HEAD_DIM
    n_kv = n_heads // 8
    ng = len(C_GROUPS)
    nq1, nkv1 = n_heads * HEAD_DIM, n_kv * KV_LANES
    nq, nkv = ng * nq1, ng * nkv1
    slopes = _alibi_slopes(n_heads)
    w_in_b, w_out_b = w_in.astype(BF16), w_out.astype(BF16)
    splits = [(0, nq, "q"), (nq, nkv, "f32"), (nq, nkv, "bf16"), (nq + nkv, nq1, "f32")]
    view = lambda mode, dil: mode if dil == 1 else f"{mode}@{dil}"
    p_splits = ([(gi * nq1, nq1, view("q", dil)) for gi, (_, dil) in enumerate(C_GROUPS)]
                + [(nq + gi * nkv1, nkv1, view("bf16", dil)) for gi, (_, dil) in enumerate(C_GROUPS)]
                + [(nq, nkv, "f32"), (nq + nkv, nq1, "f32")])
    p_outs = _proj(xp.reshape(n * s, d), g_pre, w_in_b, p_splits, tm)
    q_views, kv_views, (kv, gate) = p_outs[:ng], p_outs[ng:2 * ng], p_outs[2 * ng:]
    kv = kv.reshape(n, s, nkv)
    outs, lses, states_p = [], [], []
    for gi, (w, dil) in enumerate(C_GROUPS):
        o, lse = _banded(q_views[gi].reshape(n, s // dil, dil * nq1), kv_views[gi].reshape(n, s // dil, dil * nkv1),
                         slopes, n_heads=n_heads, n_kv=n_kv, window=w, stride=dil, with_lse=True, viewed=True)
        outs.append(o.reshape(n * s, nq1))
        lses.append(lse.reshape(n * s, nq1))
        keep = min(w, s)
        states_p.append(_kv_state(kv[:, s - keep:, gi * nkv1:(gi + 1) * nkv1], n_kv))
    yp = _outproj(outs, lses, gate, xp.reshape(n * s, d), g_post, w_out_b, tm)
    qs, kvs, _, gs = _proj(xs.reshape(m * t, d), g_pre, w_in_b, splits, tm)
    qs, kvs = qs.reshape(m, t, nq), kvs.reshape(m, t, nkv)
    outs, lses, states_s = [], [], []
    for gi, (w, dil) in enumerate(C_GROUPS):
        buf = bufs[gi]
        assert buf.shape[1] == w
        od, ld, nbuf = _dec_window(_to_decode_rows(qs[:, :, gi * nq1:(gi + 1) * nq1], n_kv),
                                   kvs[:, :, gi * nkv1:(gi + 1) * nkv1], buf, slopes,
                                   window=w, dilation=dil, past_len=past_len, with_lse=True)
        outs.append(_from_decode_rows(od, t))
        lses.append(_from_decode_rows(ld, t))
        states_s.append(nbuf)
    ys = _outproj(outs, lses, gs, xs.reshape(m * t, d), g_post, w_out_b, tm)
    return yp.reshape(n, s, d), ys.reshape(m, t, d), states_p, states_s


def kernel(x_prompt, x_sample, state_a0_kv, cache_b1_cmp_kv, cache_b1_sel_kv, state_b1_win_kv,
           state_c2_g0_kv, state_c2_g1_kv, state_c2_g2_kv, state_a3_kv, page_table, norm_g,
           w_in_0, w_out_0, sinks_0, w_in_1, w_out_1, cmp_pool_1, w_in_2, w_out_2,
           w_in_3, w_out_3, sinks_3):
    past_len = page_table.shape[1] * cache_b1_cmp_kv.shape[1]
    tm = 256
    xp, xs = x_prompt, x_sample
    xp, xs, (a0_p,), (a0_s,) = _mixer_a(xp, xs, state_a0_kv, norm_g[0, 0], norm_g[0, 1], w_in_0, w_out_0,
                                        sinks_0, past_len, tm)
    xp, xs, (b1_cmp_p, b1_sel_p, b1_win_p), (b1_cmp_s, b1_sel_s, b1_win_s) = _mixer_b(
        xp, xs, cache_b1_cmp_kv, cache_b1_sel_kv, state_b1_win_kv, page_table, norm_g[1, 0], norm_g[1, 1],
        w_in_1, w_out_1, cmp_pool_1, past_len, tm)
    xp, xs, (c2_g0_p, c2_g1_p, c2_g2_p), (c2_g0_s, c2_g1_s, c2_g2_s) = _mixer_c(
        xp, xs, (state_c2_g0_kv, state_c2_g1_kv, state_c2_g2_kv), norm_g[2, 0], norm_g[2, 1], w_in_2, w_out_2,
        past_len, tm)
    xp, xs, (a3_p,), (a3_s,) = _mixer_a(xp, xs, state_a3_kv, norm_g[3, 0], norm_g[3, 1], w_in_3, w_out_3,
                                        sinks_3, past_len, tm)
    return (xp, xs, a0_p, a0_s, b1_cmp_p, b1_cmp_s, b1_sel_p, b1_sel_s, b1_win_p, b1_win_s,
            c2_g0_p, c2_g0_s, c2_g1_p, c2_g1_s, c2_g2_p, c2_g2_s, a3_p, a3_s)
```

```python
import functools

import jax
import jax.numpy as jnp
from jax import lax
from jax.experimental import pallas as pl
from jax.experimental.pallas import tpu as pltpu

F32 = jnp.float32
BF16 = jnp.bfloat16

HEAD_DIM = 64
KV_LANES = 2 * HEAD_DIM
RMS_EPS = 1e-6
NEG = -1e30
SCALE = HEAD_DIM ** -0.5

A_WINDOW = 128
B_CMP = 32
B_SEL = 64
B_TOPN = 16
B_WINDOW = 512
C_GROUPS = ((128, 1), (512, 4), (2048, 16))

VMEM_LIMIT_BYTES = 48 * 1024 * 1024


def _params(*sem):
    return pltpu.CompilerParams(dimension_semantics=sem, vmem_limit_bytes=VMEM_LIMIT_BYTES)


def _smem():
    return pl.BlockSpec(memory_space=pltpu.SMEM)


def _dot_nt(a, b):
    return lax.dot_general(a, b, (((1,), (1,)), ((), ())), preferred_element_type=F32)


def _dot(a, b):
    return jnp.dot(a, b, preferred_element_type=F32)


def _iota(shape, dim):
    return lax.broadcasted_iota(jnp.int32, shape, dim)


def _log2(d):
    assert d > 0 and d & (d - 1) == 0
    return d.bit_length() - 1


def _fdiv(x, d):
    return lax.shift_right_arithmetic(x, _log2(d))


def _fmod(x, d):
    return x & (d - 1)


def _sum(xs):
    return functools.reduce(lambda a, b: a + b, xs)


def _alibi_slopes(n_heads):
    h = jnp.arange(1, n_heads + 1, dtype=F32)
    return jnp.exp2(-8.0 * h / n_heads)


def _softmax_parts(s, valid, sink=None):
    s = jnp.where(valid, s, NEG)
    m = jnp.max(s, axis=-1, keepdims=True)
    if sink is not None:
        m = jnp.maximum(m, sink)
    e = jnp.where(valid, jnp.exp(s - m), 0.0)
    den = jnp.sum(e, axis=-1, keepdims=True)
    if sink is not None:
        den = den + jnp.exp(sink - m)
    den = jnp.where(den > 0, den, 1.0)
    return e / den, m, den


def _proj_body(x_ref, g_ref, w_ref, *out_refs, splits, chunk, tm, seq_len):
    x = x_ref[...]
    ms = jnp.mean(x * x, axis=-1, keepdims=True)
    h = ((x * lax.rsqrt(ms + RMS_EPS)) * g_ref[...]).astype(BF16)
    h_by_dilation = {}
    for o_ref, (start, width, mode) in zip(out_refs, splits):
        if "@" in mode:
            mode, dil = mode.split("@")
            dil = int(dil)
            per = tm // dil
            if dil not in h_by_dilation:
                dst = _iota((tm, 1), 0)
                src = _fmod(dst, per) * dil + _fdiv(dst, per)
                perm = (_iota((1, tm), 1) == src).astype(BF16)
                h_by_dilation[dil] = _dot(perm, h).astype(BF16)
            for c0 in range(0, width, chunk):
                cw = min(chunk, width - c0)
                z = _dot(h_by_dilation[dil], w_ref[:, start + c0:start + c0 + cw])
                if mode == "q":
                    z = z * SCALE
                z = z.astype(o_ref.dtype)
                for r in range(dil):
                    o_ref[:, r * width + c0:r * width + c0 + cw] = z[r * per:(r + 1) * per]
            continue
        if mode == "kaug":
            nbs = seq_len // B_SEL
            aug = KV_LANES + nbs
            pos = _fmod(pl.program_id(0) * tm + _iota((tm, 1), 0), seq_len)
            onehot = (_iota((1, nbs), 1) == _fdiv(pos, B_SEL)).astype(o_ref.dtype)
            is_k = _iota((1, KV_LANES), 1) < HEAD_DIM
            for j in range(width // KV_LANES):
                z = _dot(h, w_ref[:, start + j * KV_LANES:start + (j + 1) * KV_LANES])
                o_ref[:, j * aug:j * aug + KV_LANES] = jnp.where(is_k, z, 0.0).astype(o_ref.dtype)
                o_ref[:, j * aug + KV_LANES:(j + 1) * aug] = onehot
            continue
        for c0 in range(0, width, chunk):
            cw = min(chunk, width - c0)
            z = _dot(h, w_ref[:, start + c0:start + c0 + cw])
            if mode == "f32t":
                o_ref[0, c0:c0 + cw, :] = z.T
                continue
            if mode == "q":
                z = z * SCALE
            o_ref[:, c0:c0 + cw] = z.astype(o_ref.dtype)


_PROJ_DTYPES = {"f32": F32, "bf16": BF16, "q": BF16, "kaug": BF16, "f32t": F32}


def _proj(x2d, g, w_bf16, splits, tm, seq_len=None):
    m_rows, d = x2d.shape
    n = w_bf16.shape[1]
    tm = min(tm, m_rows)
    assert m_rows % tm == 0

    out_specs, out_shape = [], []
    for _, width, mode in splits:
        if "@" in mode:
            mode, dil = mode.split("@")
            dil = int(dil)
            assert tm % (dil * NEW_PAD) == 0
            out_specs.append(pl.BlockSpec((tm // dil, dil * width), lambda i: (i, 0)))
            out_shape.append(jax.ShapeDtypeStruct((m_rows // dil, dil * width), _PROJ_DTYPES[mode]))
            continue
        if mode == "f32t":
            per_seq = seq_len // tm
            assert seq_len % tm == 0
            out_specs.append(pl.BlockSpec((1, width, tm), lambda i: (i // per_seq, 0, i % per_seq)))
            out_shape.append(jax.ShapeDtypeStruct((m_rows // seq_len, width, seq_len), F32))
            continue
        if mode == "kaug":
            width = width // KV_LANES * (KV_LANES + seq_len // B_SEL)
        out_specs.append(pl.BlockSpec((tm, width), lambda i: (i, 0)))
        out_shape.append(jax.ShapeDtypeStruct((m_rows, width), _PROJ_DTYPES[mode]))
    return pl.pallas_call(
        functools.partial(_proj_body, splits=tuple(splits), chunk=512, tm=tm, seq_len=seq_len),
        grid=(m_rows // tm,),
        in_specs=[pl.BlockSpec((tm, d), lambda i: (i, 0)),
                  pl.BlockSpec((1, d), lambda i: (0, 0)),
                  pl.BlockSpec((d, n), lambda i: (0, 0))],
        out_specs=out_specs,
        out_shape=out_shape,
        compiler_params=_params("parallel"),
        name="proj",
    )(x2d, g.reshape(1, d), w_bf16)


def _outproj_body(*refs, n_branch):
    if n_branch == 1:
        o_ref, gate_ref, x_ref, g_ref, w_ref, y_ref = refs
        o = o_ref[...]
    else:
        o_refs = refs[:n_branch]
        l_refs = refs[n_branch:2 * n_branch]
        gate_ref, x_ref, g_ref, w_ref, y_ref = refs[2 * n_branch:]
        ls = [r[...] for r in l_refs]
        m = functools.reduce(jnp.maximum, ls)
        es = [jnp.exp(l - m) for l in ls]
        den = _sum(es)
        o = _sum([(e / den) * r[...] for e, r in zip(es, o_refs)])
    gate = gate_ref[...]
    a = (o * (gate * jax.nn.sigmoid(gate))).astype(BF16)
    y = _dot(a, w_ref[...])
    ms = jnp.mean(y * y, axis=-1, keepdims=True)
    y_ref[...] = x_ref[...] + (y * lax.rsqrt(ms + RMS_EPS)) * g_ref[...]


def _outproj(os_, lses, gate, x2d, g, w_bf16, tm):
    m_rows, d = x2d.shape
    k = w_bf16.shape[0]
    tm = min(tm, m_rows)
    assert m_rows % tm == 0
    row = lambda width: pl.BlockSpec((tm, width), lambda i: (i, 0))
    n_branch = len(os_)
    ins = list(os_) + list(lses) + [gate, x2d, g.reshape(1, d), w_bf16]
    in_specs = ([row(k)] * (n_branch + len(lses)) + [row(k), row(d),
                pl.BlockSpec((1, d), lambda i: (0, 0)), pl.BlockSpec((k, d), lambda i: (0, 0))])
    return pl.pallas_call(
        functools.partial(_outproj_body, n_branch=n_branch),
        grid=(m_rows // tm,),
        in_specs=in_specs,
        out_specs=row(d),
        out_shape=jax.ShapeDtypeStruct((m_rows, d), F32),
        compiler_params=_params("parallel"),
        name="outproj",
    )(*ins)


SUB_Q = 128


def _head_rows(ref, base, g_heads, rows_per_head):
    rows = g_heads * rows_per_head
    gi = _fdiv(_iota((rows, 1), 0), rows_per_head)
    out = jnp.zeros((rows, 1), F32)
    for g in range(g_heads):
        out = jnp.where(gi == g, ref[base + g], out)
    return out


def _banded_body(*refs, tq, wsub, stride, window, g_heads, has_sink, with_lse, nsa_merge):
    it = iter(refs)
    slopes_ref = next(it)
    sinks_ref = next(it) if has_sink else None
    q_ref = next(it)
    kv_ref = next(it)
    if nsa_merge:
        oc_ref, os_ref, bgz_ref = next(it), next(it), next(it)
    o_ref = next(it)
    lse_ref = next(it) if with_lse else None
    bias_ref = next(it)

    k = pl.program_id(2)
    i = pl.program_id(3)
    span = wsub + SUB_Q
    rows = g_heads * SUB_Q
    n_var = wsub // SUB_Q + 1

    @pl.when(i == 0)
    def _():
        r = _fmod(_iota((rows, 1), 0), SUB_Q)
        c = _iota((1, span), 1)
        slope = _head_rows(slopes_ref, k * g_heads, g_heads, SUB_Q)
        for v in range(n_var):
            dist = (wsub - SUB_Q * v + r - c) * stride
            ok = (dist >= 0) & (dist <= window)
            bias_ref[v] = jnp.where(ok, -(slope * dist.astype(F32)), NEG)

    sink = _head_rows(sinks_ref, k * g_heads, g_heads, SUB_Q) if has_sink else None
    for sb in range(tq // SUB_Q):
        r0 = sb * SUB_Q
        q0 = i * tq + r0
        kst = pl.multiple_of(jnp.maximum(q0 - wsub, 0), SUB_Q)
        var = _fdiv(jnp.maximum(wsub - q0, 0), SUB_Q)
        kv = kv_ref[0, pl.ds(kst, span), :]
        qs = jnp.concatenate([q_ref[0, r0:r0 + SUB_Q, g * HEAD_DIM:(g + 1) * HEAD_DIM]
                              for g in range(g_heads)], axis=0)
        s = _dot_nt(qs, kv[:, :HEAD_DIM]) + bias_ref[var]
        m = jnp.max(s, axis=-1, keepdims=True)
        if has_sink:
            m = jnp.maximum(m, sink)
        e = jnp.exp(s - m)
        den = jnp.sum(e, axis=-1, keepdims=True)
        if has_sink:
            den = den + jnp.exp(sink - m)
        o = _dot(e.astype(BF16), kv[:, HEAD_DIM:]) / den
        if with_lse:
            lse = m + jnp.log(den)
        if nsa_merge:
            bg = jax.nn.sigmoid(bgz_ref[0, 0, r0:r0 + SUB_Q, :])
        for g in range(g_heads):
            lo, hi = g * HEAD_DIM, (g + 1) * HEAD_DIM
            og = o[g * SUB_Q:(g + 1) * SUB_Q]
            if nsa_merge:
                col = lambda j: bg[:, j * g_heads + g:j * g_heads + g + 1]
                og = (col(0) * oc_ref[0, r0:r0 + SUB_Q, lo:hi] + col(1) * os_ref[0, r0:r0 + SUB_Q, lo:hi]
                      + col(2) * og)
            o_ref[0, r0:r0 + SUB_Q, lo:hi] = og
            if with_lse:
                lse_ref[0, r0:r0 + SUB_Q, lo:hi] = jnp.broadcast_to(lse[g * SUB_Q:(g + 1) * SUB_Q],
                                                                    (SUB_Q, HEAD_DIM))


def _banded(q, kv, slopes, *, n_heads, n_kv, window, stride, q_group=0, sinks=None,
            with_lse=False, merge=None, tq=512, viewed=False):
    g_heads = n_heads // n_kv
    gw = g_heads * HEAD_DIM
    if viewed:
        (n, sp, qc), kc = q.shape, kv.shape[2] // stride
        qc, s = qc // stride, sp * stride
        qv, kvv = q, kv
    else:
        n, s, qc = q.shape
        kc = kv.shape[2]
        sp = s // stride
        qv = q.reshape(n, sp, stride * qc)
        kvv = kv.reshape(n, sp, stride * kc)
    wsub = window // stride
    tq = min(tq, sp)
    assert sp % tq == 0 and wsub % SUB_Q == 0 and tq % SUB_Q == 0 and wsub + SUB_Q <= sp
    q_blocks, kv_blocks, o_blocks = qc // gw, kc // KV_LANES, n_kv
    q_map = lambda b, r, k, i: (b, i, r * q_blocks + q_group * n_kv + k)
    kv_map = lambda b, r, k, i: (b, 0, r * kv_blocks + q_group * n_kv + k)
    o_map = lambda b, r, k, i: (b, i, r * o_blocks + k)
    ins = [slopes]
    in_specs = [_smem()]
    if sinks is not None:
        ins.append(sinks)
        in_specs.append(_smem())
    ins += [qv, kvv]
    in_specs += [pl.BlockSpec((1, tq, gw), q_map), pl.BlockSpec((1, sp, KV_LANES), kv_map)]
    if merge is not None:
        o_c, o_s, bgz = merge
        bgz = bgz.reshape(n, s, 3, n_kv, g_heads).transpose(0, 3, 1, 2, 4).reshape(n, n_kv, s, 3 * g_heads)
        ins += [o_c, o_s, bgz]
        in_specs += [pl.BlockSpec((1, tq, gw), o_map), pl.BlockSpec((1, tq, gw), o_map),
                     pl.BlockSpec((1, 1, tq, 3 * g_heads), lambda b, r, k, i: (b, k, i, 0))]
    o_sds = jax.ShapeDtypeStruct((n, sp, stride * n_heads * HEAD_DIM), F32)
    o_spec = pl.BlockSpec((1, tq, gw), o_map)
    out = pl.pallas_call(
        functools.partial(_banded_body, tq=tq, wsub=wsub, stride=stride, window=window, g_heads=g_heads,
                          has_sink=sinks is not None, with_lse=with_lse, nsa_merge=merge is not None),
        grid=(n, stride, n_kv, sp // tq),
        in_specs=in_specs,
        out_specs=[o_spec, o_spec] if with_lse else o_spec,
        out_shape=[o_sds, o_sds] if with_lse else o_sds,
        scratch_shapes=[pltpu.VMEM((wsub // SUB_Q + 1, g_heads * SUB_Q, wsub + SUB_Q), F32)],
        compiler_params=_params("parallel", "parallel", "parallel", "arbitrary"),
        name="banded_attn",
    )(*ins)
    if with_lse:
        return tuple(a.reshape(n, s, n_heads * HEAD_DIM) for a in out)
    return out.reshape(n, s, n_heads * HEAD_DIM)


POOL_ROWS = 1024


def _pool_body(*refs, n_in, rows_in, paged):
    if paged:
        refs = refs[1:]
    p_ref = refs[0]
    in_refs = refs[1:1 + n_in]
    ce_ref, co_ref = refs[1 + n_in:]
    is_k = _fmod(_iota((1, ce_ref.shape[2]), 1), KV_LANES) < HEAD_DIM
    for parity, out_ref in ((0, ce_ref), (1, co_ref)):
        acc_k = None
        acc_v = None
        for j, r in enumerate(in_refs):
            rows = r[0].astype(BF16)
            pk = p_ref[parity, :, j * rows_in:(j + 1) * rows_in]
            pv = p_ref[2 + parity, :, j * rows_in:(j + 1) * rows_in]
            dk, dv = _dot(pk, rows), _dot(pv, rows)
            acc_k = dk if acc_k is None else acc_k + dk
            acc_v = dv if acc_v is None else acc_v + dv
        out_ref[0] = jnp.where(is_k, acc_k, acc_v)


def _pool_mats(pool_w):
    nb = POOL_ROWS // B_CMP
    eye = jnp.eye(nb, dtype=F32)
    mats = []
    for x in range(2):
        full = jnp.kron(eye, pool_w[x].astype(F32)[None, :])
        mats += [full[0::2], full[1::2]]
    return jnp.stack(mats).astype(BF16)


def _pool_prompt(rows, pmats):
    n, l, c = rows.shape
    assert l % POOL_ROWS == 0
    half = POOL_ROWS // B_CMP // 2
    sds = jax.ShapeDtypeStruct((n, l // B_CMP // 2, c), F32)
    out_spec = pl.BlockSpec((1, half, c), lambda b, j: (b, j, 0))
    return pl.pallas_call(
        functools.partial(_pool_body, n_in=1, rows_in=POOL_ROWS, paged=False),
        grid=(n, l // POOL_ROWS),
        in_specs=[pl.BlockSpec(pmats.shape, lambda b, j: (0, 0, 0)),
                  pl.BlockSpec((1, POOL_ROWS, c), lambda b, j: (b, j, 0))],
        out_specs=[out_spec, out_spec],
        out_shape=[sds, sds],
        compiler_params=_params("parallel", "parallel"),
        name="nsa_pool",
    )(pmats, rows)


def _pages_view(cache):
    return jnp.transpose(cache, (0, 2, 3, 4, 1))


def _pool_paged_body(pt_ref, wt_ref, *refs, per_step, n_kv, page):
    page_refs = refs[:per_step]
    out_ref = refs[per_step]
    j = pl.program_id(1)
    width = out_ref.shape[4]
    half = width // 2

    @pl.when(j == 0)
    def _():
        out_ref[...] = jnp.zeros(out_ref.shape, F32)

    r = _iota((page, 1), 0)
    col = _iota((1, width), 1)
    blk_in_page = _fdiv(r, B_CMP)
    per_page = page // B_CMP
    hits = []
    for u in range(per_step):
        pg = j * per_step + u
        target = jnp.where(_fmod(blk_in_page, 2) == 1, half, 0) + _fdiv(per_page * pg + blk_in_page, 2)
        hits.append(col == target)
    for x in range(2):
        pw = jnp.concatenate([jnp.where(hit, wt_ref[x], 0.0).astype(BF16) for hit in hits], axis=0)
        rows_t = jnp.concatenate([ref[0, :, x].reshape(n_kv * HEAD_DIM, page).astype(BF16)
                                  for ref in page_refs], axis=1)
        out_ref[0, :, x] += _dot(rows_t, pw).reshape(n_kv, HEAD_DIM, width)


def _pool_paged(cache_t, page_table, pool_w, per_step=16):
    _, n_kv, _, _, page = cache_t.shape
    n, n_pg = page_table.shape
    per_step = min(per_step, n_pg)
    assert n_pg % per_step == 0 and page % (2 * B_CMP) == 0
    width = n_pg * page // B_CMP
    wt = jnp.broadcast_to(jnp.tile(pool_w.astype(F32), (1, page // B_CMP))[:, :, None], (2, page, width))
    page_spec = lambda u: pl.BlockSpec((1, n_kv, 2, HEAD_DIM, page),
                                       lambda b, j, pt: (pt[b, j * per_step + u], 0, 0, 0, 0))
    return pl.pallas_call(
        functools.partial(_pool_paged_body, per_step=per_step, n_kv=n_kv, page=page),
        grid_spec=pltpu.PrefetchScalarGridSpec(
            num_scalar_prefetch=1,
            grid=(n, n_pg // per_step),
            in_specs=[pl.BlockSpec(wt.shape, lambda b, j, pt: (0, 0, 0))]
                     + [page_spec(u) for u in range(per_step)],
            out_specs=pl.BlockSpec((1, n_kv, 2, HEAD_DIM, width), lambda b, j, pt: (b, 0, 0, 0, 0))),
        out_shape=jax.ShapeDtypeStruct((n, n_kv, 2, HEAD_DIM, width), F32),
        compiler_params=_params("parallel", "arbitrary"),
        name="nsa_pool_paged",
    )(page_table, wt, *([cache_t] * per_step))


def _topk_mask(score, n_sel):
    rows, width = score.shape
    lane = _iota((rows, width), 1).astype(F32)
    sel = jnp.zeros((rows, width), F32)
    sc = score
    for _ in range(n_sel):
        mx = jnp.max(sc, axis=-1, keepdims=True)
        idx = jnp.min(jnp.where(sc == mx, lane, float(width)), axis=-1, keepdims=True)
        hit = lane == idx
        sel = jnp.where(hit, 1.0, sel)
        sc = jnp.where(hit, -4.0, sc)
    return sel


def _topk_mask_t(score_t, n_sel):
    width, cols = score_t.shape
    cand = _iota((width, cols), 0).astype(F32)
    sel = jnp.zeros((width, cols), F32)
    sc = score_t
    for _ in range(n_sel):
        mx = jnp.max(sc, axis=0, keepdims=True)
        idx = jnp.min(jnp.where(sc == mx, cand, float(width)), axis=0, keepdims=True)
        hit = cand == idx
        sel = jnp.where(hit, 1.0, sel)
        sc = jnp.where(hit, -4.0, sc)
    return sel


def _sel_score(imp, blk, q_pos, g_heads):
    cur = _fdiv(q_pos, B_SEL)
    forced = (blk == 0) | (blk == cur) | (blk == cur - 1)
    return jnp.where(blk * B_SEL <= q_pos, imp + jnp.where(forced, g_heads + 1.0, 0.0), -1.0)


def _nsa_cmp_body(slopes_ref, q_ref, ce_ref, co_ref, oc_ref, selt_ref, any_ref, *, tq, g_heads, n_sel):
    k = pl.program_id(1)
    i = pl.program_id(2)
    nbs = ce_ref.shape[1]
    q_pos = i * tq + _iota((tq, 1), 0)
    blk = _iota((1, nbs), 1)
    dist_e = q_pos - ((2 * blk + 1) * B_CMP - 1)
    dist_o = q_pos - ((2 * blk + 2) * B_CMP - 1)
    ok_e, ok_o = dist_e >= 0, dist_o >= 0
    de, do = dist_e.astype(F32), dist_o.astype(F32)
    ce = ce_ref[0].astype(BF16)
    co = co_ref[0].astype(BF16)
    ke, ve, ko, vo = ce[:, :HEAD_DIM], ce[:, HEAD_DIM:], co[:, :HEAD_DIM], co[:, HEAD_DIM:]
    imp_e = jnp.zeros((tq, nbs), F32)
    imp_o = jnp.zeros((tq, nbs), F32)
    for g in range(g_heads):
        slope = slopes_ref[k * g_heads + g]
        lo, hi = g * HEAD_DIM, (g + 1) * HEAD_DIM
        qg = q_ref[0, :, lo:hi]
        se = jnp.where(ok_e, _dot_nt(qg, ke) - slope * de, NEG)
        so = jnp.where(ok_o, _dot_nt(qg, ko) - slope * do, NEG)
        m = jnp.maximum(jnp.max(se, axis=-1, keepdims=True), jnp.max(so, axis=-1, keepdims=True))
        ee = jnp.where(ok_e, jnp.exp(se - m), 0.0)
        eo = jnp.where(ok_o, jnp.exp(so - m), 0.0)
        den = jnp.sum(ee, axis=-1, keepdims=True) + jnp.sum(eo, axis=-1, keepdims=True)
        den = jnp.where(den > 0, den, 1.0)
        pe, po = ee / den, eo / den
        oc_ref[0, :, lo:hi] = _dot(pe.astype(BF16), ve) + _dot(po.astype(BF16), vo)
        imp_e = imp_e + pe
        imp_o = imp_o + po
    score = _sel_score(imp_e + imp_o, blk, q_pos, g_heads)
    sel_t = _topk_mask_t(score.T, n_sel)
    selt_ref[0, 0] = sel_t.astype(BF16)
    any_ref[0, 0, 0] = jnp.broadcast_to(jnp.max(sel_t, axis=1, keepdims=True), (nbs, KV_LANES))


def _nsa_cmp(q, cmp_e, cmp_o, slopes, *, n_heads, n_kv, tq):
    n, s, _ = q.shape
    g_heads = n_heads // n_kv
    gw = g_heads * HEAD_DIM
    nbs = cmp_e.shape[1]
    n_sel = min(B_TOPN, nbs)
    assert s % tq == 0
    return pl.pallas_call(
        functools.partial(_nsa_cmp_body, tq=tq, g_heads=g_heads, n_sel=n_sel),
        grid=(n, n_kv, s // tq),
        in_specs=[_smem(),
                  pl.BlockSpec((1, tq, gw), lambda b, k, i: (b, i, k)),
                  pl.BlockSpec((1, nbs, KV_LANES), lambda b, k, i: (b, 0, k)),
                  pl.BlockSpec((1, nbs, KV_LANES), lambda b, k, i: (b, 0, k))],
        out_specs=[pl.BlockSpec((1, tq, gw), lambda b, k, i: (b, i, k)),
                   pl.BlockSpec((1, 1, nbs, tq), lambda b, k, i: (b, k, 0, i)),
                   pl.BlockSpec((1, 1, 1, nbs, KV_LANES), lambda b, k, i: (b, k, i, 0, 0))],
        out_shape=[jax.ShapeDtypeStruct((n, s, n_heads * HEAD_DIM), F32),
                   jax.ShapeDtypeStruct((n, n_kv, nbs, s), BF16),
                   jax.ShapeDtypeStruct((n, n_kv, s // tq, nbs, KV_LANES), F32)],
        compiler_params=_params("parallel", "parallel", "parallel"),
        name="nsa_cmp_select",
    )(slopes, q, cmp_e, cmp_o)


def _flash_update(s, valid, vb, m_ref, l_ref, acc_ref, idx, v_transposed=False):
    s = jnp.where(valid, s, NEG)
    m_prev = m_ref[idx]
    m_new = jnp.maximum(m_prev, jnp.max(s, axis=-1, keepdims=True))
    alpha = jnp.exp(m_prev - m_new)
    e = jnp.where(valid, jnp.exp(s - m_new), 0.0)
    l_ref[idx] = alpha * l_ref[idx] + jnp.sum(e, axis=-1, keepdims=True)
    pv = _dot_nt(e.astype(BF16), vb) if v_transposed else _dot(e.astype(BF16), vb)
    acc_ref[idx] = alpha * acc_ref[idx] + pv
    m_ref[idx] = m_new


def _flash_init(m_ref, l_ref, acc_ref):
    m_ref[...] = jnp.full(m_ref.shape, NEG, F32)
    l_ref[...] = jnp.zeros(l_ref.shape, F32)
    acc_ref[...] = jnp.zeros(acc_ref.shape, F32)


def _flash_result(l_ref, acc_ref, idx):
    l = l_ref[idx]
    return acc_ref[idx] / jnp.where(l > 0, l, 1.0)


MASK_BIG = 1e30
M_INIT = -3e38


def _nsa_sel_body(flags_ref, slopes_ref, qt_ref, selt_ref, kaug_ref, kvt_ref, o_ref,
                  qa_ref, bias_ref, m_ref, l_ref, acc_ref, *, t, g_heads, n_kv, n_q):
    b = pl.program_id(0)
    k = pl.program_id(1)
    i = pl.program_id(2)
    width = g_heads * t
    selbias = ((selt_ref[0, 0].astype(F32) - 1.0) * MASK_BIG).astype(BF16)
    r_minus_c = (_iota((1, t), 1) - _iota((t, 1), 0)).astype(F32)
    lane_head = _fdiv(_iota((1, width), 1), t)
    slope = jnp.zeros((1, width), F32)
    for g in range(g_heads):
        cols = slice(g * t, (g + 1) * t)
        qa_ref[0:HEAD_DIM, cols] = qt_ref[0, g]
        qa_ref[HEAD_DIM:KV_LANES, cols] = jnp.zeros((HEAD_DIM, t), BF16)
        qa_ref[KV_LANES:, cols] = selbias
        bias_ref[:, cols] = slopes_ref[k * g_heads + g] * r_minus_c
        slope = jnp.where(lane_head == g, slopes_ref[k * g_heads + g], slope)
    m_ref[...] = jnp.full(m_ref.shape, M_INIT, F32)
    l_ref[...] = jnp.zeros(l_ref.shape, F32)
    acc_ref[...] = jnp.zeros(acc_ref.shape, F32)

    def chunk(c, diagonal):
        k0 = pl.multiple_of(c * t, t)
        ka = kaug_ref[0, pl.ds(k0, t), :]
        kvt = kvt_ref[0, 0, :, pl.ds(k0, t)]
        s = _dot(ka, qa_ref[...]) - bias_ref[...]
        if diagonal:
            causal = jnp.where(_iota((t, 1), 0) > _iota((1, t), 1), NEG, 0.0)
            s = s + jnp.concatenate([causal] * g_heads, axis=1)
        delta = slope * jnp.full((1, width), (i - c) * t, jnp.int32).astype(F32)
        m_prev = m_ref[...]
        m_new = jnp.maximum(m_prev, jnp.max(s, axis=0, keepdims=True) - delta)
        alpha = jnp.exp(m_prev - m_new)
        e = jnp.exp(s - (m_new + delta))
        l_ref[...] = alpha * l_ref[...] + jnp.sum(e, axis=0, keepdims=True)
        acc_ref[...] = alpha * acc_ref[...] + _dot(kvt, e.astype(BF16))
        m_ref[...] = m_new

    def maybe_chunk(c, carry):
        @pl.when(flags_ref[((b * n_kv + k) * n_q + i) * n_q + c] != 0)
        def _():
            chunk(c, False)
        return carry

    lax.fori_loop(0, i, maybe_chunk, 0)
    chunk(i, True)
    l = l_ref[...]
    o_t = acc_ref[HEAD_DIM:KV_LANES, :] / jnp.where(l > 0, l, 1.0)
    for g in range(g_heads):
        o_ref[0, :, g * HEAD_DIM:(g + 1) * HEAD_DIM] = o_t[:, g * t:(g + 1) * t].T


def _nsa_sel(qt, selt, kaug, kvt, flags, slopes, *, n_heads, n_kv, t):
    n, _, _, s = qt.shape
    g_heads = n_heads // n_kv
    gw = g_heads * HEAD_DIM
    nbs = selt.shape[2]
    aug = KV_LANES + nbs
    n_q = s // t
    assert s % t == 0 and t % B_SEL == 0
    return pl.pallas_call(
        functools.partial(_nsa_sel_body, t=t, g_heads=g_heads, n_kv=n_kv, n_q=n_q),
        grid_spec=pltpu.PrefetchScalarGridSpec(
            num_scalar_prefetch=1,
            grid=(n, n_kv, n_q),
            in_specs=[_smem(),
                      pl.BlockSpec((1, g_heads, HEAD_DIM, t), lambda b, k, i, fl: (b, k, 0, i)),
                      pl.BlockSpec((1, 1, nbs, t), lambda b, k, i, fl: (b, k, 0, i)),
                      pl.BlockSpec((1, s, aug), lambda b, k, i, fl: (b, 0, k)),
                      pl.BlockSpec((1, 1, KV_LANES, s), lambda b, k, i, fl: (b, k, 0, 0))],
            out_specs=pl.BlockSpec((1, t, gw), lambda b, k, i, fl: (b, i, k)),
            scratch_shapes=[pltpu.VMEM((aug, g_heads * t), BF16), pltpu.VMEM((t, g_heads * t), F32),
                            pltpu.VMEM((1, g_heads * t), F32), pltpu.VMEM((1, g_heads * t), F32),
                            pltpu.VMEM((KV_LANES, g_heads * t), F32)]),
        out_shape=jax.ShapeDtypeStruct((n, s, n_heads * HEAD_DIM), F32),
        compiler_params=_params("parallel", "parallel", "parallel"),
        name="nsa_selected",
    )(flags, slopes, qt, selt, kaug, kvt)


T_PAD = 8
NEW_PAD = 16
KEY_PAD = 128


def _row_select(ref, base, g_heads, rows):
    gi = _fdiv(_iota((rows, 1), 0), T_PAD)
    out = jnp.zeros((rows, 1), F32)
    for g in range(g_heads):
        out = jnp.where(gi == g, ref[base + g], out)
    return out


def _pad_rows(x, lo, total):
    return jnp.pad(x, ((0, 0), (lo, total - lo - x.shape[1]), (0, 0)))


def _dec_window_body(*refs, lb, t_new, window, dilation, past_len, g_heads, n_kv, has_sink, with_lse):
    it = iter(refs)
    slopes_ref = next(it)
    sinks_ref = next(it) if has_sink else None
    q_ref, new_lo_ref, new_hi_ref, buf_ref = next(it), next(it), next(it), next(it)
    o_ref = next(it)
    lse_ref = next(it) if with_lse else None
    nbuf_ref = next(it)

    rows = g_heads * T_PAD
    span = lb + KEY_PAD
    t = _fmod(_iota((rows, 1), 0), T_PAD)
    r = _iota((1, span), 1)
    dist = lb + t - r
    valid = (dist >= 0) & (dist <= window) & (past_len - lb + r >= 0)
    if dilation > 1:
        valid = valid & (_fmod(dist, dilation) == 0)
    distf = dist.astype(F32)
    is_new = _iota((1, KEY_PAD), 1) >= KEY_PAD - t_new
    for k in range(n_kv):
        kt_buf, vt_buf = buf_ref[0, k, 0], buf_ref[0, k, 1]
        kt = jnp.concatenate([kt_buf, new_lo_ref[0, k, 0]], axis=1).astype(BF16)
        vt = jnp.concatenate([vt_buf, new_lo_ref[0, k, 1]], axis=1).astype(BF16)
        slope = _row_select(slopes_ref, k * g_heads, g_heads, rows)
        sink = _row_select(sinks_ref, k * g_heads, g_heads, rows) if has_sink else None
        s = _dot(q_ref[0, k], kt) - slope * distf
        p, m, den = _softmax_parts(s, valid, sink)
        o_ref[0, k] = _dot_nt(p.astype(BF16), vt)
        if with_lse:
            lse_ref[0, k] = jnp.broadcast_to(m + jnp.log(den), (rows, HEAD_DIM))
        for x, old in ((0, kt_buf), (1, vt_buf)):
            shifted = pltpu.roll(old, lb - t_new, axis=1)
            nbuf_ref[0, k, x] = shifted
            nbuf_ref[0, k, x, :, lb - KEY_PAD:lb] = jnp.where(is_new, new_hi_ref[0, k, x],
                                                              shifted[:, lb - KEY_PAD:lb])


def _state_view(buf):
    return jnp.transpose(buf, (0, 2, 3, 4, 1))


def _state_unview(buf_t):
    return jnp.transpose(buf_t, (0, 4, 1, 2, 3))


def _dec_window(qd, kv_new, buf, slopes, *, window, dilation, past_len, sinks=None, with_lse=False):
    n, n_kv, rows, _ = qd.shape
    g_heads = rows // T_PAD
    t_new = kv_new.shape[1]
    lb = buf.shape[1]
    assert lb % KEY_PAD == 0 and t_new <= T_PAD
    new_t = kv_new.reshape(n, t_new, n_kv, 2, HEAD_DIM).transpose(0, 2, 3, 4, 1)
    lane_pad = lambda lo: jnp.pad(new_t, ((0, 0),) * 4 + ((lo, KEY_PAD - t_new - lo),))
    new_lo, new_hi = lane_pad(0), lane_pad(KEY_PAD - t_new)
    buf_t = _state_view(buf)
    ins = [slopes]
    in_specs = [_smem()]
    if sinks is not None:
        ins.append(sinks)
        in_specs.append(_smem())
    ins += [qd, new_lo, new_hi, buf_t]
    o_spec = pl.BlockSpec((1, n_kv, rows, HEAD_DIM), lambda b: (b, 0, 0, 0))
    new_spec = pl.BlockSpec((1, n_kv, 2, HEAD_DIM, KEY_PAD), lambda b: (b, 0, 0, 0, 0))
    buf_spec = pl.BlockSpec((1, n_kv, 2, HEAD_DIM, lb), lambda b: (b, 0, 0, 0, 0))
    in_specs += [o_spec, new_spec, new_spec, buf_spec]
    o_sds = jax.ShapeDtypeStruct((n, n_kv, rows, HEAD_DIM), F32)
    out_specs = [o_spec] + ([o_spec] if with_lse else []) + [buf_spec]
    out_shape = [o_sds] + ([o_sds] if with_lse else []) + [jax.ShapeDtypeStruct(buf_t.shape, F32)]
    outs = pl.pallas_call(
        functools.partial(_dec_window_body, lb=lb, t_new=t_new, window=window, dilation=dilation,
                          past_len=past_len, g_heads=g_heads, n_kv=n_kv, has_sink=sinks is not None,
                          with_lse=with_lse),
        grid=(n,),
        in_specs=in_specs,
        out_specs=out_specs,
        out_shape=out_shape,
        compiler_params=_params("parallel"),
        name="decode_window",
    )(*ins)
    return tuple(outs[:-1]) + (_state_unview(outs[-1]),)


def _nsa_dec_cmp_body(slopes_ref, q_ref, cmpt_ref, cnew_ref, pnew_ref, oc_ref, sel_ref, *,
                      past_len, g_heads, n_kv, n_sel):
    rows = g_heads * T_PAD
    width = cmpt_ref.shape[4]
    nbs = width // 2
    t = _fmod(_iota((rows, 1), 0), T_PAD)
    q_pos = past_len + t
    lane = _iota((1, width), 1)
    cmp_blk = jnp.where(lane < nbs, 2 * lane, 2 * (lane - nbs) + 1)
    dist_c = q_pos - ((cmp_blk + 1) * B_CMP - 1)
    lane_n = _iota((1, NEW_PAD), 1)
    dist_ne = q_pos - ((2 * nbs + 1) * B_CMP - 1)
    dist_no = q_pos - ((2 * nbs + 2) * B_CMP - 1)
    cnew = cnew_ref[0].astype(BF16)
    is_k = _fmod(_iota((1, cnew.shape[1]), 1), KV_LANES) < HEAD_DIM
    new_e = jnp.where(is_k, _dot(pnew_ref[0], cnew), _dot(pnew_ref[1], cnew)).astype(BF16)
    new_o = jnp.zeros_like(new_e)
    new_sets = ((new_e, dist_ne, (dist_ne >= 0) & (lane_n == 0)),
                (new_o, dist_no, (dist_no >= 0) & (lane_n == 0)))
    ok_c = dist_c >= 0
    q_pos8 = past_len + _iota((T_PAD, 1), 0)
    blk = _iota((1, nbs), 1)
    lane_p = _iota((1, KEY_PAD), 1)
    scores = []
    for k in range(n_kv):
        c0 = k * KV_LANES
        slope = _row_select(slopes_ref, k * g_heads, g_heads, rows)
        q = q_ref[0, k]
        kt = cmpt_ref[0, k, 0].astype(BF16)
        vt = cmpt_ref[0, k, 1].astype(BF16)
        ss = [jnp.where(ok_c, _dot(q, kt) - slope * dist_c.astype(F32), NEG)]
        oks = [ok_c]
        for keys, dist, ok in new_sets:
            ss.append(jnp.where(ok, _dot_nt(q, keys[:, c0:c0 + HEAD_DIM]) - slope * dist.astype(F32), NEG))
            oks.append(ok)
        m = functools.reduce(jnp.maximum, [jnp.max(s, axis=-1, keepdims=True) for s in ss])
        es = [jnp.where(ok, jnp.exp(s - m), 0.0) for s, ok in zip(ss, oks)]
        den = _sum([jnp.sum(e, axis=-1, keepdims=True) for e in es])
        den = jnp.where(den > 0, den, 1.0)
        ps = [e / den for e in es]
        oc_ref[0, k] = (_dot_nt(ps[0].astype(BF16), vt)
                        + _sum([_dot(p.astype(BF16), keys[:, c0 + HEAD_DIM:c0 + KV_LANES])
                                for p, (keys, _, _) in zip(ps[1:], new_sets)]))
        gsum = lambda p: _sum([p[g * T_PAD:(g + 1) * T_PAD] for g in range(g_heads)])
        imp_c = gsum(ps[0])
        imp = imp_c[:, :nbs] + imp_c[:, nbs:]
        imp_new = gsum(ps[1]) + gsum(ps[2])
        score = _sel_score(imp, blk, q_pos8, g_heads)
        score_new = _sel_score(imp_new[:, 0:1], nbs + lane_p, q_pos8, g_heads)
        score_new = jnp.where(lane_p == 0, score_new, -3.0)
        scores.append(jnp.concatenate([score, score_new], axis=1))
    sel = _topk_mask_t(jnp.concatenate(scores, axis=0).T, n_sel).T
    for k in range(n_kv):
        sel_ref[0, k] = sel[k * T_PAD:(k + 1) * T_PAD]


def _nsa_dec_cmp(qd, cmpt, c_new, pnew, slopes, *, past_len):
    n, n_kv, rows, _ = qd.shape
    g_heads = rows // T_PAD
    width = cmpt.shape[4]
    nbs = width // 2
    c = c_new.shape[2]
    n_sel = min(B_TOPN, nbs + 1)
    cnew = _pad_rows(c_new, 0, NEW_PAD)
    q_spec = pl.BlockSpec((1, n_kv, rows, HEAD_DIM), lambda b: (b, 0, 0, 0))
    return pl.pallas_call(
        functools.partial(_nsa_dec_cmp_body, past_len=past_len, g_heads=g_heads, n_kv=n_kv, n_sel=n_sel),
        grid=(n,),
        in_specs=[_smem(), q_spec,
                  pl.BlockSpec((1, n_kv, 2, HEAD_DIM, width), lambda b: (b, 0, 0, 0, 0)),
                  pl.BlockSpec((1, NEW_PAD, c), lambda b: (b, 0, 0)),
                  pl.BlockSpec(pnew.shape, lambda b: (0, 0, 0))],
        out_specs=[q_spec, pl.BlockSpec((1, n_kv, T_PAD, nbs + KEY_PAD), lambda b: (b, 0, 0, 0))],
        out_shape=[jax.ShapeDtypeStruct((n, n_kv, rows, HEAD_DIM), F32),
                   jax.ShapeDtypeStruct((n, n_kv, T_PAD, nbs + KEY_PAD), F32)],
        compiler_params=_params("parallel"),
        name="nsa_decode_cmp_select",
    )(slopes, qd, cmpt, cnew, pnew)


def _nsa_dec_sel_body(*refs, past_len, g_heads, n_kv, per_step, page):
    logical_ref, count_ref, kv_need_ref = refs[1], refs[2], refs[3]
    slopes_ref, q_ref, sel_ref, snew_ref = refs[4:8]
    page_refs = refs[8:8 + per_step]
    o_ref, m_ref, l_ref, acc_ref = refs[8 + per_step:]
    rows = g_heads * T_PAD
    b = pl.program_id(0)
    j = pl.program_id(1)
    nsel = sel_ref.shape[3]
    t = _fmod(_iota((rows, 1), 0), T_PAD)
    q_pos = past_len + t

    @pl.when(j == 0)
    def _():
        _flash_init(m_ref, l_ref, acc_ref)

    @pl.when(j * per_step < count_ref[b])
    def _():
        lane = _iota((1, page), 1)
        k_pos = jnp.concatenate([logical_ref[b, j * per_step + u] * page + lane for u in range(per_step)], axis=1)
        live = jnp.concatenate([jnp.full((1, page), j * per_step + u, jnp.int32) for u in range(per_step)],
                               axis=1) < count_ref[b]
        expand = (_iota((nsel, 1), 0) == _fdiv(k_pos, B_SEL)).astype(BF16)
        distf = (q_pos - k_pos).astype(F32)
        for k in range(n_kv):
            @pl.when(kv_need_ref[(b * pl.num_programs(1) + j) * n_kv + k] != 0)
            def _(k=k):
                kt = jnp.concatenate([r[0, k, 0].astype(BF16) for r in page_refs], axis=1)
                vt = jnp.concatenate([r[0, k, 1].astype(BF16) for r in page_refs], axis=1)
                sel = jnp.concatenate([sel_ref[0, k]] * g_heads, axis=0).astype(BF16)
                valid = (_dot(sel, expand) > 0.5) & live & (k_pos <= q_pos)
                slope = _row_select(slopes_ref, k * g_heads, g_heads, rows)
                s = _dot(q_ref[0, k], kt) - slope * distf
                _flash_update(s, valid, vt, m_ref, l_ref, acc_ref, k, v_transposed=True)

    @pl.when(j == pl.num_programs(1) - 1)
    def _():
        n_past = past_len // B_SEL
        snew = snew_ref[0].astype(BF16)
        dist = t - _iota((1, NEW_PAD), 1)
        for k in range(n_kv):
            c0 = k * KV_LANES
            sel = jnp.concatenate([sel_ref[0, k]] * g_heads, axis=0)
            valid = (sel[:, n_past:n_past + 1] > 0.5) & (dist >= 0)
            slope = _row_select(slopes_ref, k * g_heads, g_heads, rows)
            s = _dot_nt(q_ref[0, k], snew[:, c0:c0 + HEAD_DIM]) - slope * dist.astype(F32)
            _flash_update(s, valid, snew[:, c0 + HEAD_DIM:c0 + KV_LANES], m_ref, l_ref, acc_ref, k)
            o_ref[0, k] = _flash_result(l_ref, acc_ref, k)


def _needed_pages(sel, page_table, t_new, page, per_step):
    n, n_pg = page_table.shape
    n_kv = sel.shape[1]
    per_page = page // B_SEL
    picked_kv = sel[:, :, :t_new, :n_pg * per_page].max(axis=2) > 0
    need_kv = picked_kv.reshape(n, n_kv, n_pg, per_page).any(axis=-1)
    need = need_kv.any(axis=1)
    count = need.sum(axis=1).astype(jnp.int32)
    order = jnp.argsort(jnp.logical_not(need), axis=1, stable=True).astype(jnp.int32)
    slot = jnp.minimum(jnp.arange(n_pg, dtype=jnp.int32)[None, :], jnp.maximum(count[:, None] - 1, 0))
    logical = jnp.take_along_axis(order, slot, axis=1)
    physical = jnp.take_along_axis(page_table, logical, axis=1)
    kv_sorted = jnp.take_along_axis(need_kv, jnp.broadcast_to(logical[:, None, :], need_kv.shape), axis=2)
    kv_need = kv_sorted.reshape(n, n_kv, n_pg // per_step, per_step).any(axis=-1)
    kv_need = kv_need.transpose(0, 2, 1).astype(jnp.int32).reshape(-1)
    return physical, logical, count, kv_need


def _nsa_dec_sel(qd, sel, s_new, cache_t, page_table, slopes, *, past_len, per_step=4):
    n, n_kv, rows, _ = qd.shape
    g_heads = rows // T_PAD
    page = cache_t.shape[4]
    t_new, c = s_new.shape[1:]
    n_pg = page_table.shape[1]
    per_step = min(per_step, n_pg)
    assert n_pg % per_step == 0 and past_len == n_pg * page
    snew = _pad_rows(s_new, 0, NEW_PAD)
    physical, logical, count, kv_need = _needed_pages(sel, page_table, t_new, page, per_step)
    q_spec = pl.BlockSpec((1, n_kv, rows, HEAD_DIM), lambda b, j, *_: (b, 0, 0, 0))
    page_spec = lambda u: pl.BlockSpec((1, n_kv, 2, HEAD_DIM, page),
                                       lambda b, j, phys, *_: (phys[b, j * per_step + u], 0, 0, 0, 0))
    return pl.pallas_call(
        functools.partial(_nsa_dec_sel_body, past_len=past_len, g_heads=g_heads, n_kv=n_kv,
                          per_step=per_step, page=page),
        grid_spec=pltpu.PrefetchScalarGridSpec(
            num_scalar_prefetch=4,
            grid=(n, n_pg // per_step),
            in_specs=[_smem(), q_spec,
                      pl.BlockSpec((1, n_kv, T_PAD, sel.shape[3]), lambda b, j, *_: (b, 0, 0, 0)),
                      pl.BlockSpec((1, NEW_PAD, c), lambda b, j, *_: (b, 0, 0))]
                     + [page_spec(u) for u in range(per_step)],
            out_specs=q_spec,
            scratch_shapes=[pltpu.VMEM((n_kv, rows, 1), F32), pltpu.VMEM((n_kv, rows, 1), F32),
                            pltpu.VMEM((n_kv, rows, HEAD_DIM), F32)]),
        out_shape=jax.ShapeDtypeStruct((n, n_kv, rows, HEAD_DIM), F32),
        compiler_params=_params("parallel", "arbitrary"),
        name="nsa_decode_selected",
    )(physical, logical, count, kv_need, slopes, qd, sel, snew, *([cache_t] * per_step))


def _nsa_merge_body(oc_ref, os_ref, ow_ref, bgz_ref, o_ref, *, n_heads):
    bg = jax.nn.sigmoid(bgz_ref[...])
    for h in range(n_heads):
        lo, hi = h * HEAD_DIM, (h + 1) * HEAD_DIM
        o_ref[:, lo:hi] = (bg[:, h:h + 1] * oc_ref[:, lo:hi]
                           + bg[:, n_heads + h:n_heads + h + 1] * os_ref[:, lo:hi]
                           + bg[:, 2 * n_heads + h:2 * n_heads + h + 1] * ow_ref[:, lo:hi])


def _nsa_merge(o_c, o_s, o_w, bgz, n_heads):
    m_rows, d = o_c.shape
    full = lambda a: pl.BlockSpec(a.shape, lambda i: (0, 0))
    return pl.pallas_call(
        functools.partial(_nsa_merge_body, n_heads=n_heads),
        grid=(1,),
        in_specs=[full(o_c), full(o_s), full(o_w), full(bgz)],
        out_specs=full(o_c),
        out_shape=jax.ShapeDtypeStruct((m_rows, d), F32),
        compiler_params=_params("arbitrary"),
        name="nsa_decode_merge",
    )(o_c, o_s, o_w, bgz)


def _to_decode_rows(q, n_kv):
    n, t, hd = q.shape
    g_heads = hd // HEAD_DIM // n_kv
    q = q.reshape(n, t, n_kv, g_heads, HEAD_DIM).transpose(0, 2, 3, 1, 4)
    q = jnp.pad(q, ((0, 0), (0, 0), (0, 0), (0, T_PAD - t), (0, 0)))
    return q.reshape(n, n_kv, g_heads * T_PAD, HEAD_DIM)


def _from_decode_rows(o, t):
    n, n_kv, rows, _ = o.shape
    g_heads = rows // T_PAD
    o = o.reshape(n, n_kv, g_heads, T_PAD, HEAD_DIM)[:, :, :, :t]
    return o.transpose(0, 3, 1, 2, 4).reshape(n * t, n_kv * g_heads * HEAD_DIM)


def _kv_state(rows, n_kv):
    return rows.reshape(*rows.shape[:-1], n_kv, 2, HEAD_DIM)


def _mixer_a(xp, xs, buf, g_pre, g_post, w_in, w_out, sinks, past_len, tm):
    n, s, d = xp.shape
    m, t, _ = xs.shape
    n_heads = d // HEAD_DIM
    n_kv = n_heads // 8
    nq, nkv = n_heads * HEAD_DIM, n_kv * KV_LANES
    slopes = _alibi_slopes(n_heads)
    w_in_b, w_out_b = w_in.astype(BF16), w_out.astype(BF16)
    splits = [(0, nq, "q"), (nq, nkv, "f32"), (nq, nkv, "bf16"), (nq + nkv, nq, "f32")]
    q, kv, kvb, gate = _proj(xp.reshape(n * s, d), g_pre, w_in_b, splits, tm)
    o = _banded(q.reshape(n, s, nq), kvb.reshape(n, s, nkv), slopes, n_heads=n_heads, n_kv=n_kv,
                window=A_WINDOW, stride=1, sinks=sinks)
    yp = _outproj([o.reshape(n * s, nq)], [], gate, xp.reshape(n * s, d), g_post, w_out_b, tm)
    keep = min(A_WINDOW, s)
    state_p = _kv_state(kv.reshape(n, s, nkv)[:, s - keep:], n_kv)
    qs, kvs, _, gs = _proj(xs.reshape(m * t, d), g_pre, w_in_b, splits, tm)
    od, nbuf = _dec_window(_to_decode_rows(qs.reshape(m, t, nq), n_kv), kvs.reshape(m, t, nkv), buf, slopes,
                           window=A_WINDOW, dilation=1, past_len=past_len, sinks=sinks)
    ys = _outproj([_from_decode_rows(od, t)], [], gs, xs.reshape(m * t, d), g_post, w_out_b, tm)
    return yp.reshape(n, s, d), ys.reshape(m, t, d), [state_p], [nbuf]


def _mixer_b(xp, xs, cache_cmp, cache_sel, buf_win, page_table, g_pre, g_post, w_in, w_out, cmp_pool,
             past_len, tm):
    n, s, d = xp.shape
    m, t, _ = xs.shape
    n_heads = d // HEAD_DIM
    n_kv = n_heads // 4
    nq, nkv = n_heads * HEAD_DIM, n_kv * KV_LANES
    n_bg = 3 * n_heads
    slopes = _alibi_slopes(n_heads)
    w_in_b, w_out_b = w_in.astype(BF16), w_out.astype(BF16)
    pmats = _pool_mats(cmp_pool)
    c0, s0, w0, g0, b0 = nq, nq + nkv, nq + 2 * nkv, nq + 3 * nkv, 2 * nq + 3 * nkv
    base = [(0, nq, "q"), (c0, nkv, "f32"), (s0, nkv, "f32"), (w0, nkv, "f32"), (g0, nq, "f32"),
            (b0, n_bg, "f32")]
    splits = [(0, nq, "q"), (c0, nkv, "f32"), (c0, nkv, "f32t"), (s0, nkv, "f32t"), (w0, nkv, "f32"),
              (g0, nq, "f32"), (b0, n_bg, "f32"), (s0, nkv, "bf16"), (s0, nkv, "kaug"), (w0, nkv, "bf16")]
    q, c_p, c_pt, s_pt, w_p, gate, bgz, s_pb, kaug, w_pb = _proj(xp.reshape(n * s, d), g_pre, w_in_b, splits, tm,
                                                                 seq_len=s)
    q = q.reshape(n, s, nq)
    cmp_e, cmp_o = _pool_prompt(c_p.reshape(n, s, nkv), pmats)
    t_sel = min(256, s)
    o_c, selt, sel_any = _nsa_cmp(q, cmp_e, cmp_o, slopes, n_heads=n_heads, n_kv=n_kv, tq=t_sel)
    per_tile = t_sel // B_SEL
    flags = (sel_any[:, :, :, :, 0].reshape(n, n_kv, s // t_sel, s // t_sel, per_tile).max(axis=-1) > 0)
    flags = flags.astype(jnp.int32).reshape(-1)
    qt = q.reshape(n, s, n_heads, HEAD_DIM).transpose(0, 2, 3, 1)
    kvt = s_pb.reshape(n, s, n_kv, KV_LANES).transpose(0, 2, 3, 1)
    o_s = _nsa_sel(qt, selt, kaug.reshape(n, s, -1), kvt, flags, slopes, n_heads=n_heads, n_kv=n_kv, t=t_sel)
    o = _banded(q, w_pb.reshape(n, s, nkv), slopes, n_heads=n_heads, n_kv=n_kv, window=B_WINDOW, stride=1,
                merge=(o_c, o_s, bgz.reshape(n, s, n_bg)))
    yp = _outproj([o.reshape(n * s, nq)], [], gate, xp.reshape(n * s, d), g_post, w_out_b, tm)
    keep = min(B_WINDOW, s)
    from_t = lambda a: a.reshape(n, n_kv, 2, HEAD_DIM, s).transpose(0, 4, 1, 2, 3)
    states_p = [from_t(c_pt), from_t(s_pt), _kv_state(w_p.reshape(n, s, nkv)[:, s - keep:], n_kv)]
    qs, c_s, s_s, w_s, gs, bgzs = _proj(xs.reshape(m * t, d), g_pre, w_in_b, base, tm)
    qd = _to_decode_rows(qs.reshape(m, t, nq), n_kv)
    c_s, s_s, w_s = (a.reshape(m, t, nkv) for a in (c_s, s_s, w_s))
    cmpt = _pool_paged(_pages_view(cache_cmp), page_table, cmp_pool)
    n_w = min(NEW_PAD, B_CMP)
    pnew = jnp.zeros((2, NEW_PAD, NEW_PAD), F32).at[:, 0, :n_w].set(cmp_pool.astype(F32)[:, :n_w]).astype(BF16)
    oc_d, sel_d = _nsa_dec_cmp(qd, cmpt, c_s, pnew, slopes, past_len=past_len)
    os_d = _nsa_dec_sel(qd, sel_d, s_s, _pages_view(cache_sel), page_table, slopes, past_len=past_len)
    ow_d, nwin = _dec_window(qd, w_s, buf_win, slopes, window=B_WINDOW, dilation=1, past_len=past_len)
    osm = _nsa_merge(_from_decode_rows(oc_d, t), _from_decode_rows(os_d, t), _from_decode_rows(ow_d, t),
                     bgzs, n_heads)
    ys = _outproj([osm], [], gs, xs.reshape(m * t, d), g_post, w_out_b, tm)
    states_s = [_kv_state(c_s, n_kv), _kv_state(s_s, n_kv), nwin]
    return yp.reshape(n, s, d), ys.reshape(m, t, d), states_p, states_s


def _mixer_c(xp, xs, bufs, g_pre, g_post, w_in, w_out, past_len, tm):
    n, s, d = xp.shape
    m, t, _ = xs.shape
    n_heads = d // HEAD_DIM
    n_kv = n_heads // 8
    ng = len(C_GROUPS)
    nq1, nkv1 = n_heads * HEAD_DIM, n_kv * KV_LANES
    nq, nkv = ng * nq1, ng * nkv1
    slopes = _alibi_slopes(n_heads)
    w_in_b, w_out_b = w_in.astype(BF16), w_out.astype(BF16)
    splits = [(0, nq, "q"), (nq, nkv, "f32"), (nq, nkv, "bf16"), (nq + nkv, nq1, "f32")]
    view = lambda mode, dil: mode if dil == 1 else f"{mode}@{dil}"
    p_splits = ([(gi * nq1, nq1, view("q", dil)) for gi, (_, dil) in enumerate(C_GROUPS)]
                + [(nq + gi * nkv1, nkv1, view("bf16", dil)) for gi, (_, dil) in enumerate(C_GROUPS)]
                + [(nq, nkv, "f32"), (nq + nkv, nq1, "f32")])
    p_outs = _proj(xp.reshape(n * s, d), g_pre, w_in_b, p_splits, tm)
    q_views, kv_views, (kv, gate) = p_outs[:ng], p_outs[ng:2 * ng], p_outs[2 * ng:]
    kv = kv.reshape(n, s, nkv)
    outs, lses, states_p = [], [], []
    for gi, (w, dil) in enumerate(C_GROUPS):
        o, lse = _banded(q_views[gi].reshape(n, s // dil, dil * nq1), kv_views[gi].reshape(n, s // dil, dil * nkv1),
                         slopes, n_heads=n_heads, n_kv=n_kv, window=w, stride=dil, with_lse=True, viewed=True)
        outs.append(o.reshape(n * s, nq1))
        lses.append(lse.reshape(n * s, nq1))
        keep = min(w, s)
        states_p.append(_kv_state(kv[:, s - keep:, gi * nkv1:(gi + 1) * nkv1], n_kv))
    yp = _outproj(outs, lses, gate, xp.reshape(n * s, d), g_post, w_out_b, tm)
    qs, kvs, _, gs = _proj(xs.reshape(m * t, d), g_pre, w_in_b, splits, tm)
    qs, kvs = qs.reshape(m, t, nq), kvs.reshape(m, t, nkv)
    outs, lses, states_s = [], [], []
    for gi, (w, dil) in enumerate(C_GROUPS):
        buf = bufs[gi]
        assert buf.shape[1] == w
        od, ld, nbuf = _dec_window(_to_decode_rows(qs[:, :, gi * nq1:(gi + 1) * nq1], n_kv),
                                   kvs[:, :, gi * nkv1:(gi + 1) * nkv1], buf, slopes,
                                   window=w, dilation=dil, past_len=past_len, with_lse=True)
        outs.append(_from_decode_rows(od, t))
        lses.append(_from_decode_rows(ld, t))
        states_s.append(nbuf)
    ys = _outproj(outs, lses, gs, xs.reshape(m * t, d), g_post, w_out_b, tm)
    return yp.reshape(n, s, d), ys.reshape(m, t, d), states_p, states_s


def kernel(x_prompt, x_sample, state_a0_kv, cache_b1_cmp_kv, cache_b1_sel_kv, state_b1_win_kv,
           state_c2_g0_kv, state_c2_g1_kv, state_c2_g2_kv, state_a3_kv, page_table, norm_g,
           w_in_0, w_out_0, sinks_0, w_in_1, w_out_1, cmp_pool_1, w_in_2, w_out_2,
           w_in_3, w_out_3, sinks_3):
    past_len = page_table.shape[1] * cache_b1_cmp_kv.shape[1]
    tm = 256
    xp, xs = x_prompt, x_sample
    xp, xs, (a0_p,), (a0_s,) = _mixer_a(xp, xs, state_a0_kv, norm_g[0, 0], norm_g[0, 1], w_in_0, w_out_0,
                                        sinks_0, past_len, tm)
    xp, xs, (b1_cmp_p, b1_sel_p, b1_win_p), (b1_cmp_s, b1_sel_s, b1_win_s) = _mixer_b(
        xp, xs, cache_b1_cmp_kv, cache_b1_sel_kv, state_b1_win_kv, page_table, norm_g[1, 0], norm_g[1, 1],
        w_in_1, w_out_1, cmp_pool_1, past_len, tm)
    xp, xs, (c2_g0_p, c2_g1_p, c2_g2_p), (c2_g0_s, c2_g1_s, c2_g2_s) = _mixer_c(
        xp, xs, (state_c2_g0_kv, state_c2_g1_kv, state_c2_g2_kv), norm_g[2, 0], norm_g[2, 1], w_in_2, w_out_2,
        past_len, tm)
    xp, xs, (a3_p,), (a3_s,) = _mixer_a(xp, xs, state_a3_kv, norm_g[3, 0], norm_g[3, 1], w_in_3, w_out_3,
                                        sinks_3, past_len, tm)
    return (xp, xs, a0_p, a0_s, b1_cmp_p, b1_cmp_s, b1_sel_p, b1_sel_s, b1_win_p, b1_win_s,
            c2_g0_p, c2_g0_s, c2_g1_p, c2_g1_s, c2_g2_p, c2_g2_s, a3_p, a3_s)
```

```python
import functools

import jax
import jax.numpy as jnp
from jax import lax
from jax.experimental import pallas as pl
from jax.experimental.pallas import tpu as pltpu

F32 = jnp.float32
BF16 = jnp.bfloat16

HEAD_DIM = 64
KV_LANES = 2 * HEAD_DIM
RMS_EPS = 1e-6
NEG = -1e30
SCALE = HEAD_DIM ** -0.5

A_WINDOW = 128
B_CMP = 32
B_SEL = 64
B_TOPN = 16
B_WINDOW = 512
C_GROUPS = ((128, 1), (512, 4), (2048, 16))

VMEM_LIMIT_BYTES = 48 * 1024 * 1024


def _params(*sem):
    return pltpu.CompilerParams(dimension_semantics=sem, vmem_limit_bytes=VMEM_LIMIT_BYTES)


def _smem():
    return pl.BlockSpec(memory_space=pltpu.SMEM)


def _dot_nt(a, b):
    return lax.dot_general(a, b, (((1,), (1,)), ((), ())), preferred_element_type=F32)


def _dot(a, b):
    return jnp.dot(a, b, preferred_element_type=F32)


def _iota(shape, dim):
    return lax.broadcasted_iota(jnp.int32, shape, dim)


def _log2(d):
    assert d > 0 and d & (d - 1) == 0
    return d.bit_length() - 1


def _fdiv(x, d):
    return lax.shift_right_arithmetic(x, _log2(d))


def _fmod(x, d):
    return x & (d - 1)


def _sum(xs):
    return functools.reduce(lambda a, b: a + b, xs)


def _alibi_slopes(n_heads):
    h = jnp.arange(1, n_heads + 1, dtype=F32)
    return jnp.exp2(-8.0 * h / n_heads)


def _softmax_parts(s, valid, sink=None):
    s = jnp.where(valid, s, NEG)
    m = jnp.max(s, axis=-1, keepdims=True)
    if sink is not None:
        m = jnp.maximum(m, sink)
    e = jnp.where(valid, jnp.exp(s - m), 0.0)
    den = jnp.sum(e, axis=-1, keepdims=True)
    if sink is not None:
        den = den + jnp.exp(sink - m)
    den = jnp.where(den > 0, den, 1.0)
    return e / den, m, den


def _proj_body(x_ref, g_ref, w_ref, *out_refs, splits, chunk, tm, seq_len):
    x = x_ref[...]
    ms = jnp.mean(x * x, axis=-1, keepdims=True)
    h = ((x * lax.rsqrt(ms + RMS_EPS)) * g_ref[...]).astype(BF16)
    h_by_dilation = {}
    for o_ref, (start, width, mode) in zip(out_refs, splits):
        if "@" in mode:
            mode, dil = mode.split("@")
            dil = int(dil)
            per = tm // dil
            if dil not in h_by_dilation:
                dst = _iota((tm, 1), 0)
                src = _fmod(dst, per) * dil + _fdiv(dst, per)
                perm = (_iota((1, tm), 1) == src).astype(BF16)
                h_by_dilation[dil] = _dot(perm, h).astype(BF16)
            for c0 in range(0, width, chunk):
                cw = min(chunk, width - c0)
                z = _dot(h_by_dilation[dil], w_ref[:, start + c0:start + c0 + cw])
                if mode == "q":
                    z = z * SCALE
                z = z.astype(o_ref.dtype)
                for r in range(dil):
                    o_ref[:, r * width + c0:r * width + c0 + cw] = z[r * per:(r + 1) * per]
            continue
        if mode == "kaug":
            nbs = seq_len // B_SEL
            aug = KV_LANES + nbs
            pos = _fmod(pl.program_id(0) * tm + _iota((tm, 1), 0), seq_len)
            onehot = (_iota((1, nbs), 1) == _fdiv(pos, B_SEL)).astype(o_ref.dtype)
            is_k = _iota((1, KV_LANES), 1) < HEAD_DIM
            for j in range(width // KV_LANES):
                z = _dot(h, w_ref[:, start + j * KV_LANES:start + (j + 1) * KV_LANES])
                o_ref[:, j * aug:j * aug + KV_LANES] = jnp.where(is_k, z, 0.0).astype(o_ref.dtype)
                o_ref[:, j * aug + KV_LANES:(j + 1) * aug] = onehot
            continue
        for c0 in range(0, width, chunk):
            cw = min(chunk, width - c0)
            z = _dot(h, w_ref[:, start + c0:start + c0 + cw])
            if mode == "f32t":
                o_ref[0, c0:c0 + cw, :] = z.T
                continue
            if mode == "q":
                z = z * SCALE
            o_ref[:, c0:c0 + cw] = z.astype(o_ref.dtype)


_PROJ_DTYPES = {"f32": F32, "bf16": BF16, "q": BF16, "kaug": BF16, "f32t": F32}


def _proj(x2d, g, w_bf16, splits, tm, seq_len=None):
    m_rows, d = x2d.shape
    n = w_bf16.shape[1]
    tm = min(tm, m_rows)
    assert m_rows % tm == 0

    out_specs, out_shape = [], []
    for _, width, mode in splits:
        if "@" in mode:
            mode, dil = mode.split("@")
            dil = int(dil)
            assert tm % (dil * NEW_PAD) == 0
            out_specs.append(pl.BlockSpec((tm // dil, dil * width), lambda i: (i, 0)))
            out_shape.append(jax.ShapeDtypeStruct((m_rows // dil, dil * width), _PROJ_DTYPES[mode]))
            continue
        if mode == "f32t":
            per_seq = seq_len // tm
            assert seq_len % tm == 0
            out_specs.append(pl.BlockSpec((1, width, tm), lambda i: (i // per_seq, 0, i % per_seq)))
            out_shape.append(jax.ShapeDtypeStruct((m_rows // seq_len, width, seq_len), F32))
            continue
        if mode == "kaug":
            width = width // KV_LANES * (KV_LANES + seq_len // B_SEL)
        out_specs.append(pl.BlockSpec((tm, width), lambda i: (i, 0)))
        out_shape.append(jax.ShapeDtypeStruct((m_rows, width), _PROJ_DTYPES[mode]))
    return pl.pallas_call(
        functools.partial(_proj_body, splits=tuple(splits), chunk=512, tm=tm, seq_len=seq_len),
        grid=(m_rows // tm,),
        in_specs=[pl.BlockSpec((tm, d), lambda i: (i, 0)),
                  pl.BlockSpec((1, d), lambda i: (0, 0)),
                  pl.BlockSpec((d, n), lambda i: (0, 0))],
        out_specs=out_specs,
        out_shape=out_shape,
        compiler_params=_params("parallel"),
        name="proj",
    )(x2d, g.reshape(1, d), w_bf16)


def _outproj_body(*refs, n_branch):
    if n_branch == 1:
        o_ref, gate_ref, x_ref, g_ref, w_ref, y_ref = refs
        o = o_ref[...]
    else:
        o_refs = refs[:n_branch]
        l_refs = refs[n_branch:2 * n_branch]
        gate_ref, x_ref, g_ref, w_ref, y_ref = refs[2 * n_branch:]
        ls = [r[...] for r in l_refs]
        m = functools.reduce(jnp.maximum, ls)
        es = [jnp.exp(l - m) for l in ls]
        den = _sum(es)
        o = _sum([(e / den) * r[...] for e, r in zip(es, o_refs)])
    gate = gate_ref[...]
    a = (o * (gate * jax.nn.sigmoid(gate))).astype(BF16)
    y = _dot(a, w_ref[...])
    ms = jnp.mean(y * y, axis=-1, keepdims=True)
    y_ref[...] = x_ref[...] + (y * lax.rsqrt(ms + RMS_EPS)) * g_ref[...]


def _outproj(os_, lses, gate, x2d, g, w_bf16, tm):
    m_rows, d = x2d.shape
    k = w_bf16.shape[0]
    tm = min(tm, m_rows)
    assert m_rows % tm == 0
    row = lambda width: pl.BlockSpec((tm, width), lambda i: (i, 0))
    n_branch = len(os_)
    ins = list(os_) + list(lses) + [gate, x2d, g.reshape(1, d), w_bf16]
    in_specs = ([row(k)] * (n_branch + len(lses)) + [row(k), row(d),
                pl.BlockSpec((1, d), lambda i: (0, 0)), pl.BlockSpec((k, d), lambda i: (0, 0))])
    return pl.pallas_call(
        functools.partial(_outproj_body, n_branch=n_branch),
        grid=(m_rows // tm,),
        in_specs=in_specs,
        out_specs=row(d),
        out_shape=jax.ShapeDtypeStruct((m_rows, d), F32),
        compiler_params=_params("parallel"),
        name="outproj",
    )(*ins)


SUB_Q = 128


def _head_rows(ref, base, g_heads, rows_per_head):
    rows = g_heads * rows_per_head
    gi = _fdiv(_iota((rows, 1), 0), rows_per_head)
    out = jnp.zeros((rows, 1), F32)
    for g in range(g_heads):
        out = jnp.where(gi == g, ref[base + g], out)
    return out


def _banded_body(*refs, tq, wsub, stride, window, g_heads, has_sink, with_lse, nsa_merge):
    it = iter(refs)
    slopes_ref = next(it)
    sinks_ref = next(it) if has_sink else None
    q_ref = next(it)
    kv_ref = next(it)
    if nsa_merge:
        oc_ref, os_ref, bgz_ref = next(it), next(it), next(it)
    o_ref = next(it)
    lse_ref = next(it) if with_lse else None
    bias_ref = next(it)

    k = pl.program_id(2)
    i = pl.program_id(3)
    span = wsub + SUB_Q
    rows = g_heads * SUB_Q
    n_var = wsub // SUB_Q + 1

    @pl.when(i == 0)
    def _():
        r = _fmod(_iota((rows, 1), 0), SUB_Q)
        c = _iota((1, span), 1)
        slope = _head_rows(slopes_ref, k * g_heads, g_heads, SUB_Q)
        for v in range(n_var):
            dist = (wsub - SUB_Q * v + r - c) * stride
            ok = (dist >= 0) & (dist <= window)
            bias_ref[v] = jnp.where(ok, -(slope * dist.astype(F32)), NEG)

    sink = _head_rows(sinks_ref, k * g_heads, g_heads, SUB_Q) if has_sink else None
    for sb in range(tq // SUB_Q):
        r0 = sb * SUB_Q
        q0 = i * tq + r0
        kst = pl.multiple_of(jnp.maximum(q0 - wsub, 0), SUB_Q)
        var = _fdiv(jnp.maximum(wsub - q0, 0), SUB_Q)
        kv = kv_ref[0, pl.ds(kst, span), :]
        qs = jnp.concatenate([q_ref[0, r0:r0 + SUB_Q, g * HEAD_DIM:(g + 1) * HEAD_DIM]
                              for g in range(g_heads)], axis=0)
        s = _dot_nt(qs, kv[:, :HEAD_DIM]) + bias_ref[var]
        m = jnp.max(s, axis=-1, keepdims=True)
        if has_sink:
            m = jnp.maximum(m, sink)
        e = jnp.exp(s - m)
        den = jnp.sum(e, axis=-1, keepdims=True)
        if has_sink:
            den = den + jnp.exp(sink - m)
        o = _dot(e.astype(BF16), kv[:, HEAD_DIM:]) / den
        if with_lse:
            lse = m + jnp.log(den)
        if nsa_merge:
            bg = jax.nn.sigmoid(bgz_ref[0, 0, r0:r0 + SUB_Q, :])
        for g in range(g_heads):
            lo, hi = g * HEAD_DIM, (g + 1) * HEAD_DIM
            og = o[g * SUB_Q:(g + 1) * SUB_Q]
            if nsa_merge:
                col = lambda j: bg[:, j * g_heads + g:j * g_heads + g + 1]
                og = (col(0) * oc_ref[0, r0:r0 + SUB_Q, lo:hi] + col(1) * os_ref[0, r0:r0 + SUB_Q, lo:hi]
                      + col(2) * og)
            o_ref[0, r0:r0 + SUB_Q, lo:hi] = og
            if with_lse:
                lse_ref[0, r0:r0 + SUB_Q, lo:hi] = jnp.broadcast_to(lse[g * SUB_Q:(g + 1) * SUB_Q],
                                                                    (SUB_Q, HEAD_DIM))


def _banded(q, kv, slopes, *, n_heads, n_kv, window, stride, q_group=0, sinks=None,
            with_lse=False, merge=None, tq=512, viewed=False):
    g_heads = n_heads // n_kv
    gw = g_heads * HEAD_DIM
    if viewed:
        (n, sp, qc), kc = q.shape, kv.shape[2] // stride
        qc, s = qc // stride, sp * stride
        qv, kvv = q, kv
    else:
        n, s, qc = q.shape
        kc = kv.shape[2]
        sp = s // stride
        qv = q.reshape(n, sp, stride * qc)
        kvv = kv.reshape(n, sp, stride * kc)
    wsub = window // stride
    tq = min(tq, sp)
    assert sp % tq == 0 and wsub % SUB_Q == 0 and tq % SUB_Q == 0 and wsub + SUB_Q <= sp
    q_blocks, kv_blocks, o_blocks = qc // gw, kc // KV_LANES, n_kv
    q_map = lambda b, r, k, i: (b, i, r * q_blocks + q_group * n_kv + k)
    kv_map = lambda b, r, k, i: (b, 0, r * kv_blocks + q_group * n_kv + k)
    o_map = lambda b, r, k, i: (b, i, r * o_blocks + k)
    ins = [slopes]
    in_specs = [_smem()]
    if sinks is not None:
        ins.append(sinks)
        in_specs.append(_smem())
    ins += [qv, kvv]
    in_specs += [pl.BlockSpec((1, tq, gw), q_map), pl.BlockSpec((1, sp, KV_LANES), kv_map)]
    if merge is not None:
        o_c, o_s, bgz = merge
        bgz = bgz.reshape(n, s, 3, n_kv, g_heads).transpose(0, 3, 1, 2, 4).reshape(n, n_kv, s, 3 * g_heads)
        ins += [o_c, o_s, bgz]
        in_specs += [pl.BlockSpec((1, tq, gw), o_map), pl.BlockSpec((1, tq, gw), o_map),
                     pl.BlockSpec((1, 1, tq, 3 * g_heads), lambda b, r, k, i: (b, k, i, 0))]
    o_sds = jax.ShapeDtypeStruct((n, sp, stride * n_heads * HEAD_DIM), F32)
    o_spec = pl.BlockSpec((1, tq, gw), o_map)
    out = pl.pallas_call(
        functools.partial(_banded_body, tq=tq, wsub=wsub, stride=stride, window=window, g_heads=g_heads,
                          has_sink=sinks is not None, with_lse=with_lse, nsa_merge=merge is not None),
        grid=(n, stride, n_kv, sp // tq),
        in_specs=in_specs,
        out_specs=[o_spec, o_spec] if with_lse else o_spec,
        out_shape=[o_sds, o_sds] if with_lse else o_sds,
        scratch_shapes=[pltpu.VMEM((wsub // SUB_Q + 1, g_heads * SUB_Q, wsub + SUB_Q), F32)],
        compiler_params=_params("parallel", "parallel", "parallel", "arbitrary"),
        name="banded_attn",
    )(*ins)
    if with_lse:
        return tuple(a.reshape(n, s, n_heads * HEAD_DIM) for a in out)
    return out.reshape(n, s, n_heads * HEAD_DIM)


POOL_ROWS = 1024


def _pool_body(*refs, n_in, rows_in, paged):
    if paged:
        refs = refs[1:]
    p_ref = refs[0]
    in_refs = refs[1:1 + n_in]
    ce_ref, co_ref = refs[1 + n_in:]
    is_k = _fmod(_iota((1, ce_ref.shape[2]), 1), KV_LANES) < HEAD_DIM
    for parity, out_ref in ((0, ce_ref), (1, co_ref)):
        acc_k = None
        acc_v = None
        for j, r in enumerate(in_refs):
            rows = r[0].astype(BF16)
            pk = p_ref[parity, :, j * rows_in:(j + 1) * rows_in]
            pv = p_ref[2 + parity, :, j * rows_in:(j + 1) * rows_in]
            dk, dv = _dot(pk, rows), _dot(pv, rows)
            acc_k = dk if acc_k is None else acc_k + dk
            acc_v = dv if acc_v is None else acc_v + dv
        out_ref[0] = jnp.where(is_k, acc_k, acc_v)


def _pool_mats(pool_w):
    nb = POOL_ROWS // B_CMP
    eye = jnp.eye(nb, dtype=F32)
    mats = []
    for x in range(2):
        full = jnp.kron(eye, pool_w[x].astype(F32)[None, :])
        mats += [full[0::2], full[1::2]]
    return jnp.stack(mats).astype(BF16)


def _pool_prompt(rows, pmats):
    n, l, c = rows.shape
    assert l % POOL_ROWS == 0
    half = POOL_ROWS // B_CMP // 2
    sds = jax.ShapeDtypeStruct((n, l // B_CMP // 2, c), F32)
    out_spec = pl.BlockSpec((1, half, c), lambda b, j: (b, j, 0))
    return pl.pallas_call(
        functools.partial(_pool_body, n_in=1, rows_in=POOL_ROWS, paged=False),
        grid=(n, l // POOL_ROWS),
        in_specs=[pl.BlockSpec(pmats.shape, lambda b, j: (0, 0, 0)),
                  pl.BlockSpec((1, POOL_ROWS, c), lambda b, j: (b, j, 0))],
        out_specs=[out_spec, out_spec],
        out_shape=[sds, sds],
        compiler_params=_params("parallel", "parallel"),
        name="nsa_pool",
    )(pmats, rows)


def _pages_view(cache):
    return jnp.transpose(cache, (0, 2, 3, 4, 1))


def _pool_paged_body(pt_ref, wt_ref, *refs, per_step, n_kv, page):
    page_refs = refs[:per_step]
    out_ref = refs[per_step]
    j = pl.program_id(1)
    width = out_ref.shape[4]
    half = width // 2

    @pl.when(j == 0)
    def _():
        out_ref[...] = jnp.zeros(out_ref.shape, F32)

    r = _iota((page, 1), 0)
    col = _iota((1, width), 1)
    blk_in_page = _fdiv(r, B_CMP)
    per_page = page // B_CMP
    hits = []
    for u in range(per_step):
        pg = j * per_step + u
        target = jnp.where(_fmod(blk_in_page, 2) == 1, half, 0) + _fdiv(per_page * pg + blk_in_page, 2)
        hits.append(col == target)
    for x in range(2):
        pw = jnp.concatenate([jnp.where(hit, wt_ref[x], 0.0).astype(BF16) for hit in hits], axis=0)
        rows_t = jnp.concatenate([ref[0, :, x].reshape(n_kv * HEAD_DIM, page).astype(BF16)
                                  for ref in page_refs], axis=1)
        out_ref[0, :, x] += _dot(rows_t, pw).reshape(n_kv, HEAD_DIM, width)


def _pool_paged(cache_t, page_table, pool_w, per_step=16):
    _, n_kv, _, _, page = cache_t.shape
    n, n_pg = page_table.shape
    per_step = min(per_step, n_pg)
    assert n_pg % per_step == 0 and page % (2 * B_CMP) == 0
    width = n_pg * page // B_CMP
    wt = jnp.broadcast_to(jnp.tile(pool_w.astype(F32), (1, page // B_CMP))[:, :, None], (2, page, width))
    page_spec = lambda u: pl.BlockSpec((1, n_kv, 2, HEAD_DIM, page),
                                       lambda b, j, pt: (pt[b, j * per_step + u], 0, 0, 0, 0))
    return pl.pallas_call(
        functools.partial(_pool_paged_body, per_step=per_step, n_kv=n_kv, page=page),
        grid_spec=pltpu.PrefetchScalarGridSpec(
            num_scalar_prefetch=1,
            grid=(n, n_pg // per_step),
            in_specs=[pl.BlockSpec(wt.shape, lambda b, j, pt: (0, 0, 0))]
                     + [page_spec(u) for u in range(per_step)],
            out_specs=pl.BlockSpec((1, n_kv, 2, HEAD_DIM, width), lambda b, j, pt: (b, 0, 0, 0, 0))),
        out_shape=jax.ShapeDtypeStruct((n, n_kv, 2, HEAD_DIM, width), F32),
        compiler_params=_params("parallel", "arbitrary"),
        name="nsa_pool_paged",
    )(page_table, wt, *([cache_t] * per_step))


def _topk_mask_t(score_t, n_sel):
    width, cols = score_t.shape
    cand = _iota((width, cols), 0).astype(F32)
    sel = jnp.zeros((width, cols), F32)
    sc = score_t
    for _ in range(n_sel):
        mx = jnp.max(sc, axis=0, keepdims=True)
        idx = jnp.min(jnp.where(sc == mx, cand, float(width)), axis=0, keepdims=True)
        hit = cand == idx
        sel = jnp.where(hit, 1.0, sel)
        sc = jnp.where(hit, -4.0, sc)
    return sel


def _sel_score(imp, blk, q_pos, g_heads):
    cur = _fdiv(q_pos, B_SEL)
    forced = (blk == 0) | (blk == cur) | (blk == cur - 1)
    return jnp.where(blk * B_SEL <= q_pos, imp + jnp.where(forced, g_heads + 1.0, 0.0), -1.0)


def _nsa_cmp_body(slopes_ref, q_ref, ce_ref, co_ref, oc_ref, selt_ref, any_ref, *, tq, g_heads, n_sel):
    k = pl.program_id(1)
    i = pl.program_id(2)
    nbs = ce_ref.shape[1]
    q_pos = i * tq + _iota((tq, 1), 0)
    blk = _iota((1, nbs), 1)
    dist_e = q_pos - ((2 * blk + 1) * B_CMP - 1)
    dist_o = q_pos - ((2 * blk + 2) * B_CMP - 1)
    ok_e, ok_o = dist_e >= 0, dist_o >= 0
    de, do = dist_e.astype(F32), dist_o.astype(F32)
    ce = ce_ref[0].astype(BF16)
    co = co_ref[0].astype(BF16)
    ke, ve, ko, vo = ce[:, :HEAD_DIM], ce[:, HEAD_DIM:], co[:, :HEAD_DIM], co[:, HEAD_DIM:]
    imp_e = jnp.zeros((tq, nbs), F32)
    imp_o = jnp.zeros((tq, nbs), F32)
    for g in range(g_heads):
        slope = slopes_ref[k * g_heads + g]
        lo, hi = g * HEAD_DIM, (g + 1) * HEAD_DIM
        qg = q_ref[0, :, lo:hi]
        se = jnp.where(ok_e, _dot_nt(qg, ke) - slope * de, NEG)
        so = jnp.where(ok_o, _dot_nt(qg, ko) - slope * do, NEG)
        m = jnp.maximum(jnp.max(se, axis=-1, keepdims=True), jnp.max(so, axis=-1, keepdims=True))
        ee = jnp.where(ok_e, jnp.exp(se - m), 0.0)
        eo = jnp.where(ok_o, jnp.exp(so - m), 0.0)
        den = jnp.sum(ee, axis=-1, keepdims=True) + jnp.sum(eo, axis=-1, keepdims=True)
        den = jnp.where(den > 0, den, 1.0)
        pe, po = ee / den, eo / den
        oc_ref[0, :, lo:hi] = _dot(pe.astype(BF16), ve) + _dot(po.astype(BF16), vo)
        imp_e = imp_e + pe
        imp_o = imp_o + po
    score = _sel_score(imp_e + imp_o, blk, q_pos, g_heads)
    sel_t = _topk_mask_t(score.T, n_sel)
    selt_ref[0, 0] = sel_t.astype(BF16)
    any_ref[0, 0, 0] = jnp.broadcast_to(jnp.max(sel_t, axis=1, keepdims=True), (nbs, KV_LANES))


def _nsa_cmp(q, cmp_e, cmp_o, slopes, *, n_heads, n_kv, tq):
    n, s, _ = q.shape
    g_heads = n_heads // n_kv
    gw = g_heads * HEAD_DIM
    nbs = cmp_e.shape[1]
    n_sel = min(B_TOPN, nbs)
    assert s % tq == 0
    return pl.pallas_call(
        functools.partial(_nsa_cmp_body, tq=tq, g_heads=g_heads, n_sel=n_sel),
        grid=(n, n_kv, s // tq),
        in_specs=[_smem(),
                  pl.BlockSpec((1, tq, gw), lambda b, k, i: (b, i, k)),
                  pl.BlockSpec((1, nbs, KV_LANES), lambda b, k, i: (b, 0, k)),
                  pl.BlockSpec((1, nbs, KV_LANES), lambda b, k, i: (b, 0, k))],
        out_specs=[pl.BlockSpec((1, tq, gw), lambda b, k, i: (b, i, k)),
                   pl.BlockSpec((1, 1, nbs, tq), lambda b, k, i: (b, k, 0, i)),
                   pl.BlockSpec((1, 1, 1, nbs, KV_LANES), lambda b, k, i: (b, k, i, 0, 0))],
        out_shape=[jax.ShapeDtypeStruct((n, s, n_heads * HEAD_DIM), F32),
                   jax.ShapeDtypeStruct((n, n_kv, nbs, s), BF16),
                   jax.ShapeDtypeStruct((n, n_kv, s // tq, nbs, KV_LANES), F32)],
        compiler_params=_params("parallel", "parallel", "parallel"),
        name="nsa_cmp_select",
    )(slopes, q, cmp_e, cmp_o)


def _flash_update(s, valid, vb, m_ref, l_ref, acc_ref, idx, v_transposed=False):
    s = jnp.where(valid, s, NEG)
    m_prev = m_ref[idx]
    m_new = jnp.maximum(m_prev, jnp.max(s, axis=-1, keepdims=True))
    alpha = jnp.exp(m_prev - m_new)
    e = jnp.where(valid, jnp.exp(s - m_new), 0.0)
    l_ref[idx] = alpha * l_ref[idx] + jnp.sum(e, axis=-1, keepdims=True)
    pv = _dot_nt(e.astype(BF16), vb) if v_transposed else _dot(e.astype(BF16), vb)
    acc_ref[idx] = alpha * acc_ref[idx] + pv
    m_ref[idx] = m_new


def _flash_init(m_ref, l_ref, acc_ref):
    m_ref[...] = jnp.full(m_ref.shape, NEG, F32)
    l_ref[...] = jnp.zeros(l_ref.shape, F32)
    acc_ref[...] = jnp.zeros(acc_ref.shape, F32)


def _flash_result(l_ref, acc_ref, idx):
    l = l_ref[idx]
    return acc_ref[idx] / jnp.where(l > 0, l, 1.0)


MASK_BIG = 1e30
M_INIT = -3e38


def _nsa_sel_body(flags_ref, slopes_ref, qt_ref, selt_ref, kaug_ref, kvt_ref, o_ref,
                  qa_ref, bias_ref, m_ref, l_ref, acc_ref, *, t, g_heads, n_kv, n_q):
    b = pl.program_id(0)
    k = pl.program_id(1)
    i = pl.program_id(2)
    width = g_heads * t
    selbias = ((selt_ref[0, 0].astype(F32) - 1.0) * MASK_BIG).astype(BF16)
    r_minus_c = (_iota((1, t), 1) - _iota((t, 1), 0)).astype(F32)
    lane_head = _fdiv(_iota((1, width), 1), t)
    slope = jnp.zeros((1, width), F32)
    for g in range(g_heads):
        cols = slice(g * t, (g + 1) * t)
        qa_ref[0:HEAD_DIM, cols] = qt_ref[0, g]
        qa_ref[HEAD_DIM:KV_LANES, cols] = jnp.zeros((HEAD_DIM, t), BF16)
        qa_ref[KV_LANES:, cols] = selbias
        bias_ref[:, cols] = slopes_ref[k * g_heads + g] * r_minus_c
        slope = jnp.where(lane_head == g, slopes_ref[k * g_heads + g], slope)
    m_ref[...] = jnp.full(m_ref.shape, M_INIT, F32)
    l_ref[...] = jnp.zeros(l_ref.shape, F32)
    acc_ref[...] = jnp.zeros(acc_ref.shape, F32)

    def chunk(c, diagonal):
        k0 = pl.multiple_of(c * t, t)
        ka = kaug_ref[0, pl.ds(k0, t), :]
        kvt = kvt_ref[0, 0, :, pl.ds(k0, t)]
        s = _dot(ka, qa_ref[...]) - bias_ref[...]
        if diagonal:
            causal = jnp.where(_iota((t, 1), 0) > _iota((1, t), 1), NEG, 0.0)
            s = s + jnp.concatenate([causal] * g_heads, axis=1)
        delta = slope * jnp.full((1, width), (i - c) * t, jnp.int32).astype(F32)
        m_prev = m_ref[...]
        m_new = jnp.maximum(m_prev, jnp.max(s, axis=0, keepdims=True) - delta)
        alpha = jnp.exp(m_prev - m_new)
        e = jnp.exp(s - (m_new + delta))
        l_ref[...] = alpha * l_ref[...] + jnp.sum(e, axis=0, keepdims=True)
        acc_ref[...] = alpha * acc_ref[...] + _dot(kvt, e.astype(BF16))
        m_ref[...] = m_new

    def maybe_chunk(c, carry):
        @pl.when(flags_ref[((b * n_kv + k) * n_q + i) * n_q + c] != 0)
        def _():
            chunk(c, False)
        return carry

    lax.fori_loop(0, i, maybe_chunk, 0)
    chunk(i, True)
    l = l_ref[...]
    o_t = acc_ref[HEAD_DIM:KV_LANES, :] / jnp.where(l > 0, l, 1.0)
    for g in range(g_heads):
        o_ref[0, :, g * HEAD_DIM:(g + 1) * HEAD_DIM] = o_t[:, g * t:(g + 1) * t].T


def _nsa_sel(qt, selt, kaug, kvt, flags, slopes, *, n_heads, n_kv, t):
    n, _, _, s = qt.shape
    g_heads = n_heads // n_kv
    gw = g_heads * HEAD_DIM
    nbs = selt.shape[2]
    aug = KV_LANES + nbs
    n_q = s // t
    assert s % t == 0 and t % B_SEL == 0
    return pl.pallas_call(
        functools.partial(_nsa_sel_body, t=t, g_heads=g_heads, n_kv=n_kv, n_q=n_q),
        grid_spec=pltpu.PrefetchScalarGridSpec(
            num_scalar_prefetch=1,
            grid=(n, n_kv, n_q),
            in_specs=[_smem(),
                      pl.BlockSpec((1, g_heads, HEAD_DIM, t), lambda b, k, i, fl: (b, k, 0, i)),
                      pl.BlockSpec((1, 1, nbs, t), lambda b, k, i, fl: (b, k, 0, i)),
                      pl.BlockSpec((1, s, aug), lambda b, k, i, fl: (b, 0, k)),
                      pl.BlockSpec((1, 1, KV_LANES, s), lambda b, k, i, fl: (b, k, 0, 0))],
            out_specs=pl.BlockSpec((1, t, gw), lambda b, k, i, fl: (b, i, k)),
            scratch_shapes=[pltpu.VMEM((aug, g_heads * t), BF16), pltpu.VMEM((t, g_heads * t), F32),
                            pltpu.VMEM((1, g_heads * t), F32), pltpu.VMEM((1, g_heads * t), F32),
                            pltpu.VMEM((KV_LANES, g_heads * t), F32)]),
        out_shape=jax.ShapeDtypeStruct((n, s, n_heads * HEAD_DIM), F32),
        compiler_params=_params("parallel", "parallel", "parallel"),
        name="nsa_selected",
    )(flags, slopes, qt, selt, kaug, kvt)


T_PAD = 8
NEW_PAD = 16
KEY_PAD = 128


def _row_select(ref, base, g_heads, rows):
    gi = _fdiv(_iota((rows, 1), 0), T_PAD)
    out = jnp.zeros((rows, 1), F32)
    for g in range(g_heads):
        out = jnp.where(gi == g, ref[base + g], out)
    return out


def _pad_rows(x, lo, total):
    return jnp.pad(x, ((0, 0), (lo, total - lo - x.shape[1]), (0, 0)))


def _dec_window_body(*refs, lb, t_new, window, dilation, past_len, g_heads, n_kv, has_sink, with_lse):
    it = iter(refs)
    slopes_ref = next(it)
    sinks_ref = next(it) if has_sink else None
    q_ref, new_lo_ref, new_hi_ref, buf_ref = next(it), next(it), next(it), next(it)
    o_ref = next(it)
    lse_ref = next(it) if with_lse else None
    nbuf_ref = next(it)

    rows = g_heads * T_PAD
    span = lb + KEY_PAD
    t = _fmod(_iota((rows, 1), 0), T_PAD)
    r = _iota((1, span), 1)
    dist = lb + t - r
    valid = (dist >= 0) & (dist <= window) & (past_len - lb + r >= 0)
    if dilation > 1:
        valid = valid & (_fmod(dist, dilation) == 0)
    distf = dist.astype(F32)
    is_new = _iota((1, KEY_PAD), 1) >= KEY_PAD - t_new
    for k in range(n_kv):
        kt_buf, vt_buf = buf_ref[0, k, 0], buf_ref[0, k, 1]
        kt = jnp.concatenate([kt_buf, new_lo_ref[0, k, 0]], axis=1).astype(BF16)
        vt = jnp.concatenate([vt_buf, new_lo_ref[0, k, 1]], axis=1).astype(BF16)
        slope = _row_select(slopes_ref, k * g_heads, g_heads, rows)
        sink = _row_select(sinks_ref, k * g_heads, g_heads, rows) if has_sink else None
        s = _dot(q_ref[0, k], kt) - slope * distf
        p, m, den = _softmax_parts(s, valid, sink)
        o_ref[0, k] = _dot_nt(p.astype(BF16), vt)
        if with_lse:
            lse_ref[0, k] = jnp.broadcast_to(m + jnp.log(den), (rows, HEAD_DIM))
        for x, old in ((0, kt_buf), (1, vt_buf)):
            shifted = pltpu.roll(old, lb - t_new, axis=1)
            nbuf_ref[0, k, x] = shifted
            nbuf_ref[0, k, x, :, lb - KEY_PAD:lb] = jnp.where(is_new, new_hi_ref[0, k, x],
                                                              shifted[:, lb - KEY_PAD:lb])


def _state_view(buf):
    return jnp.transpose(buf, (0, 2, 3, 4, 1))


def _state_unview(buf_t):
    return jnp.transpose(buf_t, (0, 4, 1, 2, 3))


def _dec_window(qd, kv_new, buf, slopes, *, window, dilation, past_len, sinks=None, with_lse=False):
    n, n_kv, rows, _ = qd.shape
    g_heads = rows // T_PAD
    t_new = kv_new.shape[1]
    lb = buf.shape[1]
    assert lb % KEY_PAD == 0 and t_new <= T_PAD
    new_t = kv_new.reshape(n, t_new, n_kv, 2, HEAD_DIM).transpose(0, 2, 3, 4, 1)
    lane_pad = lambda lo: jnp.pad(new_t, ((0, 0),) * 4 + ((lo, KEY_PAD - t_new - lo),))
    new_lo, new_hi = lane_pad(0), lane_pad(KEY_PAD - t_new)
    buf_t = _state_view(buf)
    ins = [slopes]
    in_specs = [_smem()]
    if sinks is not None:
        ins.append(sinks)
        in_specs.append(_smem())
    ins += [qd, new_lo, new_hi, buf_t]
    o_spec = pl.BlockSpec((1, n_kv, rows, HEAD_DIM), lambda b: (b, 0, 0, 0))
    new_spec = pl.BlockSpec((1, n_kv, 2, HEAD_DIM, KEY_PAD), lambda b: (b, 0, 0, 0, 0))
    buf_spec = pl.BlockSpec((1, n_kv, 2, HEAD_DIM, lb), lambda b: (b, 0, 0, 0, 0))
    in_specs += [o_spec, new_spec, new_spec, buf_spec]
    o_sds = jax.ShapeDtypeStruct((n, n_kv, rows, HEAD_DIM), F32)
    out_specs = [o_spec] + ([o_spec] if with_lse else []) + [buf_spec]
    out_shape = [o_sds] + ([o_sds] if with_lse else []) + [jax.ShapeDtypeStruct(buf_t.shape, F32)]
    outs = pl.pallas_call(
        functools.partial(_dec_window_body, lb=lb, t_new=t_new, window=window, dilation=dilation,
                          past_len=past_len, g_heads=g_heads, n_kv=n_kv, has_sink=sinks is not None,
                          with_lse=with_lse),
        grid=(n,),
        in_specs=in_specs,
        out_specs=out_specs,
        out_shape=out_shape,
        compiler_params=_params("parallel"),
        name="decode_window",
    )(*ins)
    return tuple(outs[:-1]) + (_state_unview(outs[-1]),)


def _nsa_dec_cmp_body(slopes_ref, q_ref, cmpt_ref, cnew_ref, pnew_ref, oc_ref, sel_ref, *,
                      past_len, g_heads, n_kv, n_sel):
    rows = g_heads * T_PAD
    width = cmpt_ref.shape[4]
    nbs = width // 2
    t = _fmod(_iota((rows, 1), 0), T_PAD)
    q_pos = past_len + t
    lane = _iota((1, width), 1)
    cmp_blk = jnp.where(lane < nbs, 2 * lane, 2 * (lane - nbs) + 1)
    dist_c = q_pos - ((cmp_blk + 1) * B_CMP - 1)
    lane_n = _iota((1, NEW_PAD), 1)
    dist_ne = q_pos - ((2 * nbs + 1) * B_CMP - 1)
    dist_no = q_pos - ((2 * nbs + 2) * B_CMP - 1)
    cnew = cnew_ref[0].astype(BF16)
    is_k = _fmod(_iota((1, cnew.shape[1]), 1), KV_LANES) < HEAD_DIM
    new_e = jnp.where(is_k, _dot(pnew_ref[0], cnew), _dot(pnew_ref[1], cnew)).astype(BF16)
    new_o = jnp.zeros_like(new_e)
    new_sets = ((new_e, dist_ne, (dist_ne >= 0) & (lane_n == 0)),
                (new_o, dist_no, (dist_no >= 0) & (lane_n == 0)))
    ok_c = dist_c >= 0
    q_pos8 = past_len + _iota((T_PAD, 1), 0)
    blk = _iota((1, nbs), 1)
    lane_p = _iota((1, KEY_PAD), 1)
    scores = []
    for k in range(n_kv):
        c0 = k * KV_LANES
        slope = _row_select(slopes_ref, k * g_heads, g_heads, rows)
        q = q_ref[0, k]
        kt = cmpt_ref[0, k, 0].astype(BF16)
        vt = cmpt_ref[0, k, 1].astype(BF16)
        ss = [jnp.where(ok_c, _dot(q, kt) - slope * dist_c.astype(F32), NEG)]
        oks = [ok_c]
        for keys, dist, ok in new_sets:
            ss.append(jnp.where(ok, _dot_nt(q, keys[:, c0:c0 + HEAD_DIM]) - slope * dist.astype(F32), NEG))
            oks.append(ok)
        m = functools.reduce(jnp.maximum, [jnp.max(s, axis=-1, keepdims=True) for s in ss])
        es = [jnp.where(ok, jnp.exp(s - m), 0.0) for s, ok in zip(ss, oks)]
        den = _sum([jnp.sum(e, axis=-1, keepdims=True) for e in es])
        den = jnp.where(den > 0, den, 1.0)
        ps = [e / den for e in es]
        oc_ref[0, k] = (_dot_nt(ps[0].astype(BF16), vt)
                        + _sum([_dot(p.astype(BF16), keys[:, c0 + HEAD_DIM:c0 + KV_LANES])
                                for p, (keys, _, _) in zip(ps[1:], new_sets)]))
        gsum = lambda p: _sum([p[g * T_PAD:(g + 1) * T_PAD] for g in range(g_heads)])
        imp_c = gsum(ps[0])
        imp = imp_c[:, :nbs] + imp_c[:, nbs:]
        imp_new = gsum(ps[1]) + gsum(ps[2])
        score = _sel_score(imp, blk, q_pos8, g_heads)
        score_new = _sel_score(imp_new[:, 0:1], nbs + lane_p, q_pos8, g_heads)
        score_new = jnp.where(lane_p == 0, score_new, -3.0)
        scores.append(jnp.concatenate([score, score_new], axis=1))
    sel = _topk_mask_t(jnp.concatenate(scores, axis=0).T, n_sel).T
    for k in range(n_kv):
        sel_ref[0, k] = sel[k * T_PAD:(k + 1) * T_PAD]


def _nsa_dec_cmp(qd, cmpt, c_new, pnew, slopes, *, past_len):
    n, n_kv, rows, _ = qd.shape
    g_heads = rows // T_PAD
    width = cmpt.shape[4]
    nbs = width // 2
    c = c_new.shape[2]
    n_sel = min(B_TOPN, nbs + 1)
    cnew = _pad_rows(c_new, 0, NEW_PAD)
    q_spec = pl.BlockSpec((1, n_kv, rows, HEAD_DIM), lambda b: (b, 0, 0, 0))
    return pl.pallas_call(
        functools.partial(_nsa_dec_cmp_body, past_len=past_len, g_heads=g_heads, n_kv=n_kv, n_sel=n_sel),
        grid=(n,),
        in_specs=[_smem(), q_spec,
                  pl.BlockSpec((1, n_kv, 2, HEAD_DIM, width), lambda b: (b, 0, 0, 0, 0)),
                  pl.BlockSpec((1, NEW_PAD, c), lambda b: (b, 0, 0)),
                  pl.BlockSpec(pnew.shape, lambda b: (0, 0, 0))],
        out_specs=[q_spec, pl.BlockSpec((1, n_kv, T_PAD, nbs + KEY_PAD), lambda b: (b, 0, 0, 0))],
        out_shape=[jax.ShapeDtypeStruct((n, n_kv, rows, HEAD_DIM), F32),
                   jax.ShapeDtypeStruct((n, n_kv, T_PAD, nbs + KEY_PAD), F32)],
        compiler_params=_params("parallel"),
        name="nsa_decode_cmp_select",
    )(slopes, qd, cmpt, cnew, pnew)


def _nsa_dec_sel_body(*refs, past_len, g_heads, n_kv, per_step, page):
    logical_ref, count_ref = refs[1], refs[2]
    slopes_ref, q_ref, sel_ref, snew_ref = refs[3:7]
    page_refs = refs[7:7 + per_step]
    o_ref, m_ref, l_ref, acc_ref = refs[7 + per_step:]
    rows = g_heads * T_PAD
    b = pl.program_id(0)
    j = pl.program_id(1)
    nsel = sel_ref.shape[3]
    t = _fmod(_iota((rows, 1), 0), T_PAD)
    q_pos = past_len + t

    @pl.when(j == 0)
    def _():
        _flash_init(m_ref, l_ref, acc_ref)

    @pl.when(j * per_step < count_ref[b])
    def _():
        lane = _iota((1, page), 1)
        k_pos = jnp.concatenate([logical_ref[b, j * per_step + u] * page + lane for u in range(per_step)], axis=1)
        live = jnp.concatenate([jnp.full((1, page), j * per_step + u, jnp.int32) for u in range(per_step)],
                               axis=1) < count_ref[b]
        expand = (_iota((nsel, 1), 0) == _fdiv(k_pos, B_SEL)).astype(BF16)
        distf = (q_pos - k_pos).astype(F32)
        for k in range(n_kv):
            kt = jnp.concatenate([r[0, k, 0].astype(BF16) for r in page_refs], axis=1)
            vt = jnp.concatenate([r[0, k, 1].astype(BF16) for r in page_refs], axis=1)
            sel = jnp.concatenate([sel_ref[0, k]] * g_heads, axis=0).astype(BF16)
            valid = (_dot(sel, expand) > 0.5) & live & (k_pos <= q_pos)
            slope = _row_select(slopes_ref, k * g_heads, g_heads, rows)
            s = _dot(q_ref[0, k], kt) - slope * distf
            _flash_update(s, valid, vt, m_ref, l_ref, acc_ref, k, v_transposed=True)

    @pl.when(j == pl.num_programs(1) - 1)
    def _():
        n_past = past_len // B_SEL
        snew = snew_ref[0].astype(BF16)
        dist = t - _iota((1, NEW_PAD), 1)
        for k in range(n_kv):
            c0 = k * KV_LANES
            sel = jnp.concatenate([sel_ref[0, k]] * g_heads, axis=0)
            valid = (sel[:, n_past:n_past + 1] > 0.5) & (dist >= 0)
            slope = _row_select(slopes_ref, k * g_heads, g_heads, rows)
            s = _dot_nt(q_ref[0, k], snew[:, c0:c0 + HEAD_DIM]) - slope * dist.astype(F32)
            _flash_update(s, valid, snew[:, c0 + HEAD_DIM:c0 + KV_LANES], m_ref, l_ref, acc_ref, k)
            o_ref[0, k] = _flash_result(l_ref, acc_ref, k)


def _needed_pages(sel, page_table, t_new, page):
    n, n_pg = page_table.shape
    per_page = page // B_SEL
    picked = sel[:, :, :t_new, :n_pg * per_page].max(axis=(1, 2)) > 0
    need = picked.reshape(n, n_pg, per_page).any(axis=-1)
    count = need.sum(axis=1).astype(jnp.int32)
    order = jnp.argsort(jnp.logical_not(need), axis=1, stable=True).astype(jnp.int32)
    slot = jnp.minimum(jnp.arange(n_pg, dtype=jnp.int32)[None, :], jnp.maximum(count[:, None] - 1, 0))
    logical = jnp.take_along_axis(order, slot, axis=1)
    physical = jnp.take_along_axis(page_table, logical, axis=1)
    return physical, logical, count


def _nsa_dec_sel(qd, sel, s_new, cache_t, page_table, slopes, *, past_len, per_step=4):
    n, n_kv, rows, _ = qd.shape
    g_heads = rows // T_PAD
    page = cache_t.shape[4]
    t_new, c = s_new.shape[1:]
    n_pg = page_table.shape[1]
    per_step = min(per_step, n_pg)
    assert n_pg % per_step == 0 and past_len == n_pg * page
    snew = _pad_rows(s_new, 0, NEW_PAD)
    physical, logical, count = _needed_pages(sel, page_table, t_new, page)
    q_spec = pl.BlockSpec((1, n_kv, rows, HEAD_DIM), lambda b, j, *_: (b, 0, 0, 0))
    page_spec = lambda u: pl.BlockSpec((1, n_kv, 2, HEAD_DIM, page),
                                       lambda b, j, phys, *_: (phys[b, j * per_step + u], 0, 0, 0, 0))
    return pl.pallas_call(
        functools.partial(_nsa_dec_sel_body, past_len=past_len, g_heads=g_heads, n_kv=n_kv,
                          per_step=per_step, page=page),
        grid_spec=pltpu.PrefetchScalarGridSpec(
            num_scalar_prefetch=3,
            grid=(n, n_pg // per_step),
            in_specs=[_smem(), q_spec,
                      pl.BlockSpec((1, n_kv, T_PAD, sel.shape[3]), lambda b, j, *_: (b, 0, 0, 0)),
                      pl.BlockSpec((1, NEW_PAD, c), lambda b, j, *_: (b, 0, 0))]
                     + [page_spec(u) for u in range(per_step)],
            out_specs=q_spec,
            scratch_shapes=[pltpu.VMEM((n_kv, rows, 1), F32), pltpu.VMEM((n_kv, rows, 1), F32),
                            pltpu.VMEM((n_kv, rows, HEAD_DIM), F32)]),
        out_shape=jax.ShapeDtypeStruct((n, n_kv, rows, HEAD_DIM), F32),
        compiler_params=_params("parallel", "arbitrary"),
        name="nsa_decode_selected",
    )(physical, logical, count, slopes, qd, sel, snew, *([cache_t] * per_step))


def _nsa_merge_body(oc_ref, os_ref, ow_ref, bgz_ref, o_ref, *, n_heads):
    bg = jax.nn.sigmoid(bgz_ref[...])
    for h in range(n_heads):
        lo, hi = h * HEAD_DIM, (h + 1) * HEAD_DIM
        o_ref[:, lo:hi] = (bg[:, h:h + 1] * oc_ref[:, lo:hi]
                           + bg[:, n_heads + h:n_heads + h + 1] * os_ref[:, lo:hi]
                           + bg[:, 2 * n_heads + h:2 * n_heads + h + 1] * ow_ref[:, lo:hi])


def _nsa_merge(o_c, o_s, o_w, bgz, n_heads):
    m_rows, d = o_c.shape
    full = lambda a: pl.BlockSpec(a.shape, lambda i: (0, 0))
    return pl.pallas_call(
        functools.partial(_nsa_merge_body, n_heads=n_heads),
        grid=(1,),
        in_specs=[full(o_c), full(o_s), full(o_w), full(bgz)],
        out_specs=full(o_c),
        out_shape=jax.ShapeDtypeStruct((m_rows, d), F32),
        compiler_params=_params("arbitrary"),
        name="nsa_decode_merge",
    )(o_c, o_s, o_w, bgz)


def _to_decode_rows(q, n_kv):
    n, t, hd = q.shape
    g_heads = hd // HEAD_DIM // n_kv
    q = q.reshape(n, t, n_kv, g_heads, HEAD_DIM).transpose(0, 2, 3, 1, 4)
    q = jnp.pad(q, ((0, 0), (0, 0), (0, 0), (0, T_PAD - t), (0, 0)))
    return q.reshape(n, n_kv, g_heads * T_PAD, HEAD_DIM)


def _from_decode_rows(o, t):
    n, n_kv, rows, _ = o.shape
    g_heads = rows // T_PAD
    o = o.reshape(n, n_kv, g_heads, T_PAD, HEAD_DIM)[:, :, :, :t]
    return o.transpose(0, 3, 1, 2, 4).reshape(n * t, n_kv * g_heads * HEAD_DIM)


def _kv_state(rows, n_kv):
    return rows.reshape(*rows.shape[:-1], n_kv, 2, HEAD_DIM)


def _mixer_a(xp, xs, buf, g_pre, g_post, w_in, w_out, sinks, past_len, tm):
    n, s, d = xp.shape
    m, t, _ = xs.shape
    n_heads = d // HEAD_DIM
    n_kv = n_heads // 8
    nq, nkv = n_heads * HEAD_DIM, n_kv * KV_LANES
    slopes = _alibi_slopes(n_heads)
    w_in_b, w_out_b = w_in.astype(BF16), w_out.astype(BF16)
    splits = [(0, nq, "q"), (nq, nkv, "f32"), (nq, nkv, "bf16"), (nq + nkv, nq, "f32")]
    q, kv, kvb, gate = _proj(xp.reshape(n * s, d), g_pre, w_in_b, splits, tm)
    o = _banded(q.reshape(n, s, nq), kvb.reshape(n, s, nkv), slopes, n_heads=n_heads, n_kv=n_kv,
                window=A_WINDOW, stride=1, sinks=sinks)
    yp = _outproj([o.reshape(n * s, nq)], [], gate, xp.reshape(n * s, d), g_post, w_out_b, tm)
    keep = min(A_WINDOW, s)
    state_p = _kv_state(kv.reshape(n, s, nkv)[:, s - keep:], n_kv)
    qs, kvs, _, gs = _proj(xs.reshape(m * t, d), g_pre, w_in_b, splits, tm)
    od, nbuf = _dec_window(_to_decode_rows(qs.reshape(m, t, nq), n_kv), kvs.reshape(m, t, nkv), buf, slopes,
                           window=A_WINDOW, dilation=1, past_len=past_len, sinks=sinks)
    ys = _outproj([_from_decode_rows(od, t)], [], gs, xs.reshape(m * t, d), g_post, w_out_b, tm)
    return yp.reshape(n, s, d), ys.reshape(m, t, d), [state_p], [nbuf]


def _mixer_b(xp, xs, cache_cmp, cache_sel, buf_win, page_table, g_pre, g_post, w_in, w_out, cmp_pool,
             past_len, tm):
    n, s, d = xp.shape
    m, t, _ = xs.shape
    n_heads = d // HEAD_DIM
    n_kv = n_heads // 4
    nq, nkv = n_heads * HEAD_DIM, n_kv * KV_LANES
    n_bg = 3 * n_heads
    slopes = _alibi_slopes(n_heads)
    w_in_b, w_out_b = w_in.astype(BF16), w_out.astype(BF16)
    pmats = _pool_mats(cmp_pool)
    c0, s0, w0, g0, b0 = nq, nq + nkv, nq + 2 * nkv, nq + 3 * nkv, 2 * nq + 3 * nkv
    base = [(0, nq, "q"), (c0, nkv, "f32"), (s0, nkv, "f32"), (w0, nkv, "f32"), (g0, nq, "f32"),
            (b0, n_bg, "f32")]
    splits = [(0, nq, "q"), (c0, nkv, "f32"), (c0, nkv, "f32t"), (s0, nkv, "f32t"), (w0, nkv, "f32"),
              (g0, nq, "f32"), (b0, n_bg, "f32"), (s0, nkv, "bf16"), (s0, nkv, "kaug"), (w0, nkv, "bf16")]
    q, c_p, c_pt, s_pt, w_p, gate, bgz, s_pb, kaug, w_pb = _proj(xp.reshape(n * s, d), g_pre, w_in_b, splits, tm,
                                                                 seq_len=s)
    q = q.reshape(n, s, nq)
    cmp_e, cmp_o = _pool_prompt(c_p.reshape(n, s, nkv), pmats)
    t_sel = min(256, s)
    o_c, selt, sel_any = _nsa_cmp(q, cmp_e, cmp_o, slopes, n_heads=n_heads, n_kv=n_kv, tq=t_sel)
    per_tile = t_sel // B_SEL
    flags = (sel_any[:, :, :, :, 0].reshape(n, n_kv, s // t_sel, s // t_sel, per_tile).max(axis=-1) > 0)
    flags = flags.astype(jnp.int32).reshape(-1)
    qt = q.reshape(n, s, n_heads, HEAD_DIM).transpose(0, 2, 3, 1)
    kvt = s_pb.reshape(n, s, n_kv, KV_LANES).transpose(0, 2, 3, 1)
    o_s = _nsa_sel(qt, selt, kaug.reshape(n, s, -1), kvt, flags, slopes, n_heads=n_heads, n_kv=n_kv, t=t_sel)
    o = _banded(q, w_pb.reshape(n, s, nkv), slopes, n_heads=n_heads, n_kv=n_kv, window=B_WINDOW, stride=1,
                merge=(o_c, o_s, bgz.reshape(n, s, n_bg)))
    yp = _outproj([o.reshape(n * s, nq)], [], gate, xp.reshape(n * s, d), g_post, w_out_b, tm)
    keep = min(B_WINDOW, s)
    from_t = lambda a: a.reshape(n, n_kv, 2, HEAD_DIM, s).transpose(0, 4, 1, 2, 3)
    states_p = [from_t(c_pt), from_t(s_pt), _kv_state(w_p.reshape(n, s, nkv)[:, s - keep:], n_kv)]
    qs, c_s, s_s, w_s, gs, bgzs = _proj(xs.reshape(m * t, d), g_pre, w_in_b, base, tm)
    qd = _to_decode_rows(qs.reshape(m, t, nq), n_kv)
    c_s, s_s, w_s = (a.reshape(m, t, nkv) for a in (c_s, s_s, w_s))
    cmpt = _pool_paged(_pages_view(cache_cmp), page_table, cmp_pool)
    n_w = min(NEW_PAD, B_CMP)
    pnew = jnp.zeros((2, NEW_PAD, NEW_PAD), F32).at[:, 0, :n_w].set(cmp_pool.astype(F32)[:, :n_w]).astype(BF16)
    oc_d, sel_d = _nsa_dec_cmp(qd, cmpt, c_s, pnew, slopes, past_len=past_len)
    os_d = _nsa_dec_sel(qd, sel_d, s_s, _pages_view(cache_sel), page_table, slopes, past_len=past_len)
    ow_d, nwin = _dec_window(qd, w_s, buf_win, slopes, window=B_WINDOW, dilation=1, past_len=past_len)
    osm = _nsa_merge(_from_decode_rows(oc_d, t), _from_decode_rows(os_d, t), _from_decode_rows(ow_d, t),
                     bgzs, n_heads)
    ys = _outproj([osm], [], gs, xs.reshape(m * t, d), g_post, w_out_b, tm)
    states_s = [_kv_state(c_s, n_kv), _kv_state(s_s, n_kv), nwin]
    return yp.reshape(n, s, d), ys.reshape(m, t, d), states_p, states_s


def _mixer_c(xp, xs, bufs, g_pre, g_post, w_in, w_out, past_len, tm):
    n, s, d = xp.shape
    m, t, _ = xs.shape
    n_heads = d // HEAD_DIM
    n_kv = n_heads // 8
    ng = len(C_GROUPS)
    nq1, nkv1 = n_heads * HEAD_DIM, n_kv * KV_LANES
    nq, nkv = ng * nq1, ng * nkv1
    slopes = _alibi_slopes(n_heads)
    w_in_b, w_out_b = w_in.astype(BF16), w_out.astype(BF16)
    splits = [(0, nq, "q"), (nq, nkv, "f32"), (nq, nkv, "bf16"), (nq + nkv, nq1, "f32")]
    view = lambda mode, dil: mode if dil == 1 else f"{mode}@{dil}"
    p_splits = ([(gi * nq1, nq1, view("q", dil)) for gi, (_, dil) in enumerate(C_GROUPS)]
                + [(nq + gi * nkv1, nkv1, view("bf16", dil)) for gi, (_, dil) in enumerate(C_GROUPS)]
                + [(nq, nkv, "f32"), (nq + nkv, nq1, "f32")])
    p_outs = _proj(xp.reshape(n * s, d), g_pre, w_in_b, p_splits, tm)
    q_views, kv_views, (kv, gate) = p_outs[:ng], p_outs[ng:2 * ng], p_outs[2 * ng:]
    kv = kv.reshape(n, s, nkv)
    outs, lses, states_p = [], [], []
    for gi, (w, dil) in enumerate(C_GROUPS):
        o, lse = _banded(q_views[gi].reshape(n, s // dil, dil * nq1), kv_views[gi].reshape(n, s // dil, dil * nkv1),
                         slopes, n_heads=n_heads, n_kv=n_kv, window=w, stride=dil, with_lse=True, viewed=True)
        outs.append(o.reshape(n * s, nq1))
        lses.append(lse.reshape(n * s, nq1))
        keep = min(w, s)
        states_p.append(_kv_state(kv[:, s - keep:, gi * nkv1:(gi + 1) * nkv1], n_kv))
    yp = _outproj(outs, lses, gate, xp.reshape(n * s, d), g_post, w_out_b, tm)
    qs, kvs, _, gs = _proj(xs.reshape(m * t, d), g_pre, w_in_b, splits, tm)
    qs, kvs = qs.reshape(m, t, nq), kvs.reshape(m, t, nkv)
    outs, lses, states_s = [], [], []
    for gi, (w, dil) in enumerate(C_GROUPS):
        buf = bufs[gi]
        assert buf.shape[1] == w
        od, ld, nbuf = _dec_window(_to_decode_rows(qs[:, :, gi * nq1:(gi + 1) * nq1], n_kv),
                                   kvs[:, :, gi * nkv1:(gi + 1) * nkv1], buf, slopes,
                                   window=w, dilation=dil, past_len=past_len, with_lse=True)
        outs.append(_from_decode_rows(od, t))
        lses.append(_from_decode_rows(ld, t))
        states_s.append(nbuf)
    ys = _outproj(outs, lses, gs, xs.reshape(m * t, d), g_post, w_out_b, tm)
    return yp.reshape(n, s, d), ys.reshape(m, t, d), states_p, states_s


def kernel(x_prompt, x_sample, state_a0_kv, cache_b1_cmp_kv, cache_b1_sel_kv, state_b1_win_kv,
           state_c2_g0_kv, state_c2_g1_kv, state_c2_g2_kv, state_a3_kv, page_table, norm_g,
           w_in_0, w_out_0, sinks_0, w_in_1, w_out_1, cmp_pool_1, w_in_2, w_out_2,
           w_in_3, w_out_3, sinks_3):
    past_len = page_table.shape[1] * cache_b1_cmp_kv.shape[1]
    tm = 256
    xp, xs = x_prompt, x_sample
    xp, xs, (a0_p,), (a0_s,) = _mixer_a(xp, xs, state_a0_kv, norm_g[0, 0], norm_g[0, 1], w_in_0, w_out_0,
                                        sinks_0, past_len, tm)
    xp, xs, (b1_cmp_p, b1_sel_p, b1_win_p), (b1_cmp_s, b1_sel_s, b1_win_s) = _mixer_b(
        xp, xs, cache_b1_cmp_kv, cache_b1_sel_kv, state_b1_win_kv, page_table, norm_g[1, 0], norm_g[1, 1],
        w_in_1, w_out_1, cmp_pool_1, past_len, tm)
    xp, xs, (c2_g0_p, c2_g1_p, c2_g2_p), (c2_g0_s, c2_g1_s, c2_g2_s) = _mixer_c(
        xp, xs, (state_c2_g0_kv, state_c2_g1_kv, state_c2_g2_kv), norm_g[2, 0], norm_g[2, 1], w_in_2, w_out_2,
        past_len, tm)
    xp, xs, (a3_p,), (a3_s,) = _mixer_a(xp, xs, state_a3_kv, norm_g[3, 0], norm_g[3, 1], w_in_3, w_out_3,
                                        sinks_3, past_len, tm)
    return (xp, xs, a0_p, a0_s, b1_cmp_p, b1_cmp_s, b1_sel_p, b1_sel_s, b1_win_p, b1_win_s,
            c2_g0_p, c2_g0_s, c2_g1_p, c2_g1_s, c2_g2_p, c2_g2_s, a3_p, a3_s)
```
